```python
import math
import jax, jax.numpy as jnp
from jax import lax
import numpy as np

D_MODEL = 1024
BATCH = 8
SEQ = 4096
DEPTH = 2

D_MIX = D_MODEL
D_DIFF = D_MIX // 2
D_RWKV = D_MIX - D_DIFF
DIFF_HEADS = 4
DIFF_HEAD_DIM = D_DIFF // DIFF_HEADS // 2
DIFF_V_DIM = 2 * DIFF_HEAD_DIM
RWKV_HEAD = 64
RWKV_HEADS = D_RWKV // RWKV_HEAD
LORA_W = 64
LORA_A = 64
LORA_V = 32
LORA_G = 160
D_FF = 4 * D_MODEL
N_BUCKETS = 32
MAX_DISTANCE = 128
Q_BLOCK = 128
LN_EPS = 1e-5
SUBLN_EPS = 1e-5
GN_EPS = 64e-5
ALPHA = (2 * DEPTH) ** 0.25
BETA = (8 * DEPTH) ** -0.25

N_DIFF = 3 * D_DIFF
RW_R = 0
RW_K = D_RWKV
RW_V = 2 * D_RWKV
RW_W = 3 * D_RWKV
RW_A = RW_W + LORA_W
RW_G = RW_A + LORA_A
N_RWKV_BASE = RW_G + LORA_G
N_RWKV_REST = N_RWKV_BASE + LORA_V
N_IN_FIRST = N_DIFF + N_RWKV_BASE
N_IN_REST = N_DIFF + N_RWKV_REST

kernel_name = "hybrid_diffattn_rwkv7_deepnorm"


def layer_norm(x, g, b, eps=LN_EPS):
    xf = x.astype(jnp.float32)
    mu = jnp.mean(xf, -1, keepdims=True)
    var = jnp.mean(jnp.square(xf - mu), -1, keepdims=True)
    return ((xf - mu) * lax.rsqrt(var + eps) * g + b).astype(x.dtype)


def t5_causal_bucket(dist):
    n = jnp.maximum(dist, 0)
    max_exact = N_BUCKETS // 2
    nf = jnp.maximum(n, 1).astype(jnp.float32)
    large = max_exact + (jnp.log(nf / max_exact) / math.log(MAX_DISTANCE / max_exact)
                         * (N_BUCKETS - max_exact)).astype(jnp.int32)
    large = jnp.minimum(large, N_BUCKETS - 1)
    return jnp.where(n < max_exact, n, large)


def diff_attention(q, k, v, rel_bias, lam, subln_g, lam_init):
    B, S = q.shape[0], q.shape[1]
    nb = S // Q_BLOCK
    scale = DIFF_HEAD_DIM ** -0.5
    qb = (q * scale).reshape(B, nb, Q_BLOCK, DIFF_HEADS, 2, DIFF_HEAD_DIM)
    qb = qb.transpose(1, 0, 3, 4, 2, 5)
    kt = k.transpose(0, 2, 3, 1, 4)
    vt = v.transpose(0, 2, 1, 3)
    k_pos = jnp.arange(S, dtype=jnp.int32)

    def block(args):
        q_blk, idx = args
        q_pos = idx * Q_BLOCK + jnp.arange(Q_BLOCK, dtype=jnp.int32)
        dist = q_pos[:, None] - k_pos[None, :]
        bias = jnp.transpose(rel_bias[t5_causal_bucket(dist)], (2, 0, 1)).astype(jnp.float32)
        logits = jnp.einsum('bhmqd,bhmkd->bhmqk', q_blk, kt).astype(jnp.float32)
        logits = logits + bias[None, :, None]
        logits = jnp.where((dist >= 0)[None, None, None], logits, -jnp.inf)
        probs = jax.nn.softmax(logits, axis=-1)
        attn = probs[:, :, 0] - lam * probs[:, :, 1]
        return jnp.einsum('bhqk,bhkd->bhqd', attn.astype(vt.dtype), vt)

    out = lax.map(block, (qb, jnp.arange(nb, dtype=jnp.int32)))
    out = out.transpose(1, 0, 3, 2, 4).reshape(B, S, DIFF_HEADS, DIFF_V_DIM).astype(jnp.float32)
    out = out * lax.rsqrt(jnp.mean(jnp.square(out), -1, keepdims=True) + SUBLN_EPS) * subln_g
    out = out * (1.0 - lam_init)
    return out.reshape(B, S, D_DIFF)


def wkv7_scan(r, w, k, v, a, b):
    B, _, H, N = r.shape

    def step(state, inp):
        r_t, w_t, k_t, v_t, a_t, b_t = inp
        sa = jnp.einsum('bhij,bhj->bhi', state, a_t)
        state = (state * w_t[:, :, None, :] + sa[..., None] * b_t[:, :, None, :]
                 + v_t[..., None] * k_t[:, :, None, :])
        y = jnp.einsum('bhij,bhj->bhi', state, r_t)
        return state, y

    xs = tuple(jnp.moveaxis(t, 1, 0) for t in (r, w, k, v, a, b))
    s0 = jnp.zeros((B, H, N, N), jnp.float32)
    _, ys = lax.scan(step, s0, xs)
    return jnp.moveaxis(ys, 0, 1)


def rwkv7_time_mix(r, k, v, xw, xa, xg, w0, w_up, a0, a_up, g_up, k_k, k_a, r_k, gn_g, gn_b):
    B, S, _ = r.shape
    f32 = jnp.float32
    r, k, v, xw, xa, xg = (t.astype(f32) for t in (r, k, v, xw, xa, xg))
    w = -jax.nn.softplus(-(w0 + jnp.tanh(xw) @ w_up)) - 0.5
    decay = jnp.exp(-jnp.exp(w))
    a = jax.nn.sigmoid(a0 + xa @ a_up)
    g = jax.nn.sigmoid(xg) @ g_up
    heads = lambda t: t.reshape(B, S, RWKV_HEADS, RWKV_HEAD)
    kk = heads(k * k_k)
    kk = kk / jnp.maximum(jnp.sqrt(jnp.sum(jnp.square(kk), -1, keepdims=True)), 1e-12)
    k = k * (1.0 + (a - 1.0) * k_a)
    rh, kh, vh, ah = heads(r), heads(k), heads(v), heads(a)
    y = wkv7_scan(rh, heads(decay), kh, vh, -kk, kk * ah)
    mu = jnp.mean(y, -1, keepdims=True)
    var = jnp.mean(jnp.square(y - mu), -1, keepdims=True)
    y = ((y - mu) * lax.rsqrt(var + GN_EPS)).reshape(B, S, D_RWKV) * gn_g + gn_b
    bonus = jnp.sum(rh * kh * r_k, -1, keepdims=True) * vh
    return (y + bonus.reshape(B, S, D_RWKV)) * g


def setup_inputs(seed: int = 0) -> dict:
    key = jax.random.key(seed)
    ks = iter(jax.random.split(key, 48))
    f32 = jnp.float32
    nrm = lambda shape, s: jax.random.normal(next(ks), shape, f32) * s
    uni = lambda shape, lo, hi: jax.random.uniform(next(ks), shape, f32, lo, hi)

    def in_col_scale(n):
        s = np.ones((n,), np.float32)
        s[2 * D_DIFF:3 * D_DIFF] = BETA
        s[N_DIFF + RW_V:N_DIFF + RW_V + D_RWKV] = BETA
        return jnp.asarray(s)

    D = D_MODEL
    return {
        "x": nrm((BATCH, SEQ, D), 1.0),
        "ln_in_g": 1.0 + nrm((D,), 0.02),
        "ln_in_b": nrm((D,), 0.02),
        "w_in_first": nrm((D, N_IN_FIRST), D ** -0.5) * in_col_scale(N_IN_FIRST),
        "w_in_rest": nrm((DEPTH - 1, D, N_IN_REST), D ** -0.5) * in_col_scale(N_IN_REST),
        "mu_first": uni((N_RWKV_BASE,), 0.1, 0.9),
        "mu_rest": uni((DEPTH - 1, N_RWKV_REST), 0.1, 0.9),
        "rel_bias": nrm((N_BUCKETS, DIFF_HEADS), 0.5),
        "lambda_q1": nrm((DEPTH, DIFF_HEAD_DIM), 0.1),
        "lambda_k1": nrm((DEPTH, DIFF_HEAD_DIM), 0.1),
        "lambda_q2": nrm((DEPTH, DIFF_HEAD_DIM), 0.1),
        "lambda_k2": nrm((DEPTH, DIFF_HEAD_DIM), 0.1),
        "subln_g": 1.0 + nrm((DEPTH, DIFF_V_DIM), 0.02),
        "rw_w0": uni((DEPTH, D_RWKV), -6.0, -0.5),
        "rw_w_up": nrm((DEPTH, LORA_W, D_RWKV), 0.1),
        "rw_a0": nrm((DEPTH, D_RWKV), 0.1),
        "rw_a_up": nrm((DEPTH, LORA_A, D_RWKV), 0.1),
        "rw_g_up": nrm((DEPTH, LORA_G, D_RWKV), LORA_G ** -0.5),
        "rw_v0": uni((DEPTH - 1, D_RWKV), 0.5, 1.5),
        "rw_v_up": nrm((DEPTH - 1, LORA_V, D_RWKV), 0.1),
        "rw_k_k": 0.85 + nrm((DEPTH, D_RWKV), 0.05),
        "rw_k_a": 1.0 + nrm((DEPTH, D_RWKV), 0.05),
        "rw_r_k": nrm((DEPTH, RWKV_HEADS, RWKV_HEAD), 0.1),
        "rw_gn_g": 1.0 + nrm((DEPTH, D_RWKV), 0.02),
        "rw_gn_b": nrm((DEPTH, D_RWKV), 0.02),
        "w_out": nrm((DEPTH, D_MIX, D), D_MIX ** -0.5 * BETA),
        "ln_mix_g": 1.0 + nrm((DEPTH, D), 0.02),
        "ln_mix_b": nrm((DEPTH, D), 0.02),
        "w_up": nrm((DEPTH, D, D_FF), D ** -0.5 * BETA),
        "w_down": nrm((DEPTH, D_FF, D), D_FF ** -0.5 * BETA),
        "ln_ffn_g": 1.0 + nrm((DEPTH, D), 0.02),
        "ln_ffn_b": nrm((DEPTH, D), 0.02),
    }


def reference(x, ln_in_g, ln_in_b, w_in_first, w_in_rest, mu_first, mu_rest, rel_bias,
              lambda_q1, lambda_k1, lambda_q2, lambda_k2, subln_g,
              rw_w0, rw_w_up, rw_a0, rw_a_up, rw_g_up, rw_v0, rw_v_up,
              rw_k_k, rw_k_a, rw_r_k, rw_gn_g, rw_gn_b,
              w_out, ln_mix_g, ln_mix_b, w_up, w_down, ln_ffn_g, ln_ffn_b):
    B, S, _ = x.shape
    f32 = jnp.float32
    x = layer_norm(x, ln_in_g, ln_in_b)
    v_first = None
    for l in range(DEPTH):
        w_in = w_in_first if l == 0 else w_in_rest[l - 1]
        mu = mu_first if l == 0 else mu_rest[l - 1]
        p = jnp.einsum('bsd,dn->bsn', x, w_in)

        q = p[..., 0:D_DIFF].reshape(B, S, DIFF_HEADS, 2, DIFF_HEAD_DIM)
        k = p[..., D_DIFF:2 * D_DIFF].reshape(B, S, DIFF_HEADS, 2, DIFF_HEAD_DIM)
        v = p[..., 2 * D_DIFF:N_DIFF].reshape(B, S, DIFF_HEADS, DIFF_V_DIM)
        lam_init = 0.8 - 0.6 * math.exp(-0.3 * l)
        lam = (jnp.exp(jnp.sum(lambda_q1[l] * lambda_k1[l]).astype(f32))
               - jnp.exp(jnp.sum(lambda_q2[l] * lambda_k2[l]).astype(f32)) + lam_init)
        y_diff = diff_attention(q, k, v, rel_bias, lam, subln_g[l], lam_init)

        pr = p[..., N_DIFF:]
        pr_prev = jnp.pad(pr, ((0, 0), (1, 0), (0, 0)))[:, :-1]
        pr = pr + (pr_prev - pr) * mu
        r_rw = pr[..., RW_R:RW_R + D_RWKV]
        k_rw = pr[..., RW_K:RW_K + D_RWKV]
        v_rw = pr[..., RW_V:RW_V + D_RWKV].astype(f32)
        if l == 0:
            v_first = v_rw
        else:
            xv = pr[..., N_RWKV_BASE:N_RWKV_REST].astype(f32)
            v_rw = v_rw + (v_first - v_rw) * jax.nn.sigmoid(rw_v0[l - 1] + xv @ rw_v_up[l - 1])
        y_rw = rwkv7_time_mix(r_rw, k_rw, v_rw,
                              pr[..., RW_W:RW_A], pr[..., RW_A:RW_G], pr[..., RW_G:N_RWKV_BASE],
                              rw_w0[l], rw_w_up[l], rw_a0[l], rw_a_up[l], rw_g_up[l],
                              rw_k_k[l], rw_k_a[l], rw_r_k[l], rw_gn_g[l], rw_gn_b[l])

        mix = jnp.concatenate([y_diff.astype(x.dtype), y_rw.astype(x.dtype)], axis=-1)
        mix = jnp.einsum('bsm,md->bsd', mix, w_out[l])
        x = layer_norm(ALPHA * x + mix, ln_mix_g[l], ln_mix_b[l])

        h = jnp.square(jax.nn.relu(jnp.einsum('bsd,df->bsf', x, w_up[l])))
        h = jnp.einsum('bsf,fd->bsd', h, w_down[l])
        x = layer_norm(ALPHA * x + h, ln_ffn_g[l], ln_ffn_b[l])
    return x
```

```python
import functools
import math

import jax
import jax.numpy as jnp
from jax import lax
from jax.experimental import pallas as pl
from jax.experimental.pallas import tpu as pltpu

F32 = jnp.float32
BF16 = jnp.bfloat16

D_MODEL = 1024
DEPTH = 2
D_DIFF = 512
D_RWKV = 512
DIFF_HEADS = 4
DIFF_HEAD_DIM = 64
DIFF_V_DIM = 128
RWKV_HEAD = 64
LORA_W = 64
LORA_A = 64
LORA_V = 32
LORA_G = 160
D_FF = 4 * D_MODEL
N_BUCKETS = 32
MAX_DISTANCE = 128
LN_EPS = 1e-5
SUBLN_EPS = 1e-5
GN_EPS = 64e-5
ALPHA = (2 * DEPTH) ** 0.25
N_DIFF = 3 * D_DIFF
RW_W = 3 * D_RWKV

LANES = 128
VMEM_LIMIT = 56 * 1024 * 1024

RW_PAD = 2048
ATT_T = 512
RW_TB = 512
RW_L = 64
NEG = -1e30


def _cparams(sem):
    return pltpu.CompilerParams(dimension_semantics=sem, vmem_limit_bytes=VMEM_LIMIT)


def _layer_norm(z, g, b):
    mu = jnp.mean(z, -1, keepdims=True)
    zc = z - mu
    var = jnp.mean(zc * zc, -1, keepdims=True)
    return zc * lax.rsqrt(var + LN_EPS) * g + b


def _ln_body(x_ref, g_ref, b_ref, o_ref):
    o_ref[...] = _layer_norm(x_ref[...], g_ref[...], b_ref[...])


def _ln(x, g, b, tm=512):
    t, d = x.shape
    return pl.pallas_call(
        _ln_body,
        grid=(t // tm,),
        in_specs=[pl.BlockSpec((tm, d), lambda i: (i, 0)),
                  pl.BlockSpec((1, d), lambda i: (0, 0)),
                  pl.BlockSpec((1, d), lambda i: (0, 0))],
        out_specs=pl.BlockSpec((tm, d), lambda i: (i, 0)),
        out_shape=jax.ShapeDtypeStruct((t, d), F32),
        compiler_params=_cparams(("parallel",)),
        name="ln_in",
    )(x, g.reshape(1, d), b.reshape(1, d))


def _proj_body(x_ref, w_ref, o_ref):
    o_ref[...] = jnp.dot(x_ref[...].astype(BF16), w_ref[...],
                         preferred_element_type=F32).astype(o_ref.dtype)


def _proj(x, w, out_dtype, name, tm=1024, tn=512):
    t, d = x.shape
    n = w.shape[1]
    return pl.pallas_call(
        _proj_body,
        grid=(t // tm, n // tn),
        in_specs=[pl.BlockSpec((tm, d), lambda i, j: (i, 0)),
                  pl.BlockSpec((d, tn), lambda i, j: (0, j))],
        out_specs=pl.BlockSpec((tm, tn), lambda i, j: (i, j)),
        out_shape=jax.ShapeDtypeStruct((t, n), out_dtype),
        compiler_params=_cparams(("parallel", "arbitrary")),
        name=name,
    )(x, w)


def _bias_body(rb_ref, o_ref, *, t):
    h = pl.program_id(0)
    which = pl.program_id(1)
    i = lax.broadcasted_iota(jnp.int32, (t, t), 0)
    j = lax.broadcasted_iota(jnp.int32, (t, t), 1)
    n = jnp.maximum(which * t + i - j, 0)
    max_exact = N_BUCKETS // 2
    nf = jnp.maximum(n, 1).astype(F32)
    large = max_exact + (jnp.log(nf / max_exact) / math.log(MAX_DISTANCE / max_exact)
                         * (N_BUCKETS - max_exact)).astype(jnp.int32)
    large = jnp.minimum(large, N_BUCKETS - 1)
    bucket = jnp.where(n < max_exact, n, large)
    acc = jnp.zeros((t, t), F32)
    for b in range(N_BUCKETS):
        acc = jnp.where(bucket == b, rb_ref[b, h], acc)
    o_ref[0, 0] = acc


def _bias_tiles(rel_bias, t):
    return pl.pallas_call(
        functools.partial(_bias_body, t=t),
        grid=(DIFF_HEADS, 2),
        in_specs=[pl.BlockSpec(memory_space=pltpu.SMEM)],
        out_specs=pl.BlockSpec((1, 1, t, t), lambda h, w: (h, w, 0, 0)),
        out_shape=jax.ShapeDtypeStruct((DIFF_HEADS, 2, t, t), F32),
        compiler_params=_cparams(("parallel", "parallel")),
        name="t5_bias_tiles",
    )(rel_bias)


def _attn_body(rb_ref, lamp_ref, q_ref, k_ref, v_ref, bias_ref, g_ref, o_ref,
               m_sc, l_sc, acc_sc, *, t, lam_init):
    h = pl.program_id(1)
    iq = pl.program_id(2)
    lane = lax.broadcasted_iota(jnp.int32, (t, LANES), 1)
    q = q_ref[...].astype(F32) * (DIFF_HEAD_DIM ** -0.5)
    qm = (jnp.where(lane < DIFF_HEAD_DIM, q, 0.0).astype(BF16),
          jnp.where(lane < DIFF_HEAD_DIM, 0.0, q).astype(BF16))
    c_far = rb_ref[N_BUCKETS - 1, h]

    m_sc[...] = jnp.full(m_sc.shape, NEG, F32)
    l_sc[...] = jnp.zeros(l_sc.shape, F32)
    acc_sc[...] = jnp.zeros(acc_sc.shape, F32)

    def tile(ik, bias, causal):
        rows = pl.ds(pl.multiple_of(ik * t, t), t)
        kt = k_ref[rows, :]
        vt = v_ref[rows, :]
        for m in range(2):
            s = lax.dot_general(qm[m], kt, (((1,), (1,)), ((), ())),
                                preferred_element_type=F32)
            s = s + bias
            if causal:
                ri = lax.broadcasted_iota(jnp.int32, (t, t), 0)
                ci = lax.broadcasted_iota(jnp.int32, (t, t), 1)
                s = jnp.where(ri >= ci, s, NEG)
            m_old = m_sc[m]
            m_new = jnp.maximum(m_old, jnp.max(s, -1, keepdims=True))
            a = jnp.exp(m_old - m_new)
            p = jnp.exp(s - m_new)
            l_sc[m] = a * l_sc[m] + jnp.sum(p, -1, keepdims=True)
            acc_sc[m] = a * acc_sc[m] + jnp.dot(p.astype(BF16), vt, preferred_element_type=F32)
            m_sc[m] = m_new

    def far(ik, carry):
        tile(ik, c_far, False)
        return carry

    lax.fori_loop(0, jnp.maximum(iq - 1, 0), far, 0)

    @pl.when(iq > 0)
    def _():
        tile(iq - 1, bias_ref[0, 1], False)

    tile(iq, bias_ref[0, 0], True)

    lp = lamp_ref[...]
    lam = (jnp.exp(jnp.sum(lp[0:1] * lp[1:2], -1, keepdims=True))
           - jnp.exp(jnp.sum(lp[2:3] * lp[3:4], -1, keepdims=True)) + lam_init)
    out = acc_sc[0] / l_sc[0] - lam * (acc_sc[1] / l_sc[1])
    out = out * lax.rsqrt(jnp.mean(out * out, -1, keepdims=True) + SUBLN_EPS) * g_ref[...]
    o_ref[...] = out * (1.0 - lam_init)


def _diff_attention(qkv, bias_tiles, rel_bias, lam_params, subln_g, lam_init, batch, seq):
    t = ATT_T
    nq = seq // t
    hb = D_DIFF // LANES
    return pl.pallas_call(
        functools.partial(_attn_body, t=t, lam_init=lam_init),
        grid=(batch, DIFF_HEADS, nq),
        in_specs=[pl.BlockSpec(memory_space=pltpu.SMEM),
                  pl.BlockSpec((4, DIFF_HEAD_DIM), lambda b, h, i: (0, 0)),
                  pl.BlockSpec((t, LANES), lambda b, h, i: (b * nq + i, h)),
                  pl.BlockSpec((seq, LANES), lambda b, h, i: (b, hb + h)),
                  pl.BlockSpec((seq, LANES), lambda b, h, i: (b, 2 * hb + h)),
                  pl.BlockSpec((1, 2, t, t), lambda b, h, i: (h, 0, 0, 0)),
                  pl.BlockSpec((1, DIFF_V_DIM), lambda b, h, i: (0, 0))],
        out_specs=pl.BlockSpec((t, LANES), lambda b, h, i: (b * nq + i, h)),
        out_shape=jax.ShapeDtypeStruct((batch * seq, D_DIFF), F32),
        scratch_shapes=[pltpu.VMEM((2, t, 1), F32), pltpu.VMEM((2, t, 1), F32),
                        pltpu.VMEM((2, t, DIFF_V_DIM), F32)],
        compiler_params=_cparams(("parallel", "parallel", "arbitrary")),
        name="diff_attention",
    )(rel_bias, lam_params, qkv, qkv, qkv, bias_tiles, subln_g.reshape(1, DIFF_V_DIM))


def _split_dot(x, m):
    hi = x.astype(BF16)
    lo = (x - hi.astype(F32)).astype(BF16)
    return (jnp.dot(hi, m, preferred_element_type=F32)
            + jnp.dot(lo, m, preferred_element_type=F32))


def _stack_heads(x):
    lane = lax.broadcasted_iota(jnp.int32, x.shape, 1)
    return jnp.concatenate([jnp.where(lane < RWKV_HEAD, x, 0.0),
                            jnp.where(lane < RWKV_HEAD, 0.0, x)], axis=0)


def _bdot(a, b):
    return jnp.dot(a.astype(BF16), b.astype(BF16), preferred_element_type=F32)


def _rwkv_body(*refs, tb, first):
    if first:
        (pr_ref, mu_ref, w0_ref, a0_ref, kk_ref, ka_ref, rk_ref, gng_ref, gnb_ref,
         wup_ref, aup_ref, gup1_ref, gup2_ref,
         y_ref, vout_ref,
         prev_sc, h_sc, r_sc, k_sc, v_sc, as_sc, bs_sc, ci_sc, lw_sc, y_sc) = refs
    else:
        (pr_ref, vf_ref, mu_ref, w0_ref, a0_ref, v0_ref, kk_ref, ka_ref, rk_ref, gng_ref, gnb_ref,
         wup_ref, aup_ref, gup1_ref, gup2_ref, vup_ref,
         y_ref,
         prev_sc, h_sc, r_sc, k_sc, v_sc, as_sc, bs_sc, ci_sc, lw_sc, y_sc) = refs
    L = RW_L
    c = D_RWKV

    @pl.when(pl.program_id(1) == 0)
    def _():
        prev_sc[...] = jnp.zeros(prev_sc.shape, F32)
        h_sc[...] = jnp.zeros(h_sc.shape, F32)

    row = lax.broadcasted_iota(jnp.int32, (tb, 1), 0)

    def shifted(c0, c1):
        cur = pr_ref[:, c0:c1]
        prev = jnp.where(row == 0, prev_sc[:, c0:c1], pltpu.roll(cur, 1, 0))
        return cur + (prev - cur) * mu_ref[:, c0:c1]

    r = shifted(0, c)
    k = shifted(c, 2 * c)
    v = shifted(2 * c, 3 * c)
    t1 = shifted(RW_W, RW_W + LANES)
    t2 = shifted(RW_W + LANES, RW_W + 2 * LANES)
    t3 = shifted(RW_W + 2 * LANES, RW_W + 3 * LANES)
    prev_sc[...] = pr_ref[tb - 1:tb, :]

    lw = _bdot(jnp.tanh(t1), wup_ref[...])
    la = _bdot(t1, aup_ref[...])
    g = _bdot(jax.nn.sigmoid(t2), gup1_ref[...]) + _bdot(jax.nn.sigmoid(t3), gup2_ref[...])

    z = -(w0_ref[...] + lw)
    softplus = jnp.maximum(z, 0.0) + jnp.log(1.0 + jnp.exp(-jnp.abs(z)))
    logw = -jnp.exp(-softplus - 0.5)
    a = jax.nn.sigmoid(a0_ref[...] + la)
    if first:
        vout_ref[...] = v
    else:
        v = v + (vf_ref[...] - v) * jax.nn.sigmoid(v0_ref[...] + _bdot(t3, vup_ref[...]))

    ri = lax.broadcasted_iota(jnp.int32, (c, c), 0)
    cj = lax.broadcasted_iota(jnp.int32, (c, c), 1)
    hs = RWKV_HEAD.bit_length() - 1
    same = (ri >> hs) == (cj >> hs)
    head_ones = jnp.where(same, 1.0, 0.0).astype(BF16)
    ls_ = L.bit_length() - 1
    ti = lax.broadcasted_iota(jnp.int32, (tb, tb), 0)
    tj = lax.broadcasted_iota(jnp.int32, (tb, tb), 1)
    tri = jnp.where(((ti >> ls_) == (tj >> ls_)) & (tj <= ti), 1.0, 0.0).astype(BF16)

    kk = k * kk_ref[...]
    kk = kk / jnp.maximum(jnp.sqrt(_split_dot(kk * kk, head_ones)), 1e-12)
    k2 = k * (1.0 + (a - 1.0) * ka_ref[...])

    hi = logw.astype(BF16)
    rem = logw - hi.astype(F32)
    mid = rem.astype(BF16)
    lo = (rem - mid.astype(F32)).astype(BF16)
    ci = (jnp.dot(tri, hi, preferred_element_type=F32) + jnp.dot(tri, mid, preferred_element_type=F32)
          + jnp.dot(tri, lo, preferred_element_type=F32))

    r_sc[...] = r
    k_sc[...] = k2
    v_sc[...] = v
    as_sc[...] = -kk
    bs_sc[...] = kk * a
    ci_sc[...] = ci
    lw_sc[...] = logw

    i2 = lax.broadcasted_iota(jnp.int32, (4 * L, 4 * L), 0)
    j2 = lax.broadcasted_iota(jnp.int32, (4 * L, 4 * L), 1)
    incl = jnp.where(i2 >= 2 * L, 1, 0)
    amask = ((((i2 >> ls_) & 1) == ((j2 >> ls_) & 1))
             & ((j2 & (L - 1)) < (i2 & (L - 1)) + incl))
    e_i = lax.broadcasted_iota(jnp.int32, (2 * L, 2 * L), 0)
    e_j = lax.broadcasted_iota(jnp.int32, (2 * L, 2 * L), 1)
    eye = e_i == e_j
    eye_f = jnp.where(eye, 1.0, 0.0)

    def chunk(ch, carry):
        rows = pl.ds(pl.multiple_of(ch * L, L), L)
        ci_c = ci_sc[rows, :]
        ce_c = ci_c - lw_sc[rows, :]
        last = ci_sc[pl.ds(ch * L + L - 1, 1), :]
        e_pos = jnp.exp(ci_c)
        e_neg = jnp.exp(-ci_c)
        e_end = jnp.exp(last - ci_c)
        g_end = jnp.exp(last)
        as_c = as_sc[rows, :]
        bs_c = bs_sc[rows, :]
        k_c = k_sc[rows, :]
        at = as_c * jnp.exp(ce_c)
        bt = bs_c * e_neg
        kt = k_c * e_neg
        rt = r_sc[rows, :] * e_pos
        bh = bs_c * e_end
        kh = k_c * e_end
        v_c = v_sc[rows, :]
        for p in range(c // LANES):
            ls = slice(p * LANES, (p + 1) * LANES)
            at_s = _stack_heads(at[:, ls])
            rt_s = _stack_heads(rt[:, ls])
            v_s = _stack_heads(v_c[:, ls])
            lhs = jnp.concatenate([at_s, rt_s], axis=0).astype(BF16)
            rhs = jnp.concatenate([_stack_heads(bt[:, ls]), _stack_heads(kt[:, ls])], axis=0).astype(BF16)
            a_all = lax.dot_general(lhs, rhs, (((1,), (1,)), ((), ())), preferred_element_type=F32)
            a_all = jnp.where(amask, a_all, 0.0)
            a_ab = a_all[:2 * L, :2 * L]
            a_ak = a_all[:2 * L, 2 * L:]
            a_rb = a_all[2 * L:, :2 * L]
            a_rk = a_all[2 * L:, 2 * L:]
            pw = a_ab
            tinv = eye_f + a_ab
            for _ in range(int(math.log2(L)) - 1):
                pw = _bdot(pw, pw)
                tinv = tinv + _bdot(tinv, pw)
            akv = _bdot(a_ak, v_s)
            wu = _bdot(tinv, jnp.concatenate([at_s, akv], axis=1))
            w_t = wu[:, :2 * L]
            u0 = wu[:, 2 * L:]
            bk_t = jnp.concatenate([_stack_heads(bh[:, ls]), _stack_heads(kh[:, ls])], axis=0).T
            uv = jnp.concatenate([u0, v_s], axis=0)
            m_mat = _bdot(bk_t[:, :2 * L], w_t) + jnp.where(eye, g_end[:, ls], 0.0)
            c_mat = _bdot(bk_t, uv)
            q_mat = rt_s + _bdot(a_rb, w_t)
            y0 = _bdot(jnp.concatenate([a_rb, a_rk], axis=1), uv)
            h_old = h_sc[p]
            y_st = _bdot(q_mat, h_old) + y0
            y_sc[rows, ls] = y_st[:L] + y_st[L:]
            h_sc[p] = _bdot(m_mat, h_old) + c_mat
        return carry

    lax.fori_loop(0, tb // L, chunk, 0)

    y = y_sc[...]
    mean = _split_dot(y, head_ones) * (1.0 / RWKV_HEAD)
    yc = y - mean
    var = _split_dot(yc * yc, head_ones) * (1.0 / RWKV_HEAD)
    yn = yc * lax.rsqrt(var + GN_EPS) * gng_ref[...] + gnb_ref[...]
    bonus = _split_dot(r * k2 * rk_ref[...], head_ones) * v
    y_ref[...] = (yn + bonus) * g


def _rwkv_mix(pr, v_first, prm, batch, seq):
    tb = RW_TB
    nt = seq // tb
    first = v_first is None
    c = D_RWKV
    row_spec = lambda w: pl.BlockSpec((tb, w), lambda b, t: (b * nt + t, 0))
    vec_spec = lambda w: pl.BlockSpec((1, w), lambda b, t: (0, 0))
    mat_spec = lambda: pl.BlockSpec((LANES, c), lambda b, t: (0, 0))
    vec = lambda x: x.reshape(1, -1)
    if first:
        args = [pr, vec(prm["mu"]), vec(prm["w0"]), vec(prm["a0"]), vec(prm["k_k"]), vec(prm["k_a"]),
                vec(prm["r_k"]), vec(prm["gn_g"]), vec(prm["gn_b"]),
                prm["w_up"], prm["a_up"], prm["g_up1"], prm["g_up2"]]
        in_specs = ([row_spec(RW_PAD), vec_spec(RW_PAD)] + [vec_spec(c)] * 7 + [mat_spec()] * 4)
        out_shape = [jax.ShapeDtypeStruct((batch * seq, c), F32)] * 2
        out_specs = [row_spec(c), row_spec(c)]
    else:
        args = [pr, v_first, vec(prm["mu"]), vec(prm["w0"]), vec(prm["a0"]), vec(prm["v0"]),
                vec(prm["k_k"]), vec(prm["k_a"]), vec(prm["r_k"]), vec(prm["gn_g"]), vec(prm["gn_b"]),
                prm["w_up"], prm["a_up"], prm["g_up1"], prm["g_up2"], prm["v_up"]]
        in_specs = ([row_spec(RW_PAD), row_spec(c), vec_spec(RW_PAD)] + [vec_spec(c)] * 8 + [mat_spec()] * 5)
        out_shape = jax.ShapeDtypeStruct((batch * seq, c), F32)
        out_specs = row_spec(c)
    scratch = [pltpu.VMEM((1, RW_PAD), F32), pltpu.VMEM((c // LANES, LANES, LANES), F32)]
    scratch += [pltpu.VMEM((tb, c), F32)] * 8
    return pl.pallas_call(
        functools.partial(_rwkv_body, tb=tb, first=first),
        grid=(batch, nt),
        in_specs=in_specs,
        out_specs=out_specs,
        out_shape=out_shape,
        scratch_shapes=scratch,
        compiler_params=_cparams(("parallel", "arbitrary")),
        name="rwkv7_mix_first" if first else "rwkv7_mix_rest",
    )(*args)


def _out_body(yd_ref, yr_ref, x_ref, w_ref, g_ref, b_ref, o_ref):
    mix = (jnp.dot(yd_ref[...].astype(BF16), w_ref[:D_DIFF, :], preferred_element_type=F32)
           + jnp.dot(yr_ref[...].astype(BF16), w_ref[D_DIFF:, :], preferred_element_type=F32))
    o_ref[...] = _layer_norm(ALPHA * x_ref[...] + mix, g_ref[...], b_ref[...])


def _out_proj(yd, yr, x, w, g, b, tm=512):
    t, d = x.shape
    return pl.pallas_call(
        _out_body,
        grid=(t // tm,),
        in_specs=[pl.BlockSpec((tm, D_DIFF), lambda i: (i, 0)),
                  pl.BlockSpec((tm, D_RWKV), lambda i: (i, 0)),
                  pl.BlockSpec((tm, d), lambda i: (i, 0)),
                  pl.BlockSpec((D_DIFF + D_RWKV, d), lambda i: (0, 0)),
                  pl.BlockSpec((1, d), lambda i: (0, 0)),
                  pl.BlockSpec((1, d), lambda i: (0, 0))],
        out_specs=pl.BlockSpec((tm, d), lambda i: (i, 0)),
        out_shape=jax.ShapeDtypeStruct((t, d), F32),
        compiler_params=_cparams(("parallel",)),
        name="out_proj_norm",
    )(yd, yr, x, w, g.reshape(1, d), b.reshape(1, d))


def _mlp_body(x_ref, wu_ref, wd_ref, g_ref, b_ref, o_ref, xb_sc, acc_sc):
    f = pl.program_id(1)

    @pl.when(f == 0)
    def _():
        xb_sc[...] = x_ref[...].astype(BF16)
        acc_sc[...] = jnp.zeros(acc_sc.shape, F32)

    h = jnp.maximum(jnp.dot(xb_sc[...], wu_ref[...], preferred_element_type=F32), 0.0)
    acc_sc[...] += jnp.dot((h * h).astype(BF16), wd_ref[...], preferred_element_type=F32)

    @pl.when(f == pl.num_programs(1) - 1)
    def _():
        o_ref[...] = _layer_norm(ALPHA * x_ref[...] + acc_sc[...], g_ref[...], b_ref[...])


def _mlp(x, wu, wd, g, b, tm=1024, tf=512):
    t, d = x.shape
    ff = wu.shape[1]
    return pl.pallas_call(
        _mlp_body,
        grid=(t // tm, ff // tf),
        in_specs=[pl.BlockSpec((tm, d), lambda i, f: (i, 0)),
                  pl.BlockSpec((d, tf), lambda i, f: (0, f)),
                  pl.BlockSpec((tf, d), lambda i, f: (f, 0)),
                  pl.BlockSpec((1, d), lambda i, f: (0, 0)),
                  pl.BlockSpec((1, d), lambda i, f: (0, 0))],
        out_specs=pl.BlockSpec((tm, d), lambda i, f: (i, 0)),
        out_shape=jax.ShapeDtypeStruct((t, d), F32),
        scratch_shapes=[pltpu.VMEM((tm, d), BF16), pltpu.VMEM((tm, d), F32)],
        compiler_params=_cparams(("parallel", "arbitrary")),
        name="mlp_norm",
    )(x, wu, wd, g.reshape(1, d), b.reshape(1, d))


def _pad_rows(w, before, total):
    return jnp.pad(w, ((before, total - before - w.shape[0]), (0, 0)))


def _rwkv_params(l, mu, rw_w0, rw_w_up, rw_a0, rw_a_up, rw_g_up, rw_v0, rw_v_up,
                 rw_k_k, rw_k_a, rw_r_k, rw_gn_g, rw_gn_b):
    prm = {
        "mu": jnp.pad(mu, (0, RW_PAD - mu.shape[0])),
        "w0": rw_w0[l], "a0": rw_a0[l], "k_k": rw_k_k[l], "k_a": rw_k_a[l],
        "r_k": rw_r_k[l].reshape(-1), "gn_g": rw_gn_g[l], "gn_b": rw_gn_b[l],
        "w_up": _pad_rows(rw_w_up[l], 0, LANES).astype(BF16),
        "a_up": _pad_rows(rw_a_up[l], LORA_W, LANES).astype(BF16),
        "g_up1": rw_g_up[l][:LANES].astype(BF16),
        "g_up2": _pad_rows(rw_g_up[l][LANES:], 0, LANES).astype(BF16),
    }
    if l > 0:
        prm["v0"] = rw_v0[l - 1]
        prm["v_up"] = _pad_rows(rw_v_up[l - 1], LORA_G - LANES, LANES).astype(BF16)
    return prm


def _forward(x, ln_in_g, ln_in_b, w_in_first, w_in_rest, mu_first, mu_rest, rel_bias,
             lambda_q1, lambda_k1, lambda_q2, lambda_k2, subln_g,
             rw_w0, rw_w_up, rw_a0, rw_a_up, rw_g_up, rw_v0, rw_v_up,
             rw_k_k, rw_k_a, rw_r_k, rw_gn_g, rw_gn_b,
             w_out, ln_mix_g, ln_mix_b, w_up, w_down, ln_ffn_g, ln_ffn_b):
    batch, seq, d = x.shape
    xs = _ln(x.reshape(batch * seq, d), ln_in_g, ln_in_b)
    bias_tiles = _bias_tiles(rel_bias, ATT_T)
    v_first = None
    for l in range(DEPTH):
        w_in = w_in_first if l == 0 else w_in_rest[l - 1]
        mu = mu_first if l == 0 else mu_rest[l - 1]
        w_diff = w_in[:, :N_DIFF].astype(BF16)
        w_rw = jnp.pad(w_in[:, N_DIFF:], ((0, 0), (0, RW_PAD - (w_in.shape[1] - N_DIFF)))).astype(BF16)
        qkv = _proj(xs, w_diff, BF16, "in_proj_diff")
        pr = _proj(xs, w_rw, F32, "in_proj_rwkv")

        lam_init = 0.8 - 0.6 * math.exp(-0.3 * l)
        lam_params = jnp.stack([lambda_q1[l], lambda_k1[l], lambda_q2[l], lambda_k2[l]])
        y_diff = _diff_attention(qkv, bias_tiles, rel_bias, lam_params, subln_g[l], lam_init, batch, seq)

        prm = _rwkv_params(l, mu, rw_w0, rw_w_up, rw_a0, rw_a_up, rw_g_up, rw_v0, rw_v_up,
                           rw_k_k, rw_k_a, rw_r_k, rw_gn_g, rw_gn_b)
        if l == 0:
            y_rw, v_first = _rwkv_mix(pr, None, prm, batch, seq)
        else:
            y_rw = _rwkv_mix(pr, v_first, prm, batch, seq)

        xs = _out_proj(y_diff, y_rw, xs, w_out[l].astype(BF16), ln_mix_g[l], ln_mix_b[l])
        xs = _mlp(xs, w_up[l].astype(BF16), w_down[l].astype(BF16), ln_ffn_g[l], ln_ffn_b[l])
    return xs.reshape(batch, seq, d)


def kernel(x, ln_in_g, ln_in_b, w_in_first, w_in_rest, mu_first, mu_rest, rel_bias, lambda_q1, lambda_k1, lambda_q2, lambda_k2, subln_g, rw_w0, rw_w_up, rw_a0, rw_a_up, rw_g_up, rw_v0, rw_v_up, rw_k_k, rw_k_a, rw_r_k, rw_gn_g, rw_gn_b, w_out, ln_mix_g, ln_mix_b, w_up, w_down, ln_ffn_g, ln_ffn_b):
    return _forward(x, ln_in_g, ln_in_b, w_in_first, w_in_rest, mu_first, mu_rest, rel_bias,
                    lambda_q1, lambda_k1, lambda_q2, lambda_k2, subln_g,
                    rw_w0, rw_w_up, rw_a0, rw_a_up, rw_g_up, rw_v0, rw_v_up,
                    rw_k_k, rw_k_a, rw_r_k, rw_gn_g, rw_gn_b,
                    w_out, ln_mix_g, ln_mix_b, w_up, w_down, ln_ffn_g, ln_ffn_b)
```

```python
import functools
import math

import jax
import jax.numpy as jnp
from jax import lax
from jax.experimental import pallas as pl
from jax.experimental.pallas import tpu as pltpu

F32 = jnp.float32
BF16 = jnp.bfloat16

D_MODEL = 1024
DEPTH = 2
D_DIFF = 512
D_RWKV = 512
DIFF_HEADS = 4
DIFF_HEAD_DIM = 64
DIFF_V_DIM = 128
RWKV_HEAD = 64
LORA_W = 64
LORA_A = 64
LORA_V = 32
LORA_G = 160
D_FF = 4 * D_MODEL
N_BUCKETS = 32
MAX_DISTANCE = 128
LN_EPS = 1e-5
SUBLN_EPS = 1e-5
GN_EPS = 64e-5
ALPHA = (2 * DEPTH) ** 0.25
N_DIFF = 3 * D_DIFF
RW_W = 3 * D_RWKV

LANES = 128
VMEM_LIMIT = 56 * 1024 * 1024

RW_PAD = 2048
ATT_T = 512
RW_TB = 512
RW_L = 64
NEG = -1e30


def _cparams(sem):
    return pltpu.CompilerParams(dimension_semantics=sem, vmem_limit_bytes=VMEM_LIMIT)


def _layer_norm(z, g, b):
    mu = jnp.mean(z, -1, keepdims=True)
    zc = z - mu
    var = jnp.mean(zc * zc, -1, keepdims=True)
    return zc * lax.rsqrt(var + LN_EPS) * g + b


def _ln_body(x_ref, g_ref, b_ref, o_ref):
    o_ref[...] = _layer_norm(x_ref[...], g_ref[...], b_ref[...])


def _ln(x, g, b, tm=512):
    t, d = x.shape
    return pl.pallas_call(
        _ln_body,
        grid=(t // tm,),
        in_specs=[pl.BlockSpec((tm, d), lambda i: (i, 0)),
                  pl.BlockSpec((1, d), lambda i: (0, 0)),
                  pl.BlockSpec((1, d), lambda i: (0, 0))],
        out_specs=pl.BlockSpec((tm, d), lambda i: (i, 0)),
        out_shape=jax.ShapeDtypeStruct((t, d), F32),
        compiler_params=_cparams(("parallel",)),
        name="ln_in",
    )(x, g.reshape(1, d), b.reshape(1, d))


def _proj_body(x_ref, w_ref, o_ref):
    o_ref[...] = jnp.dot(x_ref[...].astype(BF16), w_ref[...],
                         preferred_element_type=F32).astype(o_ref.dtype)


def _proj(x, w, out_dtype, name, tm=1024, tn=512):
    t, d = x.shape
    n = w.shape[1]
    return pl.pallas_call(
        _proj_body,
        grid=(t // tm, n // tn),
        in_specs=[pl.BlockSpec((tm, d), lambda i, j: (i, 0)),
                  pl.BlockSpec((d, tn), lambda i, j: (0, j))],
        out_specs=pl.BlockSpec((tm, tn), lambda i, j: (i, j)),
        out_shape=jax.ShapeDtypeStruct((t, n), out_dtype),
        compiler_params=_cparams(("parallel", "arbitrary")),
        name=name,
    )(x, w)


def _bias_body(rb_ref, o_ref, *, t):
    h = pl.program_id(0)
    which = pl.program_id(1)
    i = lax.broadcasted_iota(jnp.int32, (t, t), 0)
    j = lax.broadcasted_iota(jnp.int32, (t, t), 1)
    n = jnp.maximum(which * t + i - j, 0)
    max_exact = N_BUCKETS // 2
    nf = jnp.maximum(n, 1).astype(F32)
    large = max_exact + (jnp.log(nf / max_exact) / math.log(MAX_DISTANCE / max_exact)
                         * (N_BUCKETS - max_exact)).astype(jnp.int32)
    large = jnp.minimum(large, N_BUCKETS - 1)
    bucket = jnp.where(n < max_exact, n, large)
    acc = jnp.zeros((t, t), F32)
    for b in range(N_BUCKETS):
        acc = jnp.where(bucket == b, rb_ref[b, h], acc)
    o_ref[0, 0] = acc


def _bias_tiles(rel_bias, t):
    return pl.pallas_call(
        functools.partial(_bias_body, t=t),
        grid=(DIFF_HEADS, 2),
        in_specs=[pl.BlockSpec(memory_space=pltpu.SMEM)],
        out_specs=pl.BlockSpec((1, 1, t, t), lambda h, w: (h, w, 0, 0)),
        out_shape=jax.ShapeDtypeStruct((DIFF_HEADS, 2, t, t), F32),
        compiler_params=_cparams(("parallel", "parallel")),
        name="t5_bias_tiles",
    )(rel_bias)


def _attn_body(rb_ref, lamp_ref, q_ref, k_ref, v_ref, bias_ref, g_ref, o_ref,
               m_sc, acc_sc, *, t, lam_init):
    h = pl.program_id(1)
    iq = pl.program_id(2)
    lane = lax.broadcasted_iota(jnp.int32, (t, LANES), 1)
    q = q_ref[...].astype(F32) * (DIFF_HEAD_DIM ** -0.5)
    qm = (jnp.where(lane < DIFF_HEAD_DIM, q, 0.0).astype(BF16),
          jnp.where(lane < DIFF_HEAD_DIM, 0.0, q).astype(BF16))
    c_far = rb_ref[N_BUCKETS - 1, h]
    ones = jnp.ones((t, LANES), BF16)

    m_sc[...] = jnp.full(m_sc.shape, NEG, F32)
    acc_sc[...] = jnp.zeros(acc_sc.shape, F32)

    def tile(ik, bias, causal):
        rows = pl.ds(pl.multiple_of(ik * t, t), t)
        kt = k_ref[rows, :]
        vt = jnp.concatenate([v_ref[rows, :], ones], axis=1)
        maps = range(2)
        s = [lax.dot_general(qm[m], kt, (((1,), (1,)), ((), ())), preferred_element_type=F32)
             for m in maps]
        if bias is None:
            m_cur = [jnp.max(s[m], -1, keepdims=True) + c_far for m in maps]
        else:
            s = [s[m] + bias for m in maps]
            if causal:
                ri = lax.broadcasted_iota(jnp.int32, (t, t), 0)
                ci = lax.broadcasted_iota(jnp.int32, (t, t), 1)
                s = [jnp.where(ri >= ci, s[m], NEG) for m in maps]
            m_cur = [jnp.max(s[m], -1, keepdims=True) for m in maps]
        m_old = [m_sc[m] for m in maps]
        m_new = [jnp.maximum(m_old[m], m_cur[m]) for m in maps]
        a = [jnp.exp(m_old[m] - m_new[m]) for m in maps]
        sub = [m_new[m] - c_far if bias is None else m_new[m] for m in maps]
        p = [jnp.exp(s[m] - jnp.concatenate([sub[m]] * (t // LANES), axis=1)).astype(BF16)
             for m in maps]
        pv = [jnp.dot(p[m], vt, preferred_element_type=F32) for m in maps]
        for m in maps:
            acc_sc[m] = jnp.concatenate([a[m], a[m]], axis=1) * acc_sc[m] + pv[m]
            m_sc[m] = m_new[m]

    def far(ik, carry):
        tile(ik, None, False)
        return carry

    lax.fori_loop(0, jnp.maximum(iq - 1, 0), far, 0)

    @pl.when(iq > 0)
    def _():
        tile(iq - 1, bias_ref[0, 1], False)

    tile(iq, bias_ref[0, 0], True)

    lp = lamp_ref[...]
    lam = (jnp.exp(jnp.sum(lp[0:1] * lp[1:2], -1, keepdims=True))
           - jnp.exp(jnp.sum(lp[2:3] * lp[3:4], -1, keepdims=True)) + lam_init)
    a0 = acc_sc[0]
    a1 = acc_sc[1]
    out = a0[:, :LANES] / a0[:, LANES:] - lam * (a1[:, :LANES] / a1[:, LANES:])
    out = out * lax.rsqrt(jnp.mean(out * out, -1, keepdims=True) + SUBLN_EPS) * g_ref[...]
    o_ref[...] = out * (1.0 - lam_init)


def _diff_attention(qkv, bias_tiles, rel_bias, lam_params, subln_g, lam_init, batch, seq):
    t = ATT_T
    nq = seq // t
    hb = D_DIFF // LANES
    return pl.pallas_call(
        functools.partial(_attn_body, t=t, lam_init=lam_init),
        grid=(batch, DIFF_HEADS, nq),
        in_specs=[pl.BlockSpec(memory_space=pltpu.SMEM),
                  pl.BlockSpec((4, DIFF_HEAD_DIM), lambda b, h, i: (0, 0)),
                  pl.BlockSpec((t, LANES), lambda b, h, i: (b * nq + i, h)),
                  pl.BlockSpec((seq, LANES), lambda b, h, i: (b, hb + h)),
                  pl.BlockSpec((seq, LANES), lambda b, h, i: (b, 2 * hb + h)),
                  pl.BlockSpec((1, 2, t, t), lambda b, h, i: (h, 0, 0, 0)),
                  pl.BlockSpec((1, DIFF_V_DIM), lambda b, h, i: (0, 0))],
        out_specs=pl.BlockSpec((t, LANES), lambda b, h, i: (b * nq + i, h)),
        out_shape=jax.ShapeDtypeStruct((batch * seq, D_DIFF), F32),
        scratch_shapes=[pltpu.VMEM((2, t, LANES), F32),
                        pltpu.VMEM((2, t, DIFF_V_DIM + LANES), F32)],
        compiler_params=_cparams(("parallel", "parallel", "arbitrary")),
        name="diff_attention",
    )(rel_bias, lam_params, qkv, qkv, qkv, bias_tiles, subln_g.reshape(1, DIFF_V_DIM))


def _split_dot(x, m):
    hi = x.astype(BF16)
    lo = (x - hi.astype(F32)).astype(BF16)
    return (jnp.dot(hi, m, preferred_element_type=F32)
            + jnp.dot(lo, m, preferred_element_type=F32))


def _stack_heads(x):
    lane = lax.broadcasted_iota(jnp.int32, x.shape, 1)
    return jnp.concatenate([jnp.where(lane < RWKV_HEAD, x, 0.0),
                            jnp.where(lane < RWKV_HEAD, 0.0, x)], axis=0)


def _bdot(a, b):
    return jnp.dot(a.astype(BF16), b.astype(BF16), preferred_element_type=F32)


def _rwkv_body(*refs, tb, first):
    if first:
        (pr_ref, mu_ref, w0_ref, a0_ref, kk_ref, ka_ref, rk_ref, gng_ref, gnb_ref,
         wup_ref, aup_ref, gup1_ref, gup2_ref,
         y_ref, vout_ref,
         prev_sc, h_sc, r_sc, k_sc, v_sc, as_sc, bs_sc, ci_sc, lw_sc, y_sc) = refs
    else:
        (pr_ref, vf_ref, mu_ref, w0_ref, a0_ref, v0_ref, kk_ref, ka_ref, rk_ref, gng_ref, gnb_ref,
         wup_ref, aup_ref, gup1_ref, gup2_ref, vup_ref,
         y_ref,
         prev_sc, h_sc, r_sc, k_sc, v_sc, as_sc, bs_sc, ci_sc, lw_sc, y_sc) = refs
    L = RW_L
    c = D_RWKV

    @pl.when(pl.program_id(1) == 0)
    def _():
        prev_sc[...] = jnp.zeros(prev_sc.shape, F32)
        h_sc[...] = jnp.zeros(h_sc.shape, F32)

    row = lax.broadcasted_iota(jnp.int32, (tb, 1), 0)

    def shifted(c0, c1):
        cur = pr_ref[:, c0:c1]
        prev = jnp.where(row == 0, prev_sc[:, c0:c1], pltpu.roll(cur, 1, 0))
        return cur + (prev - cur) * mu_ref[:, c0:c1]

    r = shifted(0, c)
    k = shifted(c, 2 * c)
    v = shifted(2 * c, 3 * c)
    t1 = shifted(RW_W, RW_W + LANES)
    t2 = shifted(RW_W + LANES, RW_W + 2 * LANES)
    t3 = shifted(RW_W + 2 * LANES, RW_W + 3 * LANES)
    prev_sc[...] = pr_ref[tb - 1:tb, :]

    lw = _bdot(jnp.tanh(t1), wup_ref[...])
    la = _bdot(t1, aup_ref[...])
    g = _bdot(jax.nn.sigmoid(t2), gup1_ref[...]) + _bdot(jax.nn.sigmoid(t3), gup2_ref[...])

    z = -(w0_ref[...] + lw)
    softplus = jnp.maximum(z, 0.0) + jnp.log(1.0 + jnp.exp(-jnp.abs(z)))
    logw = -jnp.exp(-softplus - 0.5)
    a = jax.nn.sigmoid(a0_ref[...] + la)
    if first:
        vout_ref[...] = v
    else:
        v = v + (vf_ref[...] - v) * jax.nn.sigmoid(v0_ref[...] + _bdot(t3, vup_ref[...]))

    ri = lax.broadcasted_iota(jnp.int32, (c, c), 0)
    cj = lax.broadcasted_iota(jnp.int32, (c, c), 1)
    hs = RWKV_HEAD.bit_length() - 1
    same = (ri >> hs) == (cj >> hs)
    head_ones = jnp.where(same, 1.0, 0.0).astype(BF16)
    ls_ = L.bit_length() - 1
    ti = lax.broadcasted_iota(jnp.int32, (tb, tb), 0)
    tj = lax.broadcasted_iota(jnp.int32, (tb, tb), 1)
    tri = jnp.where(((ti >> ls_) == (tj >> ls_)) & (tj <= ti), 1.0, 0.0).astype(BF16)

    kk = k * kk_ref[...]
    kk = kk / jnp.maximum(jnp.sqrt(_split_dot(kk * kk, head_ones)), 1e-12)
    k2 = k * (1.0 + (a - 1.0) * ka_ref[...])

    hi = logw.astype(BF16)
    rem = logw - hi.astype(F32)
    mid = rem.astype(BF16)
    lo = (rem - mid.astype(F32)).astype(BF16)
    ci = (jnp.dot(tri, hi, preferred_element_type=F32) + jnp.dot(tri, mid, preferred_element_type=F32)
          + jnp.dot(tri, lo, preferred_element_type=F32))

    r_sc[...] = r
    k_sc[...] = k2
    v_sc[...] = v
    as_sc[...] = -kk
    bs_sc[...] = kk * a
    ci_sc[...] = ci
    lw_sc[...] = logw

    i2 = lax.broadcasted_iota(jnp.int32, (4 * L, 4 * L), 0)
    j2 = lax.broadcasted_iota(jnp.int32, (4 * L, 4 * L), 1)
    incl = jnp.where(i2 >= 2 * L, 1, 0)
    amask = ((((i2 >> ls_) & 1) == ((j2 >> ls_) & 1))
             & ((j2 & (L - 1)) < (i2 & (L - 1)) + incl))
    e_i = lax.broadcasted_iota(jnp.int32, (2 * L, 2 * L), 0)
    e_j = lax.broadcasted_iota(jnp.int32, (2 * L, 2 * L), 1)
    eye = e_i == e_j
    eye_f = jnp.where(eye, 1.0, 0.0)

    def chunk(ch, carry):
        rows = pl.ds(pl.multiple_of(ch * L, L), L)
        ci_c = ci_sc[rows, :]
        ce_c = ci_c - lw_sc[rows, :]
        last = ci_sc[pl.ds(ch * L + L - 1, 1), :]
        e_pos = jnp.exp(ci_c)
        e_neg = jnp.exp(-ci_c)
        e_end = jnp.exp(last - ci_c)
        g_end = jnp.exp(last)
        as_c = as_sc[rows, :]
        bs_c = bs_sc[rows, :]
        k_c = k_sc[rows, :]
        at = as_c * jnp.exp(ce_c)
        bt = bs_c * e_neg
        kt = k_c * e_neg
        rt = r_sc[rows, :] * e_pos
        bh = bs_c * e_end
        kh = k_c * e_end
        v_c = v_sc[rows, :]
        pairs = range(c // LANES)
        lsl = [slice(p * LANES, (p + 1) * LANES) for p in pairs]
        at_s = [_stack_heads(at[:, ls]) for ls in lsl]
        rt_s = [_stack_heads(rt[:, ls]) for ls in lsl]
        v_s = [_stack_heads(v_c[:, ls]) for ls in lsl]
        a_all = []
        for p in pairs:
            lhs = jnp.concatenate([at_s[p], rt_s[p]], axis=0).astype(BF16)
            rhs = jnp.concatenate([_stack_heads(bt[:, lsl[p]]), _stack_heads(kt[:, lsl[p]])],
                                  axis=0).astype(BF16)
            a = lax.dot_general(lhs, rhs, (((1,), (1,)), ((), ())), preferred_element_type=F32)
            a_all.append(jnp.where(amask, a, 0.0))
        a_ab = [a[:2 * L, :2 * L] for a in a_all]
        a_ak = [a[:2 * L, 2 * L:] for a in a_all]
        a_r = [a[2 * L:, :] for a in a_all]
        pw = [_bdot(a, a) for a in a_ab]
        tinv = [eye_f + a for a in a_ab]
        akv = [_bdot(a_ak[p], v_s[p]) for p in pairs]
        for _ in range(int(math.log2(L)) - 2):
            for p in pairs:
                pw_b = pw[p].astype(BF16)
                both = jnp.dot(jnp.concatenate([pw_b, tinv[p].astype(BF16)], axis=0), pw_b,
                               preferred_element_type=F32)
                pw[p] = both[:2 * L]
                tinv[p] = tinv[p] + both[2 * L:]
        tinv = [tinv[p] + _bdot(tinv[p], pw[p]) for p in pairs]
        wu = [_bdot(tinv[p], jnp.concatenate([at_s[p], akv[p]], axis=1)) for p in pairs]
        wrg = [_bdot(jnp.concatenate([wu[p][:, :2 * L], rt_s[p],
                                      jnp.where(eye, g_end[:, lsl[p]], 0.0)], axis=0), h_sc[p])
               for p in pairs]
        uv = [jnp.concatenate([wrg[p][:2 * L] + wu[p][:, 2 * L:], v_s[p]], axis=0).astype(BF16)
              for p in pairs]
        for p in pairs:
            y_st = wrg[p][2 * L:4 * L] + jnp.dot(a_r[p].astype(BF16), uv[p],
                                                 preferred_element_type=F32)
            y_sc[rows, lsl[p]] = y_st[:L] + y_st[L:]
        for p in pairs:
            bk_t = jnp.concatenate([_stack_heads(bh[:, lsl[p]]), _stack_heads(kh[:, lsl[p]])],
                                   axis=0).T
            h_sc[p] = wrg[p][4 * L:] + jnp.dot(bk_t.astype(BF16), uv[p],
                                               preferred_element_type=F32)
        return carry

    lax.fori_loop(0, tb // L, chunk, 0)

    y = y_sc[...]
    mean = _split_dot(y, head_ones) * (1.0 / RWKV_HEAD)
    yc = y - mean
    var = _split_dot(yc * yc, head_ones) * (1.0 / RWKV_HEAD)
    yn = yc * lax.rsqrt(var + GN_EPS) * gng_ref[...] + gnb_ref[...]
    bonus = _split_dot(r * k2 * rk_ref[...], head_ones) * v
    y_ref[...] = (yn + bonus) * g


def _rwkv_mix(pr, v_first, prm, batch, seq):
    tb = RW_TB
    nt = seq // tb
    first = v_first is None
    c = D_RWKV
    row_spec = lambda w: pl.BlockSpec((tb, w), lambda b, t: (b * nt + t, 0))
    vec_spec = lambda w: pl.BlockSpec((1, w), lambda b, t: (0, 0))
    mat_spec = lambda: pl.BlockSpec((LANES, c), lambda b, t: (0, 0))
    vec = lambda x: x.reshape(1, -1)
    if first:
        args = [pr, vec(prm["mu"]), vec(prm["w0"]), vec(prm["a0"]), vec(prm["k_k"]), vec(prm["k_a"]),
                vec(prm["r_k"]), vec(prm["gn_g"]), vec(prm["gn_b"]),
                prm["w_up"], prm["a_up"], prm["g_up1"], prm["g_up2"]]
        in_specs = ([row_spec(RW_PAD), vec_spec(RW_PAD)] + [vec_spec(c)] * 7 + [mat_spec()] * 4)
        out_shape = [jax.ShapeDtypeStruct((batch * seq, c), F32)] * 2
        out_specs = [row_spec(c), row_spec(c)]
    else:
        args = [pr, v_first, vec(prm["mu"]), vec(prm["w0"]), vec(prm["a0"]), vec(prm["v0"]),
                vec(prm["k_k"]), vec(prm["k_a"]), vec(prm["r_k"]), vec(prm["gn_g"]), vec(prm["gn_b"]),
                prm["w_up"], prm["a_up"], prm["g_up1"], prm["g_up2"], prm["v_up"]]
        in_specs = ([row_spec(RW_PAD), row_spec(c), vec_spec(RW_PAD)] + [vec_spec(c)] * 8 + [mat_spec()] * 5)
        out_shape = jax.ShapeDtypeStruct((batch * seq, c), F32)
        out_specs = row_spec(c)
    scratch = [pltpu.VMEM((1, RW_PAD), F32), pltpu.VMEM((c // LANES, LANES, LANES), F32)]
    scratch += [pltpu.VMEM((tb, c), F32)] * 8
    return pl.pallas_call(
        functools.partial(_rwkv_body, tb=tb, first=first),
        grid=(batch, nt),
        in_specs=in_specs,
        out_specs=out_specs,
        out_shape=out_shape,
        scratch_shapes=scratch,
        compiler_params=_cparams(("parallel", "arbitrary")),
        name="rwkv7_mix_first" if first else "rwkv7_mix_rest",
    )(*args)


def _out_body(yd_ref, yr_ref, x_ref, w_ref, g_ref, b_ref, o_ref):
    mix = (jnp.dot(yd_ref[...].astype(BF16), w_ref[:D_DIFF, :], preferred_element_type=F32)
           + jnp.dot(yr_ref[...].astype(BF16), w_ref[D_DIFF:, :], preferred_element_type=F32))
    o_ref[...] = _layer_norm(ALPHA * x_ref[...] + mix, g_ref[...], b_ref[...])


def _out_proj(yd, yr, x, w, g, b, tm=512):
    t, d = x.shape
    return pl.pallas_call(
        _out_body,
        grid=(t // tm,),
        in_specs=[pl.BlockSpec((tm, D_DIFF), lambda i: (i, 0)),
                  pl.BlockSpec((tm, D_RWKV), lambda i: (i, 0)),
                  pl.BlockSpec((tm, d), lambda i: (i, 0)),
                  pl.BlockSpec((D_DIFF + D_RWKV, d), lambda i: (0, 0)),
                  pl.BlockSpec((1, d), lambda i: (0, 0)),
                  pl.BlockSpec((1, d), lambda i: (0, 0))],
        out_specs=pl.BlockSpec((tm, d), lambda i: (i, 0)),
        out_shape=jax.ShapeDtypeStruct((t, d), F32),
        compiler_params=_cparams(("parallel",)),
        name="out_proj_norm",
    )(yd, yr, x, w, g.reshape(1, d), b.reshape(1, d))


def _mlp_body(x_ref, wu_ref, wd_ref, g_ref, b_ref, o_ref, xb_sc, acc_sc):
    f = pl.program_id(1)

    @pl.when(f == 0)
    def _():
        xb_sc[...] = x_ref[...].astype(BF16)
        acc_sc[...] = jnp.zeros(acc_sc.shape, F32)

    h = jnp.maximum(jnp.dot(xb_sc[...], wu_ref[...], preferred_element_type=F32), 0.0)
    acc_sc[...] += jnp.dot((h * h).astype(BF16), wd_ref[...], preferred_element_type=F32)

    @pl.when(f == pl.num_programs(1) - 1)
    def _():
        o_ref[...] = _layer_norm(ALPHA * x_ref[...] + acc_sc[...], g_ref[...], b_ref[...])


def _mlp(x, wu, wd, g, b, tm=1024, tf=512):
    t, d = x.shape
    ff = wu.shape[1]
    return pl.pallas_call(
        _mlp_body,
        grid=(t // tm, ff // tf),
        in_specs=[pl.BlockSpec((tm, d), lambda i, f: (i, 0)),
                  pl.BlockSpec((d, tf), lambda i, f: (0, f)),
                  pl.BlockSpec((tf, d), lambda i, f: (f, 0)),
                  pl.BlockSpec((1, d), lambda i, f: (0, 0)),
                  pl.BlockSpec((1, d), lambda i, f: (0, 0))],
        out_specs=pl.BlockSpec((tm, d), lambda i, f: (i, 0)),
        out_shape=jax.ShapeDtypeStruct((t, d), F32),
        scratch_shapes=[pltpu.VMEM((tm, d), BF16), pltpu.VMEM((tm, d), F32)],
        compiler_params=_cparams(("parallel", "arbitrary")),
        name="mlp_norm",
    )(x, wu, wd, g.reshape(1, d), b.reshape(1, d))


def _pad_rows(w, before, total):
    return jnp.pad(w, ((before, total - before - w.shape[0]), (0, 0)))


def _rwkv_params(l, mu, rw_w0, rw_w_up, rw_a0, rw_a_up, rw_g_up, rw_v0, rw_v_up,
                 rw_k_k, rw_k_a, rw_r_k, rw_gn_g, rw_gn_b):
    prm = {
        "mu": jnp.pad(mu, (0, RW_PAD - mu.shape[0])),
        "w0": rw_w0[l], "a0": rw_a0[l], "k_k": rw_k_k[l], "k_a": rw_k_a[l],
        "r_k": rw_r_k[l].reshape(-1), "gn_g": rw_gn_g[l], "gn_b": rw_gn_b[l],
        "w_up": _pad_rows(rw_w_up[l], 0, LANES).astype(BF16),
        "a_up": _pad_rows(rw_a_up[l], LORA_W, LANES).astype(BF16),
        "g_up1": rw_g_up[l][:LANES].astype(BF16),
        "g_up2": _pad_rows(rw_g_up[l][LANES:], 0, LANES).astype(BF16),
    }
    if l > 0:
        prm["v0"] = rw_v0[l - 1]
        prm["v_up"] = _pad_rows(rw_v_up[l - 1], LORA_G - LANES, LANES).astype(BF16)
    return prm


def _forward(x, ln_in_g, ln_in_b, w_in_first, w_in_rest, mu_first, mu_rest, rel_bias,
             lambda_q1, lambda_k1, lambda_q2, lambda_k2, subln_g,
             rw_w0, rw_w_up, rw_a0, rw_a_up, rw_g_up, rw_v0, rw_v_up,
             rw_k_k, rw_k_a, rw_r_k, rw_gn_g, rw_gn_b,
             w_out, ln_mix_g, ln_mix_b, w_up, w_down, ln_ffn_g, ln_ffn_b):
    batch, seq, d = x.shape
    xs = _ln(x.reshape(batch * seq, d), ln_in_g, ln_in_b)
    bias_tiles = _bias_tiles(rel_bias, ATT_T)
    v_first = None
    for l in range(DEPTH):
        w_in = w_in_first if l == 0 else w_in_rest[l - 1]
        mu = mu_first if l == 0 else mu_rest[l - 1]
        w_diff = w_in[:, :N_DIFF].astype(BF16)
        w_rw = jnp.pad(w_in[:, N_DIFF:], ((0, 0), (0, RW_PAD - (w_in.shape[1] - N_DIFF)))).astype(BF16)
        qkv = _proj(xs, w_diff, BF16, "in_proj_diff")
        pr = _proj(xs, w_rw, F32, "in_proj_rwkv")

        lam_init = 0.8 - 0.6 * math.exp(-0.3 * l)
        lam_params = jnp.stack([lambda_q1[l], lambda_k1[l], lambda_q2[l], lambda_k2[l]])
        y_diff = _diff_attention(qkv, bias_tiles, rel_bias, lam_params, subln_g[l], lam_init, batch, seq)

        prm = _rwkv_params(l, mu, rw_w0, rw_w_up, rw_a0, rw_a_up, rw_g_up, rw_v0, rw_v_up,
                           rw_k_k, rw_k_a, rw_r_k, rw_gn_g, rw_gn_b)
        if l == 0:
            y_rw, v_first = _rwkv_mix(pr, None, prm, batch, seq)
        else:
            y_rw = _rwkv_mix(pr, v_first, prm, batch, seq)

        xs = _out_proj(y_diff, y_rw, xs, w_out[l].astype(BF16), ln_mix_g[l], ln_mix_b[l])
        xs = _mlp(xs, w_up[l].astype(BF16), w_down[l].astype(BF16), ln_ffn_g[l], ln_ffn_b[l])
    return xs.reshape(batch, seq, d)


def kernel(x, ln_in_g, ln_in_b, w_in_first, w_in_rest, mu_first, mu_rest, rel_bias, lambda_q1, lambda_k1, lambda_q2, lambda_k2, subln_g, rw_w0, rw_w_up, rw_a0, rw_a_up, rw_g_up, rw_v0, rw_v_up, rw_k_k, rw_k_a, rw_r_k, rw_gn_g, rw_gn_b, w_out, ln_mix_g, ln_mix_b, w_up, w_down, ln_ffn_g, ln_ffn_b):
    return _forward(x, ln_in_g, ln_in_b, w_in_first, w_in_rest, mu_first, mu_rest, rel_bias,
                    lambda_q1, lambda_k1, lambda_q2, lambda_k2, subln_g,
                    rw_w0, rw_w_up, rw_a0, rw_a_up, rw_g_up, rw_v0, rw_v_up,
                    rw_k_k, rw_k_a, rw_r_k, rw_gn_g, rw_gn_b,
                    w_out, ln_mix_g, ln_mix_b, w_up, w_down, ln_ffn_g, ln_ffn_b)
```

```python
import functools
import math

import jax
import jax.numpy as jnp
from jax import lax
from jax.experimental import pallas as pl
from jax.experimental.pallas import tpu as pltpu

F32 = jnp.float32
BF16 = jnp.bfloat16

D_MODEL = 1024
DEPTH = 2
D_DIFF = 512
D_RWKV = 512
DIFF_HEADS = 4
DIFF_HEAD_DIM = 64
DIFF_V_DIM = 128
RWKV_HEAD = 64
LORA_W = 64
LORA_A = 64
LORA_V = 32
LORA_G = 160
D_FF = 4 * D_MODEL
N_BUCKETS = 32
MAX_DISTANCE = 128
LN_EPS = 1e-5
SUBLN_EPS = 1e-5
GN_EPS = 64e-5
ALPHA = (2 * DEPTH) ** 0.25
N_DIFF = 3 * D_DIFF
RW_W = 3 * D_RWKV

LANES = 128
VMEM_LIMIT = 56 * 1024 * 1024

RW_PAD = 2048
ATT_T = 512
RW_TB = 512
RW_L = 64
RW_UNROLL = 2
NEG = -1e30


def _cparams(sem):
    return pltpu.CompilerParams(dimension_semantics=sem, vmem_limit_bytes=VMEM_LIMIT)


def _layer_norm(z, g, b):
    mu = jnp.mean(z, -1, keepdims=True)
    zc = z - mu
    var = jnp.mean(zc * zc, -1, keepdims=True)
    return zc * lax.rsqrt(var + LN_EPS) * g + b


def _ln_body(x_ref, g_ref, b_ref, o_ref):
    o_ref[...] = _layer_norm(x_ref[...], g_ref[...], b_ref[...])


def _ln(x, g, b, tm=512):
    t, d = x.shape
    return pl.pallas_call(
        _ln_body,
        grid=(t // tm,),
        in_specs=[pl.BlockSpec((tm, d), lambda i: (i, 0)),
                  pl.BlockSpec((1, d), lambda i: (0, 0)),
                  pl.BlockSpec((1, d), lambda i: (0, 0))],
        out_specs=pl.BlockSpec((tm, d), lambda i: (i, 0)),
        out_shape=jax.ShapeDtypeStruct((t, d), F32),
        compiler_params=_cparams(("parallel",)),
        name="ln_in",
    )(x, g.reshape(1, d), b.reshape(1, d))


def _proj_body(x_ref, w_ref, o_ref):
    o_ref[...] = jnp.dot(x_ref[...].astype(BF16), w_ref[...],
                         preferred_element_type=F32).astype(o_ref.dtype)


def _proj(x, w, out_dtype, name, tm=1024, tn=512):
    t, d = x.shape
    n = w.shape[1]
    return pl.pallas_call(
        _proj_body,
        grid=(t // tm, n // tn),
        in_specs=[pl.BlockSpec((tm, d), lambda i, j: (i, 0)),
                  pl.BlockSpec((d, tn), lambda i, j: (0, j))],
        out_specs=pl.BlockSpec((tm, tn), lambda i, j: (i, j)),
        out_shape=jax.ShapeDtypeStruct((t, n), out_dtype),
        compiler_params=_cparams(("parallel", "arbitrary")),
        name=name,
    )(x, w)


def _bias_body(rb_ref, o_ref, *, t):
    h = pl.program_id(0)
    which = pl.program_id(1)
    i = lax.broadcasted_iota(jnp.int32, (t, t), 0)
    j = lax.broadcasted_iota(jnp.int32, (t, t), 1)
    n = jnp.maximum(which * t + i - j, 0)
    max_exact = N_BUCKETS // 2
    nf = jnp.maximum(n, 1).astype(F32)
    large = max_exact + (jnp.log(nf / max_exact) / math.log(MAX_DISTANCE / max_exact)
                         * (N_BUCKETS - max_exact)).astype(jnp.int32)
    large = jnp.minimum(large, N_BUCKETS - 1)
    bucket = jnp.where(n < max_exact, n, large)
    acc = jnp.zeros((t, t), F32)
    for b in range(N_BUCKETS):
        acc = jnp.where(bucket == b, rb_ref[b, h], acc)
    o_ref[0, 0] = acc


def _bias_tiles(rel_bias, t):
    return pl.pallas_call(
        functools.partial(_bias_body, t=t),
        grid=(DIFF_HEADS, 2),
        in_specs=[pl.BlockSpec(memory_space=pltpu.SMEM)],
        out_specs=pl.BlockSpec((1, 1, t, t), lambda h, w: (h, w, 0, 0)),
        out_shape=jax.ShapeDtypeStruct((DIFF_HEADS, 2, t, t), F32),
        compiler_params=_cparams(("parallel", "parallel")),
        name="t5_bias_tiles",
    )(rel_bias)


def _attn_body(rb_ref, lamp_ref, q_ref, k_ref, v_ref, bias_ref, g_ref, o_ref,
               m_sc, acc_sc, *, t, lam_init):
    h = pl.program_id(1)
    iq = pl.program_id(2)
    lane = lax.broadcasted_iota(jnp.int32, (t, LANES), 1)
    q = q_ref[...].astype(F32) * (DIFF_HEAD_DIM ** -0.5)
    qm = (jnp.where(lane < DIFF_HEAD_DIM, q, 0.0).astype(BF16),
          jnp.where(lane < DIFF_HEAD_DIM, 0.0, q).astype(BF16))
    c_far = rb_ref[N_BUCKETS - 1, h]

    m_sc[...] = jnp.full(m_sc.shape, NEG, F32)
    acc_sc[...] = jnp.zeros(acc_sc.shape, F32)

    def tile(row0, width, bias, diag_col0):
        rows = pl.ds(pl.multiple_of(row0, t), width)
        kt = k_ref[rows, :]
        vt = jnp.concatenate([v_ref[rows, :], jnp.ones((width, LANES), BF16)], axis=1)
        maps = range(2)
        s = [lax.dot_general(qm[m], kt, (((1,), (1,)), ((), ())), preferred_element_type=F32)
             for m in maps]
        if bias is None:
            m_cur = [jnp.max(s[m], -1, keepdims=True) + c_far for m in maps]
        else:
            s = [s[m] + bias for m in maps]
            if diag_col0 is not None:
                ri = lax.broadcasted_iota(jnp.int32, (t, width), 0)
                ci = lax.broadcasted_iota(jnp.int32, (t, width), 1)
                s = [jnp.where(ri >= ci - diag_col0, s[m], NEG) for m in maps]
            m_cur = [jnp.max(s[m], -1, keepdims=True) for m in maps]
        m_old = [m_sc[m] for m in maps]
        m_new = [jnp.maximum(m_old[m], m_cur[m]) for m in maps]
        a = [jnp.exp(m_old[m] - m_new[m]) for m in maps]
        sub = [m_new[m] - c_far if bias is None else m_new[m] for m in maps]
        p = [jnp.exp(s[m] - jnp.concatenate([sub[m]] * (width // LANES), axis=1)).astype(BF16)
             for m in maps]
        pv = [jnp.dot(p[m], vt, preferred_element_type=F32) for m in maps]
        for m in maps:
            acc_sc[m] = jnp.concatenate([a[m], a[m]], axis=1) * acc_sc[m] + pv[m]
            m_sc[m] = m_new[m]

    n_far = jnp.maximum(iq - 1, 0)

    def far_pair(j, carry):
        tile(j * 2 * t, 2 * t, None, None)
        return carry

    lax.fori_loop(0, n_far >> 1, far_pair, 0)

    @pl.when((n_far & 1) == 1)
    def _():
        tile((n_far - 1) * t, t, None, None)

    @pl.when(iq > 0)
    def _():
        tile((iq - 1) * t, 2 * t, jnp.concatenate([bias_ref[0, 1], bias_ref[0, 0]], axis=1), t)

    @pl.when(iq == 0)
    def _():
        tile(0, t, bias_ref[0, 0], 0)

    lp = lamp_ref[...]
    lam = (jnp.exp(jnp.sum(lp[0:1] * lp[1:2], -1, keepdims=True))
           - jnp.exp(jnp.sum(lp[2:3] * lp[3:4], -1, keepdims=True)) + lam_init)
    a0 = acc_sc[0]
    a1 = acc_sc[1]
    out = a0[:, :LANES] / a0[:, LANES:] - lam * (a1[:, :LANES] / a1[:, LANES:])
    out = out * lax.rsqrt(jnp.mean(out * out, -1, keepdims=True) + SUBLN_EPS) * g_ref[...]
    o_ref[...] = out * (1.0 - lam_init)


def _diff_attention(qkv, bias_tiles, rel_bias, lam_params, subln_g, lam_init, batch, seq):
    t = ATT_T
    nq = seq // t
    hb = D_DIFF // LANES
    return pl.pallas_call(
        functools.partial(_attn_body, t=t, lam_init=lam_init),
        grid=(batch, DIFF_HEADS, nq),
        in_specs=[pl.BlockSpec(memory_space=pltpu.SMEM),
                  pl.BlockSpec((4, DIFF_HEAD_DIM), lambda b, h, i: (0, 0)),
                  pl.BlockSpec((t, LANES), lambda b, h, i: (b * nq + i, h)),
                  pl.BlockSpec((seq, LANES), lambda b, h, i: (b, hb + h)),
                  pl.BlockSpec((seq, LANES), lambda b, h, i: (b, 2 * hb + h)),
                  pl.BlockSpec((1, 2, t, t), lambda b, h, i: (h, 0, 0, 0)),
                  pl.BlockSpec((1, DIFF_V_DIM), lambda b, h, i: (0, 0))],
        out_specs=pl.BlockSpec((t, LANES), lambda b, h, i: (b * nq + i, h)),
        out_shape=jax.ShapeDtypeStruct((batch * seq, D_DIFF), F32),
        scratch_shapes=[pltpu.VMEM((2, t, LANES), F32),
                        pltpu.VMEM((2, t, DIFF_V_DIM + LANES), F32)],
        compiler_params=_cparams(("parallel", "parallel", "arbitrary")),
        name="diff_attention",
    )(rel_bias, lam_params, qkv, qkv, qkv, bias_tiles, subln_g.reshape(1, DIFF_V_DIM))


def _split_dot(x, m):
    hi = x.astype(BF16)
    lo = (x - hi.astype(F32)).astype(BF16)
    return (jnp.dot(hi, m, preferred_element_type=F32)
            + jnp.dot(lo, m, preferred_element_type=F32))


def _stack_heads(x):
    lane = lax.broadcasted_iota(jnp.int32, x.shape, 1)
    return jnp.concatenate([jnp.where(lane < RWKV_HEAD, x, 0.0),
                            jnp.where(lane < RWKV_HEAD, 0.0, x)], axis=0)


def _bdot(a, b):
    return jnp.dot(a.astype(BF16), b.astype(BF16), preferred_element_type=F32)


def _rwkv_body(*refs, tb, first):
    if first:
        (pr_ref, mu_ref, w0_ref, a0_ref, kk_ref, ka_ref, rk_ref, gng_ref, gnb_ref,
         wup_ref, aup_ref, gup1_ref, gup2_ref,
         y_ref, vout_ref,
         prev_sc, h_sc, r_sc, k_sc, v_sc, as_sc, bs_sc, ci_sc, lw_sc, y_sc) = refs
    else:
        (pr_ref, vf_ref, mu_ref, w0_ref, a0_ref, v0_ref, kk_ref, ka_ref, rk_ref, gng_ref, gnb_ref,
         wup_ref, aup_ref, gup1_ref, gup2_ref, vup_ref,
         y_ref,
         prev_sc, h_sc, r_sc, k_sc, v_sc, as_sc, bs_sc, ci_sc, lw_sc, y_sc) = refs
    L = RW_L
    c = D_RWKV

    @pl.when(pl.program_id(1) == 0)
    def _():
        prev_sc[...] = jnp.zeros(prev_sc.shape, F32)
        h_sc[...] = jnp.zeros(h_sc.shape, F32)

    row = lax.broadcasted_iota(jnp.int32, (tb, 1), 0)

    def shifted(c0, c1):
        cur = pr_ref[:, c0:c1]
        prev = jnp.where(row == 0, prev_sc[:, c0:c1], pltpu.roll(cur, 1, 0))
        return cur + (prev - cur) * mu_ref[:, c0:c1]

    r = shifted(0, c)
    k = shifted(c, 2 * c)
    v = shifted(2 * c, 3 * c)
    t1 = shifted(RW_W, RW_W + LANES)
    t2 = shifted(RW_W + LANES, RW_W + 2 * LANES)
    t3 = shifted(RW_W + 2 * LANES, RW_W + 3 * LANES)
    prev_sc[...] = pr_ref[tb - 1:tb, :]

    lw = _bdot(jnp.tanh(t1), wup_ref[...])
    la = _bdot(t1, aup_ref[...])
    g = _bdot(jax.nn.sigmoid(t2), gup1_ref[...]) + _bdot(jax.nn.sigmoid(t3), gup2_ref[...])

    z = -(w0_ref[...] + lw)
    softplus = jnp.maximum(z, 0.0) + jnp.log(1.0 + jnp.exp(-jnp.abs(z)))
    logw = -jnp.exp(-softplus - 0.5)
    a = jax.nn.sigmoid(a0_ref[...] + la)
    if first:
        vout_ref[...] = v
    else:
        v = v + (vf_ref[...] - v) * jax.nn.sigmoid(v0_ref[...] + _bdot(t3, vup_ref[...]))

    ri = lax.broadcasted_iota(jnp.int32, (c, c), 0)
    cj = lax.broadcasted_iota(jnp.int32, (c, c), 1)
    hs = RWKV_HEAD.bit_length() - 1
    same = (ri >> hs) == (cj >> hs)
    head_ones = jnp.where(same, 1.0, 0.0).astype(BF16)
    ls_ = L.bit_length() - 1
    ti = lax.broadcasted_iota(jnp.int32, (tb, tb), 0)
    tj = lax.broadcasted_iota(jnp.int32, (tb, tb), 1)
    tri = jnp.where(((ti >> ls_) == (tj >> ls_)) & (tj <= ti), 1.0, 0.0).astype(BF16)

    kk = k * kk_ref[...]
    kk = kk / jnp.maximum(jnp.sqrt(_split_dot(kk * kk, head_ones)), 1e-12)
    k2 = k * (1.0 + (a - 1.0) * ka_ref[...])

    hi = logw.astype(BF16)
    rem = logw - hi.astype(F32)
    mid = rem.astype(BF16)
    lo = (rem - mid.astype(F32)).astype(BF16)
    ci = (jnp.dot(tri, hi, preferred_element_type=F32) + jnp.dot(tri, mid, preferred_element_type=F32)
          + jnp.dot(tri, lo, preferred_element_type=F32))

    r_sc[...] = r
    k_sc[...] = k2
    v_sc[...] = v
    as_sc[...] = -kk
    bs_sc[...] = kk * a
    ci_sc[...] = ci
    lw_sc[...] = logw

    i2 = lax.broadcasted_iota(jnp.int32, (4 * L, 4 * L), 0)
    j2 = lax.broadcasted_iota(jnp.int32, (4 * L, 4 * L), 1)
    incl = jnp.where(i2 >= 2 * L, 1, 0)
    amask = ((((i2 >> ls_) & 1) == ((j2 >> ls_) & 1))
             & ((j2 & (L - 1)) < (i2 & (L - 1)) + incl))
    e_i = lax.broadcasted_iota(jnp.int32, (2 * L, 2 * L), 0)
    e_j = lax.broadcasted_iota(jnp.int32, (2 * L, 2 * L), 1)
    eye = e_i == e_j
    eye_f = jnp.where(eye, 1.0, 0.0)

    pairs = range(c // LANES)
    lsl = [slice(p * LANES, (p + 1) * LANES) for p in pairs]
    n_sq = int(math.log2(L)) - 2

    def chunk_group(grp, carry):
        cins = []
        for j in range(RW_UNROLL):
            ch = grp * RW_UNROLL + j
            rows = pl.ds(pl.multiple_of(ch * L, L), L)
            ci_c = ci_sc[rows, :]
            last = ci_sc[pl.ds(ch * L + L - 1, 1), :]
            e_neg = jnp.exp(-ci_c)
            e_end = jnp.exp(last - ci_c)
            bs_c = bs_sc[rows, :]
            k_c = k_sc[rows, :]
            cins.append(dict(
                rows=rows, g_end=jnp.exp(last),
                at=as_sc[rows, :] * jnp.exp(ci_c - lw_sc[rows, :]),
                bt=bs_c * e_neg, kt=k_c * e_neg, rt=r_sc[rows, :] * jnp.exp(ci_c),
                bh=bs_c * e_end, kh=k_c * e_end, v=v_sc[rows, :]))
        combos = [(j, p) for j in range(RW_UNROLL) for p in pairs]
        at_s = [_stack_heads(cins[j]["at"][:, lsl[p]]) for j, p in combos]
        rt_s = [_stack_heads(cins[j]["rt"][:, lsl[p]]) for j, p in combos]
        v_s = [_stack_heads(cins[j]["v"][:, lsl[p]]) for j, p in combos]
        a_all = []
        for i, (j, p) in enumerate(combos):
            lhs = jnp.concatenate([at_s[i], rt_s[i]], axis=0).astype(BF16)
            rhs = jnp.concatenate([_stack_heads(cins[j]["bt"][:, lsl[p]]),
                                   _stack_heads(cins[j]["kt"][:, lsl[p]])], axis=0).astype(BF16)
            a = lax.dot_general(lhs, rhs, (((1,), (1,)), ((), ())), preferred_element_type=F32)
            a_all.append(jnp.where(amask, a, 0.0))
        ids = range(len(combos))
        a_ab = [a[:2 * L, :2 * L] for a in a_all]
        a_r = [a[2 * L:, :] for a in a_all]
        pw = [_bdot(a, a) for a in a_ab]
        tinv = [eye_f + a for a in a_ab]
        akv = [_bdot(a_all[i][:2 * L, 2 * L:], v_s[i]) for i in ids]
        for _ in range(n_sq):
            for i in ids:
                pw_b = pw[i].astype(BF16)
                both = jnp.dot(jnp.concatenate([pw_b, tinv[i].astype(BF16)], axis=0), pw_b,
                               preferred_element_type=F32)
                pw[i] = both[:2 * L]
                tinv[i] = tinv[i] + both[2 * L:]
        tinv = [tinv[i] + _bdot(tinv[i], pw[i]) for i in ids]
        wu = [_bdot(tinv[i], jnp.concatenate([at_s[i], akv[i]], axis=1)) for i in ids]
        bk_t = [jnp.concatenate([_stack_heads(cins[j]["bh"][:, lsl[p]]),
                                 _stack_heads(cins[j]["kh"][:, lsl[p]])], axis=0).T.astype(BF16)
                for j, p in combos]
        h = [h_sc[p] for p in pairs]
        for j in range(RW_UNROLL):
            sel = [j * len(pairs) + p for p in pairs]
            wrg = [_bdot(jnp.concatenate(
                [wu[i][:, :2 * L], rt_s[i], jnp.where(eye, cins[j]["g_end"][:, lsl[p]], 0.0)],
                axis=0), h[p]) for p, i in zip(pairs, sel)]
            uv = [jnp.concatenate([wrg[p][:2 * L] + wu[i][:, 2 * L:], v_s[i]],
                                  axis=0).astype(BF16) for p, i in zip(pairs, sel)]
            for p, i in zip(pairs, sel):
                y_st = wrg[p][2 * L:4 * L] + jnp.dot(a_r[i].astype(BF16), uv[p],
                                                     preferred_element_type=F32)
                y_sc[cins[j]["rows"], lsl[p]] = y_st[:L] + y_st[L:]
            h = [wrg[p][4 * L:] + jnp.dot(bk_t[i], uv[p], preferred_element_type=F32)
                 for p, i in zip(pairs, sel)]
        for p in pairs:
            h_sc[p] = h[p]
        return carry

    lax.fori_loop(0, tb // (L * RW_UNROLL), chunk_group, 0)

    y = y_sc[...]
    mean = _split_dot(y, head_ones) * (1.0 / RWKV_HEAD)
    yc = y - mean
    var = _split_dot(yc * yc, head_ones) * (1.0 / RWKV_HEAD)
    yn = yc * lax.rsqrt(var + GN_EPS) * gng_ref[...] + gnb_ref[...]
    bonus = _split_dot(r * k2 * rk_ref[...], head_ones) * v
    y_ref[...] = (yn + bonus) * g


def _rwkv_mix(pr, v_first, prm, batch, seq):
    tb = RW_TB
    nt = seq // tb
    first = v_first is None
    c = D_RWKV
    row_spec = lambda w: pl.BlockSpec((tb, w), lambda b, t: (b * nt + t, 0))
    vec_spec = lambda w: pl.BlockSpec((1, w), lambda b, t: (0, 0))
    mat_spec = lambda: pl.BlockSpec((LANES, c), lambda b, t: (0, 0))
    vec = lambda x: x.reshape(1, -1)
    if first:
        args = [pr, vec(prm["mu"]), vec(prm["w0"]), vec(prm["a0"]), vec(prm["k_k"]), vec(prm["k_a"]),
                vec(prm["r_k"]), vec(prm["gn_g"]), vec(prm["gn_b"]),
                prm["w_up"], prm["a_up"], prm["g_up1"], prm["g_up2"]]
        in_specs = ([row_spec(RW_PAD), vec_spec(RW_PAD)] + [vec_spec(c)] * 7 + [mat_spec()] * 4)
        out_shape = [jax.ShapeDtypeStruct((batch * seq, c), F32)] * 2
        out_specs = [row_spec(c), row_spec(c)]
    else:
        args = [pr, v_first, vec(prm["mu"]), vec(prm["w0"]), vec(prm["a0"]), vec(prm["v0"]),
                vec(prm["k_k"]), vec(prm["k_a"]), vec(prm["r_k"]), vec(prm["gn_g"]), vec(prm["gn_b"]),
                prm["w_up"], prm["a_up"], prm["g_up1"], prm["g_up2"], prm["v_up"]]
        in_specs = ([row_spec(RW_PAD), row_spec(c), vec_spec(RW_PAD)] + [vec_spec(c)] * 8 + [mat_spec()] * 5)
        out_shape = jax.ShapeDtypeStruct((batch * seq, c), F32)
        out_specs = row_spec(c)
    scratch = [pltpu.VMEM((1, RW_PAD), F32), pltpu.VMEM((c // LANES, LANES, LANES), F32)]
    scratch += [pltpu.VMEM((tb, c), F32)] * 8
    return pl.pallas_call(
        functools.partial(_rwkv_body, tb=tb, first=first),
        grid=(batch, nt),
        in_specs=in_specs,
        out_specs=out_specs,
        out_shape=out_shape,
        scratch_shapes=scratch,
        compiler_params=_cparams(("parallel", "arbitrary")),
        name="rwkv7_mix_first" if first else "rwkv7_mix_rest",
    )(*args)


def _out_body(yd_ref, yr_ref, x_ref, w_ref, g_ref, b_ref, o_ref):
    mix = (jnp.dot(yd_ref[...].astype(BF16), w_ref[:D_DIFF, :], preferred_element_type=F32)
           + jnp.dot(yr_ref[...].astype(BF16), w_ref[D_DIFF:, :], preferred_element_type=F32))
    o_ref[...] = _layer_norm(ALPHA * x_ref[...] + mix, g_ref[...], b_ref[...])


def _out_proj(yd, yr, x, w, g, b, tm=512):
    t, d = x.shape
    return pl.pallas_call(
        _out_body,
        grid=(t // tm,),
        in_specs=[pl.BlockSpec((tm, D_DIFF), lambda i: (i, 0)),
                  pl.BlockSpec((tm, D_RWKV), lambda i: (i, 0)),
                  pl.BlockSpec((tm, d), lambda i: (i, 0)),
                  pl.BlockSpec((D_DIFF + D_RWKV, d), lambda i: (0, 0)),
                  pl.BlockSpec((1, d), lambda i: (0, 0)),
                  pl.BlockSpec((1, d), lambda i: (0, 0))],
        out_specs=pl.BlockSpec((tm, d), lambda i: (i, 0)),
        out_shape=jax.ShapeDtypeStruct((t, d), F32),
        compiler_params=_cparams(("parallel",)),
        name="out_proj_norm",
    )(yd, yr, x, w, g.reshape(1, d), b.reshape(1, d))


def _mlp_body(x_ref, wu_ref, wd_ref, g_ref, b_ref, o_ref, xb_sc, acc_sc):
    f = pl.program_id(1)

    @pl.when(f == 0)
    def _():
        xb_sc[...] = x_ref[...].astype(BF16)
        acc_sc[...] = jnp.zeros(acc_sc.shape, F32)

    h = jnp.maximum(jnp.dot(xb_sc[...], wu_ref[...], preferred_element_type=F32), 0.0)
    acc_sc[...] += jnp.dot((h * h).astype(BF16), wd_ref[...], preferred_element_type=F32)

    @pl.when(f == pl.num_programs(1) - 1)
    def _():
        o_ref[...] = _layer_norm(ALPHA * x_ref[...] + acc_sc[...], g_ref[...], b_ref[...])


def _mlp(x, wu, wd, g, b, tm=1024, tf=512):
    t, d = x.shape
    ff = wu.shape[1]
    return pl.pallas_call(
        _mlp_body,
        grid=(t // tm, ff // tf),
        in_specs=[pl.BlockSpec((tm, d), lambda i, f: (i, 0)),
                  pl.BlockSpec((d, tf), lambda i, f: (0, f)),
                  pl.BlockSpec((tf, d), lambda i, f: (f, 0)),
                  pl.BlockSpec((1, d), lambda i, f: (0, 0)),
                  pl.BlockSpec((1, d), lambda i, f: (0, 0))],
        out_specs=pl.BlockSpec((tm, d), lambda i, f: (i, 0)),
        out_shape=jax.ShapeDtypeStruct((t, d), F32),
        scratch_shapes=[pltpu.VMEM((tm, d), BF16), pltpu.VMEM((tm, d), F32)],
        compiler_params=_cparams(("parallel", "arbitrary")),
        name="mlp_norm",
    )(x, wu, wd, g.reshape(1, d), b.reshape(1, d))


def _pad_rows(w, before, total):
    return jnp.pad(w, ((before, total - before - w.shape[0]), (0, 0)))


def _rwkv_params(l, mu, rw_w0, rw_w_up, rw_a0, rw_a_up, rw_g_up, rw_v0, rw_v_up,
                 rw_k_k, rw_k_a, rw_r_k, rw_gn_g, rw_gn_b):
    prm = {
        "mu": jnp.pad(mu, (0, RW_PAD - mu.shape[0])),
        "w0": rw_w0[l], "a0": rw_a0[l], "k_k": rw_k_k[l], "k_a": rw_k_a[l],
        "r_k": rw_r_k[l].reshape(-1), "gn_g": rw_gn_g[l], "gn_b": rw_gn_b[l],
        "w_up": _pad_rows(rw_w_up[l], 0, LANES).astype(BF16),
        "a_up": _pad_rows(rw_a_up[l], LORA_W, LANES).astype(BF16),
        "g_up1": rw_g_up[l][:LANES].astype(BF16),
        "g_up2": _pad_rows(rw_g_up[l][LANES:], 0, LANES).astype(BF16),
    }
    if l > 0:
        prm["v0"] = rw_v0[l - 1]
        prm["v_up"] = _pad_rows(rw_v_up[l - 1], LORA_G - LANES, LANES).astype(BF16)
    return prm


def _forward(x, ln_in_g, ln_in_b, w_in_first, w_in_rest, mu_first, mu_rest, rel_bias,
             lambda_q1, lambda_k1, lambda_q2, lambda_k2, subln_g,
             rw_w0, rw_w_up, rw_a0, rw_a_up, rw_g_up, rw_v0, rw_v_up,
             rw_k_k, rw_k_a, rw_r_k, rw_gn_g, rw_gn_b,
             w_out, ln_mix_g, ln_mix_b, w_up, w_down, ln_ffn_g, ln_ffn_b):
    batch, seq, d = x.shape
    xs = _ln(x.reshape(batch * seq, d), ln_in_g, ln_in_b)
    bias_tiles = _bias_tiles(rel_bias, ATT_T)
    v_first = None
    for l in range(DEPTH):
        w_in = w_in_first if l == 0 else w_in_rest[l - 1]
        mu = mu_first if l == 0 else mu_rest[l - 1]
        w_diff = w_in[:, :N_DIFF].astype(BF16)
        w_rw = jnp.pad(w_in[:, N_DIFF:], ((0, 0), (0, RW_PAD - (w_in.shape[1] - N_DIFF)))).astype(BF16)
        qkv = _proj(xs, w_diff, BF16, "in_proj_diff")
        pr = _proj(xs, w_rw, F32, "in_proj_rwkv")

        lam_init = 0.8 - 0.6 * math.exp(-0.3 * l)
        lam_params = jnp.stack([lambda_q1[l], lambda_k1[l], lambda_q2[l], lambda_k2[l]])
        y_diff = _diff_attention(qkv, bias_tiles, rel_bias, lam_params, subln_g[l], lam_init, batch, seq)

        prm = _rwkv_params(l, mu, rw_w0, rw_w_up, rw_a0, rw_a_up, rw_g_up, rw_v0, rw_v_up,
                           rw_k_k, rw_k_a, rw_r_k, rw_gn_g, rw_gn_b)
        if l == 0:
            y_rw, v_first = _rwkv_mix(pr, None, prm, batch, seq)
        else:
            y_rw = _rwkv_mix(pr, v_first, prm, batch, seq)

        xs = _out_proj(y_diff, y_rw, xs, w_out[l].astype(BF16), ln_mix_g[l], ln_mix_b[l])
        xs = _mlp(xs, w_up[l].astype(BF16), w_down[l].astype(BF16), ln_ffn_g[l], ln_ffn_b[l])
    return xs.reshape(batch, seq, d)


def kernel(x, ln_in_g, ln_in_b, w_in_first, w_in_rest, mu_first, mu_rest, rel_bias, lambda_q1, lambda_k1, lambda_q2, lambda_k2, subln_g, rw_w0, rw_w_up, rw_a0, rw_a_up, rw_g_up, rw_v0, rw_v_up, rw_k_k, rw_k_a, rw_r_k, rw_gn_g, rw_gn_b, w_out, ln_mix_g, ln_mix_b, w_up, w_down, ln_ffn_g, ln_ffn_b):
    return _forward(x, ln_in_g, ln_in_b, w_in_first, w_in_rest, mu_first, mu_rest, rel_bias,
                    lambda_q1, lambda_k1, lambda_q2, lambda_k2, subln_g,
                    rw_w0, rw_w_up, rw_a0, rw_a_up, rw_g_up, rw_v0, rw_v_up,
                    rw_k_k, rw_k_a, rw_r_k, rw_gn_g, rw_gn_b,
                    w_out, ln_mix_g, ln_mix_b, w_up, w_down, ln_ffn_g, ln_ffn_b)
```

```python
import functools
import math

import jax
import jax.numpy as jnp
from jax import lax
from jax.experimental import pallas as pl
from jax.experimental.pallas import tpu as pltpu

F32 = jnp.float32
BF16 = jnp.bfloat16

D_MODEL = 1024
DEPTH = 2
D_DIFF = 512
D_RWKV = 512
DIFF_HEADS = 4
DIFF_HEAD_DIM = 64
DIFF_V_DIM = 128
RWKV_HEAD = 64
LORA_W = 64
LORA_A = 64
LORA_V = 32
LORA_G = 160
D_FF = 4 * D_MODEL
N_BUCKETS = 32
MAX_DISTANCE = 128
LN_EPS = 1e-5
SUBLN_EPS = 1e-5
GN_EPS = 64e-5
ALPHA = (2 * DEPTH) ** 0.25
N_DIFF = 3 * D_DIFF
RW_W = 3 * D_RWKV

LANES = 128
MXU_W = 256
VMEM_LIMIT = 56 * 1024 * 1024

RW_PAD = 2048
ATT_T = 512
RW_TB = 512
RW_L = 64
RW_UNROLL = 2
NEG = -1e30


def _cparams(sem):
    return pltpu.CompilerParams(dimension_semantics=sem, vmem_limit_bytes=VMEM_LIMIT)


def _layer_norm(z, g, b):
    mu = jnp.mean(z, -1, keepdims=True)
    zc = z - mu
    var = jnp.mean(zc * zc, -1, keepdims=True)
    return zc * lax.rsqrt(var + LN_EPS) * g + b


def _proj_body(*refs, with_ln, tn, tiles_per_seq):
    if with_ln:
        x_ref, g_ref, b_ref, w_ref, mu_ref, xn_ref, qkv_ref, pr_ref, carry_sc = refs
        xn = _layer_norm(x_ref[...], g_ref[...], b_ref[...])
        xn_ref[...] = xn
    else:
        x_ref, w_ref, mu_ref, qkv_ref, pr_ref, carry_sc = refs
        xn = x_ref[...]
    tm = x_ref.shape[0]
    xb = xn.astype(BF16)
    seq_start = (pl.program_id(0) % tiles_per_seq) == 0
    row8 = lax.broadcasted_iota(jnp.int32, (8, 1), 0)
    for j in range(w_ref.shape[1] // tn):
        o = jnp.dot(xb, w_ref[:, j * tn:(j + 1) * tn], preferred_element_type=F32)
        if (j + 1) * tn <= N_DIFF:
            qkv_ref[:, j * tn:(j + 1) * tn] = o.astype(BF16)
        else:
            cols = slice(j * tn - N_DIFF, (j + 1) * tn - N_DIFF)
            carry = jnp.where(seq_start, 0.0, carry_sc[:, cols])
            rolled = pltpu.roll(o, 1, 0)
            prev = jnp.concatenate([jnp.where(row8 == 0, carry, rolled[:8]), rolled[8:]], axis=0)
            carry_sc[:, cols] = o[tm - 1:tm, :]
            pr_ref[:, cols] = o + (prev - o) * mu_ref[:, cols]


def _in_proj(x, w, mu, ln, seq, tm=512, tn=512):
    t, d = x.shape
    n = w.shape[1]
    with_ln = ln is not None
    row = lambda wd: pl.BlockSpec((tm, wd), lambda i: (i, 0))
    const = lambda r, wd: pl.BlockSpec((r, wd), lambda i: (0, 0))
    in_specs = ([row(d)] + ([const(1, d), const(1, d)] if with_ln else [])
                + [const(d, n), const(1, n - N_DIFF)])
    args = ([x] + ([ln[0].reshape(1, d), ln[1].reshape(1, d)] if with_ln else [])
            + [w, mu.reshape(1, n - N_DIFF)])
    out_shape = [jax.ShapeDtypeStruct((t, N_DIFF), BF16), jax.ShapeDtypeStruct((t, n - N_DIFF), F32)]
    out_specs = [row(N_DIFF), row(n - N_DIFF)]
    if with_ln:
        out_shape = [jax.ShapeDtypeStruct((t, d), F32)] + out_shape
        out_specs = [row(d)] + out_specs
    return pl.pallas_call(
        functools.partial(_proj_body, with_ln=with_ln, tn=tn, tiles_per_seq=seq // tm),
        grid=(t // tm,),
        in_specs=in_specs,
        out_specs=out_specs,
        out_shape=out_shape,
        scratch_shapes=[pltpu.VMEM((1, n - N_DIFF), F32)],
        compiler_params=_cparams(("arbitrary",)),
        name="ln_in_proj" if with_ln else "in_proj",
    )(*args)


def _bias_body(rb_ref, o_ref, *, t):
    h = pl.program_id(0)
    which = pl.program_id(1)
    i = lax.broadcasted_iota(jnp.int32, (t, t), 0)
    j = lax.broadcasted_iota(jnp.int32, (t, t), 1)
    n = jnp.maximum(which * t + i - j, 0)
    max_exact = N_BUCKETS // 2
    nf = jnp.maximum(n, 1).astype(F32)
    large = max_exact + (jnp.log(nf / max_exact) / math.log(MAX_DISTANCE / max_exact)
                         * (N_BUCKETS - max_exact)).astype(jnp.int32)
    large = jnp.minimum(large, N_BUCKETS - 1)
    bucket = jnp.where(n < max_exact, n, large)
    acc = jnp.zeros((t, t), F32)
    for b in range(N_BUCKETS):
        acc = jnp.where(bucket == b, rb_ref[b, h], acc)
    o_ref[0, 0] = acc


def _bias_tiles(rel_bias, t):
    return pl.pallas_call(
        functools.partial(_bias_body, t=t),
        grid=(DIFF_HEADS, 2),
        in_specs=[pl.BlockSpec(memory_space=pltpu.SMEM)],
        out_specs=pl.BlockSpec((1, 1, t, t), lambda h, w: (h, w, 0, 0)),
        out_shape=jax.ShapeDtypeStruct((DIFF_HEADS, 2, t, t), F32),
        compiler_params=_cparams(("parallel", "parallel")),
        name="t5_bias_tiles",
    )(rel_bias)


def _attn_body(rb_ref, lamp_ref, q_ref, k_ref, v_ref, bias_ref, g_ref, o_ref,
               m_sc, acc_sc, *, t, lam_init):
    h = pl.program_id(1)
    iq = pl.program_id(2)
    lane = lax.broadcasted_iota(jnp.int32, (t, LANES), 1)
    q = q_ref[...].astype(F32) * (DIFF_HEAD_DIM ** -0.5)
    qm = (jnp.where(lane < DIFF_HEAD_DIM, q, 0.0).astype(BF16),
          jnp.where(lane < DIFF_HEAD_DIM, 0.0, q).astype(BF16))
    c_far = rb_ref[N_BUCKETS - 1, h]

    m_sc[...] = jnp.full(m_sc.shape, NEG, F32)
    acc_sc[...] = jnp.zeros(acc_sc.shape, F32)

    def tile(row0, width, bias, diag_col0):
        rows = pl.ds(pl.multiple_of(row0, t), width)
        kt = k_ref[rows, :]
        vt = jnp.concatenate([v_ref[rows, :], jnp.ones((width, LANES), BF16)], axis=1)
        maps = range(2)
        s = [lax.dot_general(qm[m], kt, (((1,), (1,)), ((), ())), preferred_element_type=F32)
             for m in maps]
        if bias is None:
            m_cur = [jnp.max(s[m], -1, keepdims=True) + c_far for m in maps]
        else:
            s = [s[m] + bias for m in maps]
            if diag_col0 is not None:
                ri = lax.broadcasted_iota(jnp.int32, (t, width), 0)
                ci = lax.broadcasted_iota(jnp.int32, (t, width), 1)
                s = [jnp.where(ri >= ci - diag_col0, s[m], NEG) for m in maps]
            m_cur = [jnp.max(s[m], -1, keepdims=True) for m in maps]
        m_old = [m_sc[m] for m in maps]
        m_new = [jnp.maximum(m_old[m], m_cur[m]) for m in maps]
        a = [jnp.exp(m_old[m] - m_new[m]) for m in maps]
        sub = [m_new[m] - c_far if bias is None else m_new[m] for m in maps]
        p = [jnp.exp(s[m] - jnp.concatenate([sub[m]] * (width // LANES), axis=1)).astype(BF16)
             for m in maps]
        pv = [jnp.dot(p[m], vt, preferred_element_type=F32) for m in maps]
        for m in maps:
            acc_sc[m] = jnp.concatenate([a[m], a[m]], axis=1) * acc_sc[m] + pv[m]
            m_sc[m] = m_new[m]

    n_far = jnp.maximum(iq - 1, 0)

    def far_pair(j, carry):
        tile(j * 2 * t, 2 * t, None, None)
        return carry

    lax.fori_loop(0, n_far >> 1, far_pair, 0)

    @pl.when((n_far & 1) == 1)
    def _():
        tile((n_far - 1) * t, t, None, None)

    @pl.when(iq > 0)
    def _():
        tile((iq - 1) * t, 2 * t, jnp.concatenate([bias_ref[0, 1], bias_ref[0, 0]], axis=1), t)

    @pl.when(iq == 0)
    def _():
        tile(0, t, bias_ref[0, 0], 0)

    lp = lamp_ref[...]
    lam = (jnp.exp(jnp.sum(lp[0:1] * lp[1:2], -1, keepdims=True))
           - jnp.exp(jnp.sum(lp[2:3] * lp[3:4], -1, keepdims=True)) + lam_init)
    a0 = acc_sc[0]
    a1 = acc_sc[1]
    out = a0[:, :LANES] / a0[:, LANES:] - lam * (a1[:, :LANES] / a1[:, LANES:])
    out = out * lax.rsqrt(jnp.mean(out * out, -1, keepdims=True) + SUBLN_EPS) * g_ref[...]
    o_ref[...] = (out * (1.0 - lam_init)).astype(o_ref.dtype)


def _diff_attention(qkv, bias_tiles, rel_bias, lam_params, subln_g, lam_init, batch, seq):
    t = ATT_T
    nq = seq // t
    hb = D_DIFF // LANES
    return pl.pallas_call(
        functools.partial(_attn_body, t=t, lam_init=lam_init),
        grid=(batch, DIFF_HEADS, nq),
        in_specs=[pl.BlockSpec(memory_space=pltpu.SMEM),
                  pl.BlockSpec((4, DIFF_HEAD_DIM), lambda b, h, i: (0, 0)),
                  pl.BlockSpec((t, LANES), lambda b, h, i: (b * nq + i, h)),
                  pl.BlockSpec((seq, LANES), lambda b, h, i: (b, hb + h)),
                  pl.BlockSpec((seq, LANES), lambda b, h, i: (b, 2 * hb + h)),
                  pl.BlockSpec((1, 2, t, t), lambda b, h, i: (h, 0, 0, 0)),
                  pl.BlockSpec((1, DIFF_V_DIM), lambda b, h, i: (0, 0))],
        out_specs=pl.BlockSpec((t, LANES), lambda b, h, i: (b * nq + i, h)),
        out_shape=jax.ShapeDtypeStruct((batch * seq, D_DIFF), BF16),
        scratch_shapes=[pltpu.VMEM((2, t, LANES), F32),
                        pltpu.VMEM((2, t, DIFF_V_DIM + LANES), F32)],
        compiler_params=_cparams(("parallel", "parallel", "arbitrary")),
        name="diff_attention",
    )(rel_bias, lam_params, qkv, qkv, qkv, bias_tiles, subln_g.reshape(1, DIFF_V_DIM))


def _split_dot(x, m):
    hi = x.astype(BF16)
    lo = (x - hi.astype(F32)).astype(BF16)
    return (jnp.dot(hi, m, preferred_element_type=F32)
            + jnp.dot(lo, m, preferred_element_type=F32))


def _stack_heads(x):
    lane = lax.broadcasted_iota(jnp.int32, x.shape, 1)
    return jnp.concatenate([jnp.where(lane < RWKV_HEAD, x, 0.0),
                            jnp.where(lane < RWKV_HEAD, 0.0, x)], axis=0)


def _bdot(a, b):
    return jnp.dot(a.astype(BF16), b.astype(BF16), preferred_element_type=F32)


def _rwkv_body(*refs, tb, first):
    if first:
        (pr_ref, w0_ref, a0_ref, kk_ref, ka_ref, rk_ref, gng_ref, gnb_ref,
         wup_ref, aup_ref, gup1_ref, gup2_ref,
         y_ref, vout_ref,
         h_sc, r_sc, k_sc, v_sc, as_sc, bs_sc, ci_sc, lw_sc, y_sc) = refs
    else:
        (pr_ref, vf_ref, w0_ref, a0_ref, v0_ref, kk_ref, ka_ref, rk_ref, gng_ref, gnb_ref,
         wup_ref, aup_ref, gup1_ref, gup2_ref, vup_ref,
         y_ref,
         h_sc, r_sc, k_sc, v_sc, as_sc, bs_sc, ci_sc, lw_sc, y_sc) = refs
    L = RW_L
    c = D_RWKV

    @pl.when(pl.program_id(1) == 0)
    def _():
        h_sc[...] = jnp.zeros(h_sc.shape, F32)

    r = pr_ref[:, 0:c]
    k = pr_ref[:, c:2 * c]
    v = pr_ref[:, 2 * c:3 * c]
    t1 = pr_ref[:, RW_W:RW_W + LANES]
    t2 = pr_ref[:, RW_W + LANES:RW_W + 2 * LANES]
    t3 = pr_ref[:, RW_W + 2 * LANES:RW_W + 3 * LANES]

    lw = _bdot(jnp.tanh(t1), wup_ref[...])
    la = _bdot(t1, aup_ref[...])
    g = _bdot(jax.nn.sigmoid(t2), gup1_ref[...]) + _bdot(jax.nn.sigmoid(t3), gup2_ref[...])

    z = -(w0_ref[...] + lw)
    softplus = jnp.maximum(z, 0.0) + jnp.log(1.0 + jnp.exp(-jnp.abs(z)))
    logw = -jnp.exp(-softplus - 0.5)
    a = jax.nn.sigmoid(a0_ref[...] + la)
    if first:
        vout_ref[...] = v
    else:
        v = v + (vf_ref[...] - v) * jax.nn.sigmoid(v0_ref[...] + _bdot(t3, vup_ref[...]))

    bi = lax.broadcasted_iota(jnp.int32, (MXU_W, MXU_W), 0)
    bj = lax.broadcasted_iota(jnp.int32, (MXU_W, MXU_W), 1)
    hs = RWKV_HEAD.bit_length() - 1
    head_ones = jnp.where((bi >> hs) == (bj >> hs), 1.0, 0.0).astype(BF16)
    ls_ = L.bit_length() - 1
    tri = jnp.where(((bi >> ls_) == (bj >> ls_)) & (bj <= bi), 1.0, 0.0).astype(BF16)

    def head_sum(x):
        return jnp.concatenate([_split_dot(x[:, i:i + MXU_W], head_ones)
                                for i in range(0, c, MXU_W)], axis=1)

    kk = k * kk_ref[...]
    kk = kk * jnp.minimum(lax.rsqrt(head_sum(kk * kk)), 1e12)
    k2 = k * (1.0 + (a - 1.0) * ka_ref[...])

    hi = logw.astype(BF16)
    rem = logw - hi.astype(F32)
    mid = rem.astype(BF16)
    lo = (rem - mid.astype(F32)).astype(BF16)
    ci = jnp.concatenate(
        [jnp.dot(tri, hi[i:i + MXU_W], preferred_element_type=F32)
         + jnp.dot(tri, mid[i:i + MXU_W], preferred_element_type=F32)
         + jnp.dot(tri, lo[i:i + MXU_W], preferred_element_type=F32)
         for i in range(0, tb, MXU_W)], axis=0)

    r_sc[...] = r
    k_sc[...] = k2
    v_sc[...] = v
    as_sc[...] = -kk
    bs_sc[...] = kk * a
    ci_sc[...] = ci
    lw_sc[...] = logw

    i2 = lax.broadcasted_iota(jnp.int32, (4 * L, 4 * L), 0)
    j2 = lax.broadcasted_iota(jnp.int32, (4 * L, 4 * L), 1)
    incl = jnp.where(i2 >= 2 * L, 1, 0)
    amask = ((((i2 >> ls_) & 1) == ((j2 >> ls_) & 1))
             & ((j2 & (L - 1)) < (i2 & (L - 1)) + incl))
    e_i = lax.broadcasted_iota(jnp.int32, (2 * L, 2 * L), 0)
    e_j = lax.broadcasted_iota(jnp.int32, (2 * L, 2 * L), 1)
    eye = e_i == e_j
    eye_f = jnp.where(eye, 1.0, 0.0)

    pairs = range(c // LANES)
    lsl = [slice(p * LANES, (p + 1) * LANES) for p in pairs]
    n_sq = int(math.log2(L)) - 2

    def chunk_group(grp, carry):
        cins = []
        for j in range(RW_UNROLL):
            ch = grp * RW_UNROLL + j
            rows = pl.ds(pl.multiple_of(ch * L, L), L)
            ci_c = ci_sc[rows, :]
            last = ci_sc[pl.ds(ch * L + L - 1, 1), :]
            e_neg = jnp.exp(-ci_c)
            e_end = jnp.exp(last - ci_c)
            bs_c = bs_sc[rows, :]
            k_c = k_sc[rows, :]
            cins.append(dict(
                rows=rows, g_end=jnp.exp(last),
                at=as_sc[rows, :] * jnp.exp(ci_c - lw_sc[rows, :]),
                bt=bs_c * e_neg, kt=k_c * e_neg, rt=r_sc[rows, :] * jnp.exp(ci_c),
                bh=bs_c * e_end, kh=k_c * e_end, v=v_sc[rows, :]))
        combos = [(j, p) for j in range(RW_UNROLL) for p in pairs]
        at_s = [_stack_heads(cins[j]["at"][:, lsl[p]]) for j, p in combos]
        rt_s = [_stack_heads(cins[j]["rt"][:, lsl[p]]) for j, p in combos]
        v_s = [_stack_heads(cins[j]["v"][:, lsl[p]]) for j, p in combos]
        a_all = []
        for i, (j, p) in enumerate(combos):
            lhs = jnp.concatenate([at_s[i], rt_s[i]], axis=0).astype(BF16)
            rhs = jnp.concatenate([_stack_heads(cins[j]["bt"][:, lsl[p]]),
                                   _stack_heads(cins[j]["kt"][:, lsl[p]])], axis=0).astype(BF16)
            a = lax.dot_general(lhs, rhs, (((1,), (1,)), ((), ())), preferred_element_type=F32)
            a_all.append(jnp.where(amask, a, 0.0))
        ids = range(len(combos))
        a_ab = [a[:2 * L, :2 * L] for a in a_all]
        a_r = [a[2 * L:, :] for a in a_all]
        pw = [_bdot(a, a) for a in a_ab]
        tinv = [eye_f + a for a in a_ab]
        akv = [_bdot(a_all[i][:2 * L, 2 * L:], v_s[i]) for i in ids]
        for _ in range(n_sq):
            for i in ids:
                pw_b = pw[i].astype(BF16)
                both = jnp.dot(jnp.concatenate([pw_b, tinv[i].astype(BF16)], axis=0), pw_b,
                               preferred_element_type=F32)
                pw[i] = both[:2 * L]
                tinv[i] = tinv[i] + both[2 * L:]
        tinv = [tinv[i] + _bdot(tinv[i], pw[i]) for i in ids]
        wu = [_bdot(tinv[i], jnp.concatenate([at_s[i], akv[i]], axis=1)) for i in ids]
        bk_t = [jnp.concatenate([_stack_heads(cins[j]["bh"][:, lsl[p]]),
                                 _stack_heads(cins[j]["kh"][:, lsl[p]])], axis=0).T.astype(BF16)
                for j, p in combos]
        h = [h_sc[p] for p in pairs]
        for j in range(RW_UNROLL):
            sel = [j * len(pairs) + p for p in pairs]
            wrg = [_bdot(jnp.concatenate(
                [wu[i][:, :2 * L], rt_s[i], jnp.where(eye, cins[j]["g_end"][:, lsl[p]], 0.0)],
                axis=0), h[p]) for p, i in zip(pairs, sel)]
            uv = [jnp.concatenate([wrg[p][:2 * L] + wu[i][:, 2 * L:], v_s[i]],
                                  axis=0).astype(BF16) for p, i in zip(pairs, sel)]
            for p, i in zip(pairs, sel):
                y_st = wrg[p][2 * L:4 * L] + jnp.dot(a_r[i].astype(BF16), uv[p],
                                                     preferred_element_type=F32)
                y_sc[cins[j]["rows"], lsl[p]] = y_st[:L] + y_st[L:]
            h = [wrg[p][4 * L:] + jnp.dot(bk_t[i], uv[p], preferred_element_type=F32)
                 for p, i in zip(pairs, sel)]
        for p in pairs:
            h_sc[p] = h[p]
        return carry

    lax.fori_loop(0, tb // (L * RW_UNROLL), chunk_group, 0)

    y = y_sc[...]
    mean = head_sum(y) * (1.0 / RWKV_HEAD)
    yc = y - mean
    var = head_sum(yc * yc) * (1.0 / RWKV_HEAD)
    yn = yc * lax.rsqrt(var + GN_EPS) * gng_ref[...] + gnb_ref[...]
    bonus = head_sum(r * k2 * rk_ref[...]) * v
    y_ref[...] = ((yn + bonus) * g).astype(y_ref.dtype)


def _rwkv_mix(pr, v_first, prm, batch, seq):
    tb = RW_TB
    nt = seq // tb
    first = v_first is None
    c = D_RWKV
    row_spec = lambda w: pl.BlockSpec((tb, w), lambda b, t: (b * nt + t, 0))
    vec_spec = lambda w: pl.BlockSpec((1, w), lambda b, t: (0, 0))
    mat_spec = lambda: pl.BlockSpec((LANES, c), lambda b, t: (0, 0))
    vec = lambda x: x.reshape(1, -1)
    if first:
        args = [pr, vec(prm["w0"]), vec(prm["a0"]), vec(prm["k_k"]), vec(prm["k_a"]),
                vec(prm["r_k"]), vec(prm["gn_g"]), vec(prm["gn_b"]),
                prm["w_up"], prm["a_up"], prm["g_up1"], prm["g_up2"]]
        in_specs = [row_spec(RW_PAD)] + [vec_spec(c)] * 7 + [mat_spec()] * 4
        out_shape = [jax.ShapeDtypeStruct((batch * seq, c), BF16),
                     jax.ShapeDtypeStruct((batch * seq, c), F32)]
        out_specs = [row_spec(c), row_spec(c)]
    else:
        args = [pr, v_first, vec(prm["w0"]), vec(prm["a0"]), vec(prm["v0"]),
                vec(prm["k_k"]), vec(prm["k_a"]), vec(prm["r_k"]), vec(prm["gn_g"]), vec(prm["gn_b"]),
                prm["w_up"], prm["a_up"], prm["g_up1"], prm["g_up2"], prm["v_up"]]
        in_specs = [row_spec(RW_PAD), row_spec(c)] + [vec_spec(c)] * 8 + [mat_spec()] * 5
        out_shape = jax.ShapeDtypeStruct((batch * seq, c), BF16)
        out_specs = row_spec(c)
    scratch = [pltpu.VMEM((c // LANES, LANES, LANES), F32)]
    scratch += [pltpu.VMEM((tb, c), F32)] * 8
    return pl.pallas_call(
        functools.partial(_rwkv_body, tb=tb, first=first),
        grid=(batch, nt),
        in_specs=in_specs,
        out_specs=out_specs,
        out_shape=out_shape,
        scratch_shapes=scratch,
        compiler_params=_cparams(("parallel", "arbitrary")),
        name="rwkv7_mix_first" if first else "rwkv7_mix_rest",
    )(*args)


def _out_body(yd_ref, yr_ref, x_ref, w_ref, g_ref, b_ref, o_ref):
    mix = (jnp.dot(yd_ref[...], w_ref[:D_DIFF, :], preferred_element_type=F32)
           + jnp.dot(yr_ref[...], w_ref[D_DIFF:, :], preferred_element_type=F32))
    o_ref[...] = _layer_norm(ALPHA * x_ref[...] + mix, g_ref[...], b_ref[...])


def _out_proj(yd, yr, x, w, g, b, tm=512):
    t, d = x.shape
    return pl.pallas_call(
        _out_body,
        grid=(t // tm,),
        in_specs=[pl.BlockSpec((tm, D_DIFF), lambda i: (i, 0)),
                  pl.BlockSpec((tm, D_RWKV), lambda i: (i, 0)),
                  pl.BlockSpec((tm, d), lambda i: (i, 0)),
                  pl.BlockSpec((D_DIFF + D_RWKV, d), lambda i: (0, 0)),
                  pl.BlockSpec((1, d), lambda i: (0, 0)),
                  pl.BlockSpec((1, d), lambda i: (0, 0))],
        out_specs=pl.BlockSpec((tm, d), lambda i: (i, 0)),
        out_shape=jax.ShapeDtypeStruct((t, d), F32),
        compiler_params=_cparams(("parallel",)),
        name="out_proj_norm",
    )(yd, yr, x, w, g.reshape(1, d), b.reshape(1, d))


def _mlp_body(x_ref, wu_ref, wd_ref, g_ref, b_ref, o_ref, xb_sc, acc_sc):
    f = pl.program_id(1)

    @pl.when(f == 0)
    def _():
        xb_sc[...] = x_ref[...].astype(BF16)
        acc_sc[...] = jnp.zeros(acc_sc.shape, F32)

    h = jnp.maximum(jnp.dot(xb_sc[...], wu_ref[...], preferred_element_type=F32), 0.0)
    acc_sc[...] += jnp.dot((h * h).astype(BF16), wd_ref[...], preferred_element_type=F32)

    @pl.when(f == pl.num_programs(1) - 1)
    def _():
        o_ref[...] = _layer_norm(ALPHA * x_ref[...] + acc_sc[...], g_ref[...], b_ref[...])


def _mlp(x, wu, wd, g, b, tm=1024, tf=1024):
    t, d = x.shape
    ff = wu.shape[1]
    return pl.pallas_call(
        _mlp_body,
        grid=(t // tm, ff // tf),
        in_specs=[pl.BlockSpec((tm, d), lambda i, f: (i, 0)),
                  pl.BlockSpec((d, tf), lambda i, f: (0, f)),
                  pl.BlockSpec((tf, d), lambda i, f: (f, 0)),
                  pl.BlockSpec((1, d), lambda i, f: (0, 0)),
                  pl.BlockSpec((1, d), lambda i, f: (0, 0))],
        out_specs=pl.BlockSpec((tm, d), lambda i, f: (i, 0)),
        out_shape=jax.ShapeDtypeStruct((t, d), F32),
        scratch_shapes=[pltpu.VMEM((tm, d), BF16), pltpu.VMEM((tm, d), F32)],
        compiler_params=_cparams(("parallel", "arbitrary")),
        name="mlp_norm",
    )(x, wu, wd, g.reshape(1, d), b.reshape(1, d))


def _pad_rows(w, before, total):
    return jnp.pad(w, ((before, total - before - w.shape[0]), (0, 0)))


def _rwkv_params(l, rw_w0, rw_w_up, rw_a0, rw_a_up, rw_g_up, rw_v0, rw_v_up,
                 rw_k_k, rw_k_a, rw_r_k, rw_gn_g, rw_gn_b):
    prm = {
        "w0": rw_w0[l], "a0": rw_a0[l], "k_k": rw_k_k[l], "k_a": rw_k_a[l],
        "r_k": rw_r_k[l].reshape(-1), "gn_g": rw_gn_g[l], "gn_b": rw_gn_b[l],
        "w_up": _pad_rows(rw_w_up[l], 0, LANES).astype(BF16),
        "a_up": _pad_rows(rw_a_up[l], LORA_W, LANES).astype(BF16),
        "g_up1": rw_g_up[l][:LANES].astype(BF16),
        "g_up2": _pad_rows(rw_g_up[l][LANES:], 0, LANES).astype(BF16),
    }
    if l > 0:
        prm["v0"] = rw_v0[l - 1]
        prm["v_up"] = _pad_rows(rw_v_up[l - 1], LORA_G - LANES, LANES).astype(BF16)
    return prm


def _forward(x, ln_in_g, ln_in_b, w_in_first, w_in_rest, mu_first, mu_rest, rel_bias,
             lambda_q1, lambda_k1, lambda_q2, lambda_k2, subln_g,
             rw_w0, rw_w_up, rw_a0, rw_a_up, rw_g_up, rw_v0, rw_v_up,
             rw_k_k, rw_k_a, rw_r_k, rw_gn_g, rw_gn_b,
             w_out, ln_mix_g, ln_mix_b, w_up, w_down, ln_ffn_g, ln_ffn_b):
    batch, seq, d = x.shape
    xs = x.reshape(batch * seq, d)
    bias_tiles = _bias_tiles(rel_bias, ATT_T)
    v_first = None
    for l in range(DEPTH):
        w_in = w_in_first if l == 0 else w_in_rest[l - 1]
        mu = mu_first if l == 0 else mu_rest[l - 1]
        w_pad = jnp.pad(w_in, ((0, 0), (0, N_DIFF + RW_PAD - w_in.shape[1]))).astype(BF16)
        mu_pad = jnp.pad(mu, (0, RW_PAD - mu.shape[0]))
        if l == 0:
            xs, qkv, pr = _in_proj(xs, w_pad, mu_pad, (ln_in_g, ln_in_b), seq)
        else:
            qkv, pr = _in_proj(xs, w_pad, mu_pad, None, seq)

        lam_init = 0.8 - 0.6 * math.exp(-0.3 * l)
        lam_params = jnp.stack([lambda_q1[l], lambda_k1[l], lambda_q2[l], lambda_k2[l]])
        y_diff = _diff_attention(qkv, bias_tiles, rel_bias, lam_params, subln_g[l], lam_init, batch, seq)

        prm = _rwkv_params(l, rw_w0, rw_w_up, rw_a0, rw_a_up, rw_g_up, rw_v0, rw_v_up,
                           rw_k_k, rw_k_a, rw_r_k, rw_gn_g, rw_gn_b)
        if l == 0:
            y_rw, v_first = _rwkv_mix(pr, None, prm, batch, seq)
        else:
            y_rw = _rwkv_mix(pr, v_first, prm, batch, seq)

        xs = _out_proj(y_diff, y_rw, xs, w_out[l].astype(BF16), ln_mix_g[l], ln_mix_b[l])
        xs = _mlp(xs, w_up[l].astype(BF16), w_down[l].astype(BF16), ln_ffn_g[l], ln_ffn_b[l])
    return xs.reshape(batch, seq, d)


def kernel(x, ln_in_g, ln_in_b, w_in_first, w_in_rest, mu_first, mu_rest, rel_bias, lambda_q1, lambda_k1, lambda_q2, lambda_k2, subln_g, rw_w0, rw_w_up, rw_a0, rw_a_up, rw_g_up, rw_v0, rw_v_up, rw_k_k, rw_k_a, rw_r_k, rw_gn_g, rw_gn_b, w_out, ln_mix_g, ln_mix_b, w_up, w_down, ln_ffn_g, ln_ffn_b):
    return _forward(x, ln_in_g, ln_in_b, w_in_first, w_in_rest, mu_first, mu_rest, rel_bias,
                    lambda_q1, lambda_k1, lambda_q2, lambda_k2, subln_g,
                    rw_w0, rw_w_up, rw_a0, rw_a_up, rw_g_up, rw_v0, rw_v_up,
                    rw_k_k, rw_k_a, rw_r_k, rw_gn_g, rw_gn_b,
                    w_out, ln_mix_g, ln_mix_b, w_up, w_down, ln_ffn_g, ln_ffn_b)
```

```python
import functools
import math

import jax
import jax.numpy as jnp
from jax import lax
from jax.experimental import pallas as pl
from jax.experimental.pallas import tpu as pltpu

F32 = jnp.float32
BF16 = jnp.bfloat16

D_MODEL = 1024
DEPTH = 2
D_DIFF = 512
D_RWKV = 512
DIFF_HEADS = 4
DIFF_HEAD_DIM = 64
DIFF_V_DIM = 128
RWKV_HEAD = 64
LORA_W = 64
LORA_A = 64
LORA_V = 32
LORA_G = 160
D_FF = 4 * D_MODEL
N_BUCKETS = 32
MAX_DISTANCE = 128
LN_EPS = 1e-5
SUBLN_EPS = 1e-5
GN_EPS = 64e-5
ALPHA = (2 * DEPTH) ** 0.25
N_DIFF = 3 * D_DIFF
RW_W = 3 * D_RWKV

LANES = 128
MXU_W = 256
VMEM_LIMIT = 56 * 1024 * 1024

RW_PAD = 2048
ATT_T = 512
RW_TB = 512
RW_L = 64
RW_UNROLL = 4
NEG = -1e30


def _cparams(sem):
    return pltpu.CompilerParams(dimension_semantics=sem, vmem_limit_bytes=VMEM_LIMIT)


def _layer_norm(z, g, b):
    mu = jnp.mean(z, -1, keepdims=True)
    zc = z - mu
    var = jnp.mean(zc * zc, -1, keepdims=True)
    return zc * lax.rsqrt(var + LN_EPS) * g + b


def _proj_body(*refs, with_ln, tn, tiles_per_seq):
    if with_ln:
        x_ref, g_ref, b_ref, w_ref, mu_ref, xn_ref, qkv_ref, pr_ref, carry_sc = refs
        xn = _layer_norm(x_ref[...], g_ref[...], b_ref[...])
        xn_ref[...] = xn
    else:
        x_ref, w_ref, mu_ref, qkv_ref, pr_ref, carry_sc = refs
        xn = x_ref[...]
    tm = x_ref.shape[0]
    xb = xn.astype(BF16)
    seq_start = (pl.program_id(0) % tiles_per_seq) == 0
    row8 = lax.broadcasted_iota(jnp.int32, (8, 1), 0)
    for j in range(w_ref.shape[1] // tn):
        o = jnp.dot(xb, w_ref[:, j * tn:(j + 1) * tn], preferred_element_type=F32)
        if (j + 1) * tn <= N_DIFF:
            qkv_ref[:, j * tn:(j + 1) * tn] = o.astype(BF16)
        else:
            cols = slice(j * tn - N_DIFF, (j + 1) * tn - N_DIFF)
            carry = jnp.where(seq_start, 0.0, carry_sc[:, cols])
            rolled = pltpu.roll(o, 1, 0)
            prev = jnp.concatenate([jnp.where(row8 == 0, carry, rolled[:8]), rolled[8:]], axis=0)
            carry_sc[:, cols] = o[tm - 1:tm, :]
            pr_ref[:, cols] = o + (prev - o) * mu_ref[:, cols]


def _in_proj(x, w, mu, ln, seq, tm=512, tn=512):
    t, d = x.shape
    n = w.shape[1]
    with_ln = ln is not None
    row = lambda wd: pl.BlockSpec((tm, wd), lambda i: (i, 0))
    const = lambda r, wd: pl.BlockSpec((r, wd), lambda i: (0, 0))
    in_specs = ([row(d)] + ([const(1, d), const(1, d)] if with_ln else [])
                + [const(d, n), const(1, n - N_DIFF)])
    args = ([x] + ([ln[0].reshape(1, d), ln[1].reshape(1, d)] if with_ln else [])
            + [w, mu.reshape(1, n - N_DIFF)])
    out_shape = [jax.ShapeDtypeStruct((t, N_DIFF), BF16), jax.ShapeDtypeStruct((t, n - N_DIFF), F32)]
    out_specs = [row(N_DIFF), row(n - N_DIFF)]
    if with_ln:
        out_shape = [jax.ShapeDtypeStruct((t, d), F32)] + out_shape
        out_specs = [row(d)] + out_specs
    return pl.pallas_call(
        functools.partial(_proj_body, with_ln=with_ln, tn=tn, tiles_per_seq=seq // tm),
        grid=(t // tm,),
        in_specs=in_specs,
        out_specs=out_specs,
        out_shape=out_shape,
        scratch_shapes=[pltpu.VMEM((1, n - N_DIFF), F32)],
        compiler_params=_cparams(("arbitrary",)),
        name="ln_in_proj" if with_ln else "in_proj",
    )(*args)


def _bias_body(rb_ref, o_ref, *, t):
    h = pl.program_id(0)
    which = pl.program_id(1)
    i = lax.broadcasted_iota(jnp.int32, (t, t), 0)
    j = lax.broadcasted_iota(jnp.int32, (t, t), 1)
    n = jnp.maximum(which * t + i - j, 0)
    max_exact = N_BUCKETS // 2
    nf = jnp.maximum(n, 1).astype(F32)
    large = max_exact + (jnp.log(nf / max_exact) / math.log(MAX_DISTANCE / max_exact)
                         * (N_BUCKETS - max_exact)).astype(jnp.int32)
    large = jnp.minimum(large, N_BUCKETS - 1)
    bucket = jnp.where(n < max_exact, n, large)
    acc = jnp.zeros((t, t), F32)
    for b in range(N_BUCKETS):
        acc = jnp.where(bucket == b, rb_ref[b, h], acc)
    o_ref[0, 0] = acc


def _bias_tiles(rel_bias, t):
    return pl.pallas_call(
        functools.partial(_bias_body, t=t),
        grid=(DIFF_HEADS, 2),
        in_specs=[pl.BlockSpec(memory_space=pltpu.SMEM)],
        out_specs=pl.BlockSpec((1, 1, t, t), lambda h, w: (h, w, 0, 0)),
        out_shape=jax.ShapeDtypeStruct((DIFF_HEADS, 2, t, t), F32),
        compiler_params=_cparams(("parallel", "parallel")),
        name="t5_bias_tiles",
    )(rel_bias)


def _attn_body(rb_ref, lamp_ref, q_ref, k_ref, v_ref, bias_ref, g_ref, o_ref,
               m_sc, acc_sc, *, t, nq, lam_init):
    h = pl.program_id(1)
    lane = lax.broadcasted_iota(jnp.int32, (t, LANES), 1)
    c_far = rb_ref[N_BUCKETS - 1, h]
    lp = lamp_ref[...]
    lam = (jnp.exp(jnp.sum(lp[0:1] * lp[1:2], -1, keepdims=True))
           - jnp.exp(jnp.sum(lp[2:3] * lp[3:4], -1, keepdims=True)) + lam_init)

    def q_tile(iq, carry):
        qrows = pl.ds(pl.multiple_of(iq * t, t), t)
        q = q_ref[qrows, :].astype(F32) * (DIFF_HEAD_DIM ** -0.5)
        qm = (jnp.where(lane < DIFF_HEAD_DIM, q, 0.0).astype(BF16),
              jnp.where(lane < DIFF_HEAD_DIM, 0.0, q).astype(BF16))
        m_sc[...] = jnp.full(m_sc.shape, NEG, F32)
        acc_sc[...] = jnp.zeros(acc_sc.shape, F32)

        def tile(row0, width, bias, diag_col0):
            rows = pl.ds(pl.multiple_of(row0, t), width)
            kt = k_ref[rows, :]
            vt = jnp.concatenate([v_ref[rows, :], jnp.ones((width, LANES), BF16)], axis=1)
            maps = range(2)
            s = [lax.dot_general(qm[m], kt, (((1,), (1,)), ((), ())), preferred_element_type=F32)
                 for m in maps]
            if bias is None:
                m_cur = [jnp.max(s[m], -1, keepdims=True) + c_far for m in maps]
            else:
                s = [s[m] + bias for m in maps]
                if diag_col0 is not None:
                    ri = lax.broadcasted_iota(jnp.int32, (t, width), 0)
                    ci = lax.broadcasted_iota(jnp.int32, (t, width), 1)
                    s = [jnp.where(ri >= ci - diag_col0, s[m], NEG) for m in maps]
                m_cur = [jnp.max(s[m], -1, keepdims=True) for m in maps]
            m_old = [m_sc[m] for m in maps]
            m_new = [jnp.maximum(m_old[m], m_cur[m]) for m in maps]
            a = [jnp.exp(m_old[m] - m_new[m]) for m in maps]
            sub = [m_new[m] - c_far if bias is None else m_new[m] for m in maps]
            p = [jnp.exp(s[m] - jnp.concatenate([sub[m]] * (width // LANES), axis=1)).astype(BF16)
                 for m in maps]
            pv = [jnp.dot(p[m], vt, preferred_element_type=F32) for m in maps]
            for m in maps:
                acc_sc[m] = jnp.concatenate([a[m], a[m]], axis=1) * acc_sc[m] + pv[m]
                m_sc[m] = m_new[m]

        n_far = jnp.maximum(iq - 1, 0)

        def far_pair(j, c):
            tile(j * 2 * t, 2 * t, None, None)
            return c

        lax.fori_loop(0, n_far >> 1, far_pair, 0)
        near = jnp.concatenate([bias_ref[0, 1], bias_ref[0, 0]], axis=1)

        @pl.when((n_far & 1) == 1)
        def _():
            tile((iq - 2) * t, 3 * t,
                 jnp.concatenate([jnp.full((t, t), c_far, F32), near], axis=1), 2 * t)

        @pl.when((iq > 0) & ((n_far & 1) == 0))
        def _():
            tile((iq - 1) * t, 2 * t, near, t)

        @pl.when(iq == 0)
        def _():
            tile(0, t, bias_ref[0, 0], 0)

        a0 = acc_sc[0]
        a1 = acc_sc[1]
        out = a0[:, :LANES] / a0[:, LANES:] - lam * (a1[:, :LANES] / a1[:, LANES:])
        out = out * lax.rsqrt(jnp.mean(out * out, -1, keepdims=True) + SUBLN_EPS) * g_ref[...]
        o_ref[qrows, :] = (out * (1.0 - lam_init)).astype(o_ref.dtype)
        return carry

    lax.fori_loop(0, nq, q_tile, 0)


def _diff_attention(qkv, bias_tiles, rel_bias, lam_params, subln_g, lam_init, batch, seq):
    t = ATT_T
    hb = D_DIFF // LANES
    return pl.pallas_call(
        functools.partial(_attn_body, t=t, nq=seq // t, lam_init=lam_init),
        grid=(batch, DIFF_HEADS),
        in_specs=[pl.BlockSpec(memory_space=pltpu.SMEM),
                  pl.BlockSpec((4, DIFF_HEAD_DIM), lambda b, h: (0, 0)),
                  pl.BlockSpec((seq, LANES), lambda b, h: (b, h)),
                  pl.BlockSpec((seq, LANES), lambda b, h: (b, hb + h)),
                  pl.BlockSpec((seq, LANES), lambda b, h: (b, 2 * hb + h)),
                  pl.BlockSpec((1, 2, t, t), lambda b, h: (h, 0, 0, 0)),
                  pl.BlockSpec((1, DIFF_V_DIM), lambda b, h: (0, 0))],
        out_specs=pl.BlockSpec((seq, LANES), lambda b, h: (b, h)),
        out_shape=jax.ShapeDtypeStruct((batch * seq, D_DIFF), BF16),
        scratch_shapes=[pltpu.VMEM((2, t, LANES), F32),
                        pltpu.VMEM((2, t, DIFF_V_DIM + LANES), F32)],
        compiler_params=_cparams(("parallel", "parallel")),
        name="diff_attention",
    )(rel_bias, lam_params, qkv, qkv, qkv, bias_tiles, subln_g.reshape(1, DIFF_V_DIM))


def _split_dot(x, m):
    hi = x.astype(BF16)
    lo = (x - hi.astype(F32)).astype(BF16)
    return (jnp.dot(hi, m, preferred_element_type=F32)
            + jnp.dot(lo, m, preferred_element_type=F32))


def _stack_heads(x):
    lane = lax.broadcasted_iota(jnp.int32, x.shape, 1)
    return jnp.concatenate([jnp.where(lane < RWKV_HEAD, x, 0.0),
                            jnp.where(lane < RWKV_HEAD, 0.0, x)], axis=0)


def _bdot(a, b):
    return jnp.dot(a.astype(BF16), b.astype(BF16), preferred_element_type=F32)


def _rwkv_body(*refs, tb, first):
    if first:
        (pr_ref, w0_ref, a0_ref, kk_ref, ka_ref, rk_ref, gng_ref, gnb_ref,
         wup_ref, aup_ref, gup1_ref, gup2_ref,
         y_ref, vout_ref,
         h_sc, r_sc, k_sc, v_sc, as_sc, bs_sc, ci_sc, lw_sc, y_sc) = refs
    else:
        (pr_ref, vf_ref, w0_ref, a0_ref, v0_ref, kk_ref, ka_ref, rk_ref, gng_ref, gnb_ref,
         wup_ref, aup_ref, gup1_ref, gup2_ref, vup_ref,
         y_ref,
         h_sc, r_sc, k_sc, v_sc, as_sc, bs_sc, ci_sc, lw_sc, y_sc) = refs
    L = RW_L
    c = D_RWKV

    @pl.when(pl.program_id(1) == 0)
    def _():
        h_sc[...] = jnp.zeros(h_sc.shape, F32)

    r = pr_ref[:, 0:c]
    k = pr_ref[:, c:2 * c]
    v = pr_ref[:, 2 * c:3 * c]
    t1 = pr_ref[:, RW_W:RW_W + LANES]
    t2 = pr_ref[:, RW_W + LANES:RW_W + 2 * LANES]
    t3 = pr_ref[:, RW_W + 2 * LANES:RW_W + 3 * LANES]

    lw = _bdot(jnp.tanh(t1), wup_ref[...])
    la = _bdot(t1, aup_ref[...])
    g = _bdot(jax.nn.sigmoid(t2), gup1_ref[...]) + _bdot(jax.nn.sigmoid(t3), gup2_ref[...])

    z = -(w0_ref[...] + lw)
    softplus = jnp.maximum(z, 0.0) + jnp.log(1.0 + jnp.exp(-jnp.abs(z)))
    logw = -jnp.exp(-softplus - 0.5)
    a = jax.nn.sigmoid(a0_ref[...] + la)
    if first:
        vout_ref[...] = v
    else:
        v = v + (vf_ref[...] - v) * jax.nn.sigmoid(v0_ref[...] + _bdot(t3, vup_ref[...]))

    bi = lax.broadcasted_iota(jnp.int32, (MXU_W, MXU_W), 0)
    bj = lax.broadcasted_iota(jnp.int32, (MXU_W, MXU_W), 1)
    hs = RWKV_HEAD.bit_length() - 1
    head_ones = jnp.where((bi >> hs) == (bj >> hs), 1.0, 0.0).astype(BF16)
    ls_ = L.bit_length() - 1
    tri = jnp.where(((bi >> ls_) == (bj >> ls_)) & (bj <= bi), 1.0, 0.0).astype(BF16)

    def head_sum(x):
        return jnp.concatenate([_split_dot(x[:, i:i + MXU_W], head_ones)
                                for i in range(0, c, MXU_W)], axis=1)

    kk = k * kk_ref[...]
    kk = kk * jnp.minimum(lax.rsqrt(head_sum(kk * kk)), 1e12)
    k2 = k * (1.0 + (a - 1.0) * ka_ref[...])

    hi = logw.astype(BF16)
    rem = logw - hi.astype(F32)
    mid = rem.astype(BF16)
    lo = (rem - mid.astype(F32)).astype(BF16)
    ci = jnp.concatenate(
        [jnp.dot(tri, hi[i:i + MXU_W], preferred_element_type=F32)
         + jnp.dot(tri, mid[i:i + MXU_W], preferred_element_type=F32)
         + jnp.dot(tri, lo[i:i + MXU_W], preferred_element_type=F32)
         for i in range(0, tb, MXU_W)], axis=0)

    r_sc[...] = r
    k_sc[...] = k2
    v_sc[...] = v
    as_sc[...] = -kk
    bs_sc[...] = kk * a
    ci_sc[...] = ci
    lw_sc[...] = logw

    i2 = lax.broadcasted_iota(jnp.int32, (4 * L, 4 * L), 0)
    j2 = lax.broadcasted_iota(jnp.int32, (4 * L, 4 * L), 1)
    incl = jnp.where(i2 >= 2 * L, 1, 0)
    amask = ((((i2 >> ls_) & 1) == ((j2 >> ls_) & 1))
             & ((j2 & (L - 1)) < (i2 & (L - 1)) + incl))
    e_i = lax.broadcasted_iota(jnp.int32, (2 * L, 2 * L), 0)
    e_j = lax.broadcasted_iota(jnp.int32, (2 * L, 2 * L), 1)
    eye = e_i == e_j
    eye_f = jnp.where(eye, 1.0, 0.0)

    pairs = range(c // LANES)
    lsl = [slice(p * LANES, (p + 1) * LANES) for p in pairs]
    n_sq = int(math.log2(L)) - 2

    def chunk_group(grp, carry):
        cins = []
        for j in range(RW_UNROLL):
            ch = grp * RW_UNROLL + j
            rows = pl.ds(pl.multiple_of(ch * L, L), L)
            ci_c = ci_sc[rows, :]
            last = ci_sc[pl.ds(ch * L + L - 1, 1), :]
            e_neg = jnp.exp(-ci_c)
            e_end = jnp.exp(last - ci_c)
            bs_c = bs_sc[rows, :]
            k_c = k_sc[rows, :]
            cins.append(dict(
                rows=rows, g_end=jnp.exp(last),
                at=as_sc[rows, :] * jnp.exp(ci_c - lw_sc[rows, :]),
                bt=bs_c * e_neg, kt=k_c * e_neg, rt=r_sc[rows, :] * jnp.exp(ci_c),
                bh=bs_c * e_end, kh=k_c * e_end, v=v_sc[rows, :]))
        combos = [(j, p) for j in range(RW_UNROLL) for p in pairs]
        at_s = [_stack_heads(cins[j]["at"][:, lsl[p]]) for j, p in combos]
        rt_s = [_stack_heads(cins[j]["rt"][:, lsl[p]]) for j, p in combos]
        v_s = [_stack_heads(cins[j]["v"][:, lsl[p]]) for j, p in combos]
        a_all = []
        for i, (j, p) in enumerate(combos):
            lhs = jnp.concatenate([at_s[i], rt_s[i]], axis=0).astype(BF16)
            rhs = jnp.concatenate([_stack_heads(cins[j]["bt"][:, lsl[p]]),
                                   _stack_heads(cins[j]["kt"][:, lsl[p]])], axis=0).astype(BF16)
            a = lax.dot_general(lhs, rhs, (((1,), (1,)), ((), ())), preferred_element_type=F32)
            a_all.append(jnp.where(amask, a, 0.0))
        ids = range(len(combos))
        a_ab = [a[:2 * L, :2 * L] for a in a_all]
        a_r = [a[2 * L:, :] for a in a_all]
        pw = [_bdot(a, a) for a in a_ab]
        tinv = [eye_f + a for a in a_ab]
        akv = [_bdot(a_all[i][:2 * L, 2 * L:], v_s[i]) for i in ids]
        for _ in range(n_sq):
            for i in ids:
                pw_b = pw[i].astype(BF16)
                both = jnp.dot(jnp.concatenate([pw_b, tinv[i].astype(BF16)], axis=0), pw_b,
                               preferred_element_type=F32)
                pw[i] = both[:2 * L]
                tinv[i] = tinv[i] + both[2 * L:]
        tinv = [tinv[i] + _bdot(tinv[i], pw[i]) for i in ids]
        wu = [_bdot(tinv[i], jnp.concatenate([at_s[i], akv[i]], axis=1)) for i in ids]
        bk_t = [jnp.concatenate([_stack_heads(cins[j]["bh"][:, lsl[p]]),
                                 _stack_heads(cins[j]["kh"][:, lsl[p]])], axis=0).T.astype(BF16)
                for j, p in combos]
        h = [h_sc[p] for p in pairs]
        for j in range(RW_UNROLL):
            sel = [j * len(pairs) + p for p in pairs]
            wrg = [_bdot(jnp.concatenate(
                [wu[i][:, :2 * L], rt_s[i], jnp.where(eye, cins[j]["g_end"][:, lsl[p]], 0.0)],
                axis=0), h[p]) for p, i in zip(pairs, sel)]
            uv = [jnp.concatenate([wrg[p][:2 * L] + wu[i][:, 2 * L:], v_s[i]],
                                  axis=0).astype(BF16) for p, i in zip(pairs, sel)]
            for p, i in zip(pairs, sel):
                y_st = wrg[p][2 * L:4 * L] + jnp.dot(a_r[i].astype(BF16), uv[p],
                                                     preferred_element_type=F32)
                y_sc[cins[j]["rows"], lsl[p]] = y_st[:L] + y_st[L:]
            h = [wrg[p][4 * L:] + jnp.dot(bk_t[i], uv[p], preferred_element_type=F32)
                 for p, i in zip(pairs, sel)]
        for p in pairs:
            h_sc[p] = h[p]
        return carry

    lax.fori_loop(0, tb // (L * RW_UNROLL), chunk_group, 0)

    y = y_sc[...]
    mean = head_sum(y) * (1.0 / RWKV_HEAD)
    yc = y - mean
    var = head_sum(yc * yc) * (1.0 / RWKV_HEAD)
    yn = yc * lax.rsqrt(var + GN_EPS) * gng_ref[...] + gnb_ref[...]
    bonus = head_sum(r * k2 * rk_ref[...]) * v
    y_ref[...] = ((yn + bonus) * g).astype(y_ref.dtype)


def _rwkv_mix(pr, v_first, prm, batch, seq):
    tb = RW_TB
    nt = seq // tb
    first = v_first is None
    c = D_RWKV
    row_spec = lambda w: pl.BlockSpec((tb, w), lambda b, t: (b * nt + t, 0))
    vec_spec = lambda w: pl.BlockSpec((1, w), lambda b, t: (0, 0))
    mat_spec = lambda: pl.BlockSpec((LANES, c), lambda b, t: (0, 0))
    vec = lambda x: x.reshape(1, -1)
    if first:
        args = [pr, vec(prm["w0"]), vec(prm["a0"]), vec(prm["k_k"]), vec(prm["k_a"]),
                vec(prm["r_k"]), vec(prm["gn_g"]), vec(prm["gn_b"]),
                prm["w_up"], prm["a_up"], prm["g_up1"], prm["g_up2"]]
        in_specs = [row_spec(RW_PAD)] + [vec_spec(c)] * 7 + [mat_spec()] * 4
        out_shape = [jax.ShapeDtypeStruct((batch * seq, c), BF16),
                     jax.ShapeDtypeStruct((batch * seq, c), F32)]
        out_specs = [row_spec(c), row_spec(c)]
    else:
        args = [pr, v_first, vec(prm["w0"]), vec(prm["a0"]), vec(prm["v0"]),
                vec(prm["k_k"]), vec(prm["k_a"]), vec(prm["r_k"]), vec(prm["gn_g"]), vec(prm["gn_b"]),
                prm["w_up"], prm["a_up"], prm["g_up1"], prm["g_up2"], prm["v_up"]]
        in_specs = [row_spec(RW_PAD), row_spec(c)] + [vec_spec(c)] * 8 + [mat_spec()] * 5
        out_shape = jax.ShapeDtypeStruct((batch * seq, c), BF16)
        out_specs = row_spec(c)
    scratch = [pltpu.VMEM((c // LANES, LANES, LANES), F32)]
    scratch += [pltpu.VMEM((tb, c), F32)] * 8
    return pl.pallas_call(
        functools.partial(_rwkv_body, tb=tb, first=first),
        grid=(batch, nt),
        in_specs=in_specs,
        out_specs=out_specs,
        out_shape=out_shape,
        scratch_shapes=scratch,
        compiler_params=_cparams(("parallel", "arbitrary")),
        name="rwkv7_mix_first" if first else "rwkv7_mix_rest",
    )(*args)


def _out_body(yd_ref, yr_ref, x_ref, w_ref, g_ref, b_ref, o_ref):
    mix = (jnp.dot(yd_ref[...], w_ref[:D_DIFF, :], preferred_element_type=F32)
           + jnp.dot(yr_ref[...], w_ref[D_DIFF:, :], preferred_element_type=F32))
    o_ref[...] = _layer_norm(ALPHA * x_ref[...] + mix, g_ref[...], b_ref[...])


def _out_proj(yd, yr, x, w, g, b, tm=512):
    t, d = x.shape
    return pl.pallas_call(
        _out_body,
        grid=(t // tm,),
        in_specs=[pl.BlockSpec((tm, D_DIFF), lambda i: (i, 0)),
                  pl.BlockSpec((tm, D_RWKV), lambda i: (i, 0)),
                  pl.BlockSpec((tm, d), lambda i: (i, 0)),
                  pl.BlockSpec((D_DIFF + D_RWKV, d), lambda i: (0, 0)),
                  pl.BlockSpec((1, d), lambda i: (0, 0)),
                  pl.BlockSpec((1, d), lambda i: (0, 0))],
        out_specs=pl.BlockSpec((tm, d), lambda i: (i, 0)),
        out_shape=jax.ShapeDtypeStruct((t, d), F32),
        compiler_params=_cparams(("parallel",)),
        name="out_proj_norm",
    )(yd, yr, x, w, g.reshape(1, d), b.reshape(1, d))


def _mlp_body(x_ref, wu_ref, wd_ref, g_ref, b_ref, o_ref, xb_sc, acc_sc):
    f = pl.program_id(1)

    @pl.when(f == 0)
    def _():
        xb_sc[...] = x_ref[...].astype(BF16)
        acc_sc[...] = jnp.zeros(acc_sc.shape, F32)

    h = jnp.maximum(jnp.dot(xb_sc[...], wu_ref[...], preferred_element_type=F32), 0.0)
    acc_sc[...] += jnp.dot((h * h).astype(BF16), wd_ref[...], preferred_element_type=F32)

    @pl.when(f == pl.num_programs(1) - 1)
    def _():
        o_ref[...] = _layer_norm(ALPHA * x_ref[...] + acc_sc[...], g_ref[...], b_ref[...])


def _mlp(x, wu, wd, g, b, tm=1024, tf=1024):
    t, d = x.shape
    ff = wu.shape[1]
    return pl.pallas_call(
        _mlp_body,
        grid=(t // tm, ff // tf),
        in_specs=[pl.BlockSpec((tm, d), lambda i, f: (i, 0)),
                  pl.BlockSpec((d, tf), lambda i, f: (0, f)),
                  pl.BlockSpec((tf, d), lambda i, f: (f, 0)),
                  pl.BlockSpec((1, d), lambda i, f: (0, 0)),
                  pl.BlockSpec((1, d), lambda i, f: (0, 0))],
        out_specs=pl.BlockSpec((tm, d), lambda i, f: (i, 0)),
        out_shape=jax.ShapeDtypeStruct((t, d), F32),
        scratch_shapes=[pltpu.VMEM((tm, d), BF16), pltpu.VMEM((tm, d), F32)],
        compiler_params=_cparams(("parallel", "arbitrary")),
        name="mlp_norm",
    )(x, wu, wd, g.reshape(1, d), b.reshape(1, d))


def _pad_rows(w, before, total):
    return jnp.pad(w, ((before, total - before - w.shape[0]), (0, 0)))


def _rwkv_params(l, rw_w0, rw_w_up, rw_a0, rw_a_up, rw_g_up, rw_v0, rw_v_up,
                 rw_k_k, rw_k_a, rw_r_k, rw_gn_g, rw_gn_b):
    prm = {
        "w0": rw_w0[l], "a0": rw_a0[l], "k_k": rw_k_k[l], "k_a": rw_k_a[l],
        "r_k": rw_r_k[l].reshape(-1), "gn_g": rw_gn_g[l], "gn_b": rw_gn_b[l],
        "w_up": _pad_rows(rw_w_up[l], 0, LANES).astype(BF16),
        "a_up": _pad_rows(rw_a_up[l], LORA_W, LANES).astype(BF16),
        "g_up1": rw_g_up[l][:LANES].astype(BF16),
        "g_up2": _pad_rows(rw_g_up[l][LANES:], 0, LANES).astype(BF16),
    }
    if l > 0:
        prm["v0"] = rw_v0[l - 1]
        prm["v_up"] = _pad_rows(rw_v_up[l - 1], LORA_G - LANES, LANES).astype(BF16)
    return prm


def _forward(x, ln_in_g, ln_in_b, w_in_first, w_in_rest, mu_first, mu_rest, rel_bias,
             lambda_q1, lambda_k1, lambda_q2, lambda_k2, subln_g,
             rw_w0, rw_w_up, rw_a0, rw_a_up, rw_g_up, rw_v0, rw_v_up,
             rw_k_k, rw_k_a, rw_r_k, rw_gn_g, rw_gn_b,
             w_out, ln_mix_g, ln_mix_b, w_up, w_down, ln_ffn_g, ln_ffn_b):
    batch, seq, d = x.shape
    xs = x.reshape(batch * seq, d)
    bias_tiles = _bias_tiles(rel_bias, ATT_T)
    v_first = None
    for l in range(DEPTH):
        w_in = w_in_first if l == 0 else w_in_rest[l - 1]
        mu = mu_first if l == 0 else mu_rest[l - 1]
        w_pad = jnp.pad(w_in, ((0, 0), (0, N_DIFF + RW_PAD - w_in.shape[1]))).astype(BF16)
        mu_pad = jnp.pad(mu, (0, RW_PAD - mu.shape[0]))
        if l == 0:
            xs, qkv, pr = _in_proj(xs, w_pad, mu_pad, (ln_in_g, ln_in_b), seq)
        else:
            qkv, pr = _in_proj(xs, w_pad, mu_pad, None, seq)

        lam_init = 0.8 - 0.6 * math.exp(-0.3 * l)
        lam_params = jnp.stack([lambda_q1[l], lambda_k1[l], lambda_q2[l], lambda_k2[l]])
        y_diff = _diff_attention(qkv, bias_tiles, rel_bias, lam_params, subln_g[l], lam_init, batch, seq)

        prm = _rwkv_params(l, rw_w0, rw_w_up, rw_a0, rw_a_up, rw_g_up, rw_v0, rw_v_up,
                           rw_k_k, rw_k_a, rw_r_k, rw_gn_g, rw_gn_b)
        if l == 0:
            y_rw, v_first = _rwkv_mix(pr, None, prm, batch, seq)
        else:
            y_rw = _rwkv_mix(pr, v_first, prm, batch, seq)

        xs = _out_proj(y_diff, y_rw, xs, w_out[l].astype(BF16), ln_mix_g[l], ln_mix_b[l])
        xs = _mlp(xs, w_up[l].astype(BF16), w_down[l].astype(BF16), ln_ffn_g[l], ln_ffn_b[l])
    return xs.reshape(batch, seq, d)


def kernel(x, ln_in_g, ln_in_b, w_in_first, w_in_rest, mu_first, mu_rest, rel_bias, lambda_q1, lambda_k1, lambda_q2, lambda_k2, subln_g, rw_w0, rw_w_up, rw_a0, rw_a_up, rw_g_up, rw_v0, rw_v_up, rw_k_k, rw_k_a, rw_r_k, rw_gn_g, rw_gn_b, w_out, ln_mix_g, ln_mix_b, w_up, w_down, ln_ffn_g, ln_ffn_b):
    return _forward(x, ln_in_g, ln_in_b, w_in_first, w_in_rest, mu_first, mu_rest, rel_bias,
                    lambda_q1, lambda_k1, lambda_q2, lambda_k2, subln_g,
                    rw_w0, rw_w_up, rw_a0, rw_a_up, rw_g_up, rw_v0, rw_v_up,
                    rw_k_k, rw_k_a, rw_r_k, rw_gn_g, rw_gn_b,
                    w_out, ln_mix_g, ln_mix_b, w_up, w_down, ln_ffn_g, ln_ffn_b)
```

```python
import functools
import math

import jax
import jax.numpy as jnp
from jax import lax
from jax.experimental import pallas as pl
from jax.experimental.pallas import tpu as pltpu

F32 = jnp.float32
BF16 = jnp.bfloat16

D_MODEL = 1024
DEPTH = 2
D_DIFF = 512
D_RWKV = 512
DIFF_HEADS = 4
DIFF_HEAD_DIM = 64
DIFF_V_DIM = 128
RWKV_HEAD = 64
LORA_W = 64
LORA_A = 64
LORA_V = 32
LORA_G = 160
D_FF = 4 * D_MODEL
N_BUCKETS = 32
MAX_DISTANCE = 128
LN_EPS = 1e-5
SUBLN_EPS = 1e-5
GN_EPS = 64e-5
ALPHA = (2 * DEPTH) ** 0.25
N_DIFF = 3 * D_DIFF
RW_W = 3 * D_RWKV

LANES = 128
MXU_W = 256
VMEM_LIMIT = 56 * 1024 * 1024

RW_PAD = 2048
ATT_T = 512
RW_TB = 512
RW_L = 64
RW_UNROLL = 4
NEG = -1e30


def _cparams(sem):
    return pltpu.CompilerParams(dimension_semantics=sem, vmem_limit_bytes=VMEM_LIMIT)


def _layer_norm(z, g, b):
    mu = jnp.mean(z, -1, keepdims=True)
    zc = z - mu
    var = jnp.mean(zc * zc, -1, keepdims=True)
    return zc * lax.rsqrt(var + LN_EPS) * g + b


def _proj_body(*refs, with_ln, tn, tiles_per_seq):
    if with_ln:
        x_ref, g_ref, b_ref, w_ref, mu_ref, xn_ref, qkv_ref, pr_ref, carry_sc = refs
        xn = _layer_norm(x_ref[...], g_ref[...], b_ref[...])
        xn_ref[...] = xn
    else:
        x_ref, w_ref, mu_ref, qkv_ref, pr_ref, carry_sc = refs
        xn = x_ref[...]
    tm = x_ref.shape[0]
    xb = xn.astype(BF16)
    seq_start = (pl.program_id(0) % tiles_per_seq) == 0
    row8 = lax.broadcasted_iota(jnp.int32, (8, 1), 0)
    for j in range(w_ref.shape[1] // tn):
        o = jnp.dot(xb, w_ref[:, j * tn:(j + 1) * tn], preferred_element_type=F32)
        if (j + 1) * tn <= N_DIFF:
            qkv_ref[:, j * tn:(j + 1) * tn] = o.astype(BF16)
        else:
            cols = slice(j * tn - N_DIFF, (j + 1) * tn - N_DIFF)
            carry = jnp.where(seq_start, 0.0, carry_sc[:, cols])
            rolled = pltpu.roll(o, 1, 0)
            prev = jnp.concatenate([jnp.where(row8 == 0, carry, rolled[:8]), rolled[8:]], axis=0)
            carry_sc[:, cols] = o[tm - 1:tm, :]
            pr_ref[:, cols] = o + (prev - o) * mu_ref[:, cols]


def _in_proj(x, w, mu, ln, seq, tm=512, tn=512):
    t, d = x.shape
    n = w.shape[1]
    with_ln = ln is not None
    row = lambda wd: pl.BlockSpec((tm, wd), lambda i: (i, 0))
    const = lambda r, wd: pl.BlockSpec((r, wd), lambda i: (0, 0))
    in_specs = ([row(d)] + ([const(1, d), const(1, d)] if with_ln else [])
                + [const(d, n), const(1, n - N_DIFF)])
    args = ([x] + ([ln[0].reshape(1, d), ln[1].reshape(1, d)] if with_ln else [])
            + [w, mu.reshape(1, n - N_DIFF)])
    out_shape = [jax.ShapeDtypeStruct((t, N_DIFF), BF16), jax.ShapeDtypeStruct((t, n - N_DIFF), F32)]
    out_specs = [row(N_DIFF), row(n - N_DIFF)]
    if with_ln:
        out_shape = [jax.ShapeDtypeStruct((t, d), F32)] + out_shape
        out_specs = [row(d)] + out_specs
    return pl.pallas_call(
        functools.partial(_proj_body, with_ln=with_ln, tn=tn, tiles_per_seq=seq // tm),
        grid=(t // tm,),
        in_specs=in_specs,
        out_specs=out_specs,
        out_shape=out_shape,
        scratch_shapes=[pltpu.VMEM((1, n - N_DIFF), F32)],
        compiler_params=_cparams(("arbitrary",)),
        name="ln_in_proj" if with_ln else "in_proj",
    )(*args)


def _bias_body(rb_ref, o_ref, *, t):
    h = pl.program_id(0)
    which = pl.program_id(1)
    i = lax.broadcasted_iota(jnp.int32, (t, t), 0)
    j = lax.broadcasted_iota(jnp.int32, (t, t), 1)
    n = jnp.maximum(which * t + i - j, 0)
    max_exact = N_BUCKETS // 2
    nf = jnp.maximum(n, 1).astype(F32)
    large = max_exact + (jnp.log(nf / max_exact) / math.log(MAX_DISTANCE / max_exact)
                         * (N_BUCKETS - max_exact)).astype(jnp.int32)
    large = jnp.minimum(large, N_BUCKETS - 1)
    bucket = jnp.where(n < max_exact, n, large)
    acc = jnp.zeros((t, t), F32)
    for b in range(N_BUCKETS):
        acc = jnp.where(bucket == b, rb_ref[b, h], acc)
    o_ref[0, 0] = acc


def _bias_tiles(rel_bias, t):
    return pl.pallas_call(
        functools.partial(_bias_body, t=t),
        grid=(DIFF_HEADS, 2),
        in_specs=[pl.BlockSpec(memory_space=pltpu.SMEM)],
        out_specs=pl.BlockSpec((1, 1, t, t), lambda h, w: (h, w, 0, 0)),
        out_shape=jax.ShapeDtypeStruct((DIFF_HEADS, 2, t, t), F32),
        compiler_params=_cparams(("parallel", "parallel")),
        name="t5_bias_tiles",
    )(rel_bias)


def _attn_body(rb_ref, lamp_ref, q_ref, k_ref, v_ref, bias_ref, g_ref, o_ref,
               m_sc, acc_sc, *, t, nq, lam_init):
    h = pl.program_id(1)
    lane = lax.broadcasted_iota(jnp.int32, (t, LANES), 1)
    c_far = rb_ref[N_BUCKETS - 1, h]
    lp = lamp_ref[...]
    lam = (jnp.exp(jnp.sum(lp[0:1] * lp[1:2], -1, keepdims=True))
           - jnp.exp(jnp.sum(lp[2:3] * lp[3:4], -1, keepdims=True)) + lam_init)

    def q_tile(iq, carry):
        qrows = pl.ds(pl.multiple_of(iq * t, t), t)
        q = q_ref[qrows, :].astype(F32) * (DIFF_HEAD_DIM ** -0.5)
        qm = (jnp.where(lane < DIFF_HEAD_DIM, q, 0.0).astype(BF16),
              jnp.where(lane < DIFF_HEAD_DIM, 0.0, q).astype(BF16))
        m_sc[...] = jnp.full(m_sc.shape, NEG, F32)
        acc_sc[...] = jnp.zeros(acc_sc.shape, F32)

        def tile(row0, width, bias, diag_col0):
            rows = pl.ds(pl.multiple_of(row0, t), width)
            kt = k_ref[rows, :]
            vt = jnp.concatenate([v_ref[rows, :], jnp.ones((width, LANES), BF16)], axis=1)
            maps = range(2)
            s = [lax.dot_general(qm[m], kt, (((1,), (1,)), ((), ())), preferred_element_type=F32)
                 for m in maps]
            if bias is None:
                m_cur = [jnp.max(s[m], -1, keepdims=True) + c_far for m in maps]
            else:
                s = [s[m] + bias for m in maps]
                if diag_col0 is not None:
                    ri = lax.broadcasted_iota(jnp.int32, (t, width), 0)
                    ci = lax.broadcasted_iota(jnp.int32, (t, width), 1)
                    s = [jnp.where(ri >= ci - diag_col0, s[m], NEG) for m in maps]
                m_cur = [jnp.max(s[m], -1, keepdims=True) for m in maps]
            m_old = [m_sc[m] for m in maps]
            m_new = [jnp.maximum(m_old[m], m_cur[m]) for m in maps]
            a = [jnp.exp(m_old[m] - m_new[m]) for m in maps]
            sub = [m_new[m] - c_far if bias is None else m_new[m] for m in maps]
            p = [jnp.exp(s[m] - jnp.concatenate([sub[m]] * (width // LANES), axis=1)).astype(BF16)
                 for m in maps]
            pv = [jnp.dot(p[m], vt, preferred_element_type=F32) for m in maps]
            for m in maps:
                acc_sc[m] = jnp.concatenate([a[m], a[m]], axis=1) * acc_sc[m] + pv[m]
                m_sc[m] = m_new[m]

        n_far = jnp.maximum(iq - 1, 0)

        def far_pair(j, c):
            tile(j * 2 * t, 2 * t, None, None)
            return c

        lax.fori_loop(0, n_far >> 1, far_pair, 0)
        near = jnp.concatenate([bias_ref[0, 1], bias_ref[0, 0]], axis=1)

        @pl.when((n_far & 1) == 1)
        def _():
            tile((iq - 2) * t, 3 * t,
                 jnp.concatenate([jnp.full((t, t), c_far, F32), near], axis=1), 2 * t)

        @pl.when((iq > 0) & ((n_far & 1) == 0))
        def _():
            tile((iq - 1) * t, 2 * t, near, t)

        @pl.when(iq == 0)
        def _():
            tile(0, t, bias_ref[0, 0], 0)

        a0 = acc_sc[0]
        a1 = acc_sc[1]
        out = a0[:, :LANES] / a0[:, LANES:] - lam * (a1[:, :LANES] / a1[:, LANES:])
        out = out * lax.rsqrt(jnp.mean(out * out, -1, keepdims=True) + SUBLN_EPS) * g_ref[...]
        o_ref[qrows, :] = (out * (1.0 - lam_init)).astype(o_ref.dtype)
        return carry

    lax.fori_loop(0, nq, q_tile, 0)


def _diff_attention(qkv, bias_tiles, rel_bias, lam_params, subln_g, lam_init, batch, seq):
    t = ATT_T
    hb = D_DIFF // LANES
    return pl.pallas_call(
        functools.partial(_attn_body, t=t, nq=seq // t, lam_init=lam_init),
        grid=(batch, DIFF_HEADS),
        in_specs=[pl.BlockSpec(memory_space=pltpu.SMEM),
                  pl.BlockSpec((4, DIFF_HEAD_DIM), lambda b, h: (0, 0)),
                  pl.BlockSpec((seq, LANES), lambda b, h: (b, h)),
                  pl.BlockSpec((seq, LANES), lambda b, h: (b, hb + h)),
                  pl.BlockSpec((seq, LANES), lambda b, h: (b, 2 * hb + h)),
                  pl.BlockSpec((1, 2, t, t), lambda b, h: (h, 0, 0, 0)),
                  pl.BlockSpec((1, DIFF_V_DIM), lambda b, h: (0, 0))],
        out_specs=pl.BlockSpec((seq, LANES), lambda b, h: (b, h)),
        out_shape=jax.ShapeDtypeStruct((batch * seq, D_DIFF), BF16),
        scratch_shapes=[pltpu.VMEM((2, t, LANES), F32),
                        pltpu.VMEM((2, t, DIFF_V_DIM + LANES), F32)],
        compiler_params=_cparams(("parallel", "parallel")),
        name="diff_attention",
    )(rel_bias, lam_params, qkv, qkv, qkv, bias_tiles, subln_g.reshape(1, DIFF_V_DIM))


def _split_dot(x, m):
    hi = x.astype(BF16)
    lo = (x - hi.astype(F32)).astype(BF16)
    return (jnp.dot(hi, m, preferred_element_type=F32)
            + jnp.dot(lo, m, preferred_element_type=F32))


def _stack_heads(x):
    lane = lax.broadcasted_iota(jnp.int32, x.shape, 1)
    return jnp.concatenate([jnp.where(lane < RWKV_HEAD, x, 0.0),
                            jnp.where(lane < RWKV_HEAD, 0.0, x)], axis=0)


def _bdot(a, b):
    return jnp.dot(a.astype(BF16), b.astype(BF16), preferred_element_type=F32)


def _rwkv_body(*refs, tb, first):
    if first:
        (pr_ref, w0_ref, a0_ref, kk_ref, ka_ref, rk_ref, gng_ref, gnb_ref,
         wup_ref, aup_ref, gup1_ref, gup2_ref,
         y_ref, vout_ref,
         h_sc, r_sc, k_sc, v_sc, as_sc, bs_sc, ci_sc, lw_sc, g_sc, y_sc) = refs
    else:
        (pr_ref, vf_ref, w0_ref, a0_ref, v0_ref, kk_ref, ka_ref, rk_ref, gng_ref, gnb_ref,
         wup_ref, aup_ref, gup1_ref, gup2_ref, vup_ref,
         y_ref,
         h_sc, r_sc, k_sc, v_sc, as_sc, bs_sc, ci_sc, lw_sc, g_sc, y_sc) = refs
    L = RW_L
    c = D_RWKV
    grp_rows = L * RW_UNROLL
    assert grp_rows == MXU_W and tb % grp_rows == 0

    @pl.when(pl.program_id(1) == 0)
    def _():
        h_sc[...] = jnp.zeros(h_sc.shape, F32)

    bi = lax.broadcasted_iota(jnp.int32, (MXU_W, MXU_W), 0)
    bj = lax.broadcasted_iota(jnp.int32, (MXU_W, MXU_W), 1)
    hs = RWKV_HEAD.bit_length() - 1
    head_ones = jnp.where((bi >> hs) == (bj >> hs), 1.0, 0.0).astype(BF16)
    ls_ = L.bit_length() - 1
    tri = jnp.where(((bi >> ls_) == (bj >> ls_)) & (bj <= bi), 1.0, 0.0).astype(BF16)

    def head_sum(x):
        return jnp.concatenate([_split_dot(x[:, i:i + MXU_W], head_ones)
                                for i in range(0, c, MXU_W)], axis=1)

    def prep(r0):
        rows = slice(r0, r0 + MXU_W)
        r = pr_ref[rows, 0:c]
        k = pr_ref[rows, c:2 * c]
        v = pr_ref[rows, 2 * c:3 * c]
        t1 = pr_ref[rows, RW_W:RW_W + LANES]
        t2 = pr_ref[rows, RW_W + LANES:RW_W + 2 * LANES]
        t3 = pr_ref[rows, RW_W + 2 * LANES:RW_W + 3 * LANES]

        lw = _bdot(jnp.tanh(t1), wup_ref[...])
        la = _bdot(t1, aup_ref[...])
        g_sc[rows, :] = (_bdot(jax.nn.sigmoid(t2), gup1_ref[...])
                         + _bdot(jax.nn.sigmoid(t3), gup2_ref[...]))

        z = -(w0_ref[...] + lw)
        softplus = jnp.maximum(z, 0.0) + jnp.log(1.0 + jnp.exp(-jnp.abs(z)))
        logw = -jnp.exp(-softplus - 0.5)
        a = jax.nn.sigmoid(a0_ref[...] + la)
        if first:
            vout_ref[rows, :] = v
        else:
            v = v + (vf_ref[rows, :] - v) * jax.nn.sigmoid(v0_ref[...] + _bdot(t3, vup_ref[...]))

        kk = k * kk_ref[...]
        kk = kk * jnp.minimum(lax.rsqrt(head_sum(kk * kk)), 1e12)

        hi = logw.astype(BF16)
        rem = logw - hi.astype(F32)
        mid = rem.astype(BF16)
        lo = (rem - mid.astype(F32)).astype(BF16)
        ci_sc[rows, :] = (jnp.dot(tri, hi, preferred_element_type=F32)
                          + jnp.dot(tri, mid, preferred_element_type=F32)
                          + jnp.dot(tri, lo, preferred_element_type=F32))
        r_sc[rows, :] = r
        k_sc[rows, :] = k * (1.0 + (a - 1.0) * ka_ref[...])
        v_sc[rows, :] = v
        as_sc[rows, :] = -kk
        bs_sc[rows, :] = kk * a
        lw_sc[rows, :] = logw

    def post(r0):
        rows = slice(r0, r0 + MXU_W)
        y = y_sc[rows, :]
        mean = head_sum(y) * (1.0 / RWKV_HEAD)
        yc = y - mean
        var = head_sum(yc * yc) * (1.0 / RWKV_HEAD)
        yn = yc * lax.rsqrt(var + GN_EPS) * gng_ref[...] + gnb_ref[...]
        bonus = head_sum(r_sc[rows, :] * k_sc[rows, :] * rk_ref[...]) * v_sc[rows, :]
        y_ref[rows, :] = ((yn + bonus) * g_sc[rows, :]).astype(y_ref.dtype)

    i2 = lax.broadcasted_iota(jnp.int32, (4 * L, 4 * L), 0)
    j2 = lax.broadcasted_iota(jnp.int32, (4 * L, 4 * L), 1)
    incl = jnp.where(i2 >= 2 * L, 1, 0)
    amask = ((((i2 >> ls_) & 1) == ((j2 >> ls_) & 1))
             & ((j2 & (L - 1)) < (i2 & (L - 1)) + incl))
    e_i = lax.broadcasted_iota(jnp.int32, (2 * L, 2 * L), 0)
    e_j = lax.broadcasted_iota(jnp.int32, (2 * L, 2 * L), 1)
    eye = e_i == e_j
    eye_f = jnp.where(eye, 1.0, 0.0)

    pairs = range(c // LANES)
    lsl = [slice(p * LANES, (p + 1) * LANES) for p in pairs]
    n_sq = int(math.log2(L)) - 2

    def chunk_group(grp):
        cins = []
        for j in range(RW_UNROLL):
            ch = grp * RW_UNROLL + j
            rows = slice(ch * L, (ch + 1) * L)
            ci_c = ci_sc[rows, :]
            last = ci_sc[ch * L + L - 1:ch * L + L, :]
            e_neg = jnp.exp(-ci_c)
            e_end = jnp.exp(last - ci_c)
            bs_c = bs_sc[rows, :]
            k_c = k_sc[rows, :]
            cins.append(dict(
                rows=rows, g_end=jnp.exp(last),
                at=as_sc[rows, :] * jnp.exp(ci_c - lw_sc[rows, :]),
                bt=bs_c * e_neg, kt=k_c * e_neg, rt=r_sc[rows, :] * jnp.exp(ci_c),
                bh=bs_c * e_end, kh=k_c * e_end, v=v_sc[rows, :]))
        combos = [(j, p) for j in range(RW_UNROLL) for p in pairs]
        at_s = [_stack_heads(cins[j]["at"][:, lsl[p]]) for j, p in combos]
        rt_s = [_stack_heads(cins[j]["rt"][:, lsl[p]]) for j, p in combos]
        v_s = [_stack_heads(cins[j]["v"][:, lsl[p]]) for j, p in combos]
        a_all = []
        for i, (j, p) in enumerate(combos):
            lhs = jnp.concatenate([at_s[i], rt_s[i]], axis=0).astype(BF16)
            rhs = jnp.concatenate([_stack_heads(cins[j]["bt"][:, lsl[p]]),
                                   _stack_heads(cins[j]["kt"][:, lsl[p]])], axis=0).astype(BF16)
            a = lax.dot_general(lhs, rhs, (((1,), (1,)), ((), ())), preferred_element_type=F32)
            a_all.append(jnp.where(amask, a, 0.0))
        ids = range(len(combos))
        a_ab = [a[:2 * L, :2 * L] for a in a_all]
        a_r = [a[2 * L:, :] for a in a_all]
        pw = [_bdot(a, a) for a in a_ab]
        tinv = [eye_f + a for a in a_ab]
        akv = [_bdot(a_all[i][:2 * L, 2 * L:], v_s[i]) for i in ids]
        for _ in range(n_sq):
            for i in ids:
                pw_b = pw[i].astype(BF16)
                both = jnp.dot(jnp.concatenate([pw_b, tinv[i].astype(BF16)], axis=0), pw_b,
                               preferred_element_type=F32)
                pw[i] = both[:2 * L]
                tinv[i] = tinv[i] + both[2 * L:]
        tinv = [tinv[i] + _bdot(tinv[i], pw[i]) for i in ids]
        wu = [_bdot(tinv[i], jnp.concatenate([at_s[i], akv[i]], axis=1)) for i in ids]
        bk_t = [jnp.concatenate([_stack_heads(cins[j]["bh"][:, lsl[p]]),
                                 _stack_heads(cins[j]["kh"][:, lsl[p]])], axis=0).T.astype(BF16)
                for j, p in combos]
        h = [h_sc[p] for p in pairs]
        for j in range(RW_UNROLL):
            sel = [j * len(pairs) + p for p in pairs]
            wrg = [_bdot(jnp.concatenate(
                [wu[i][:, :2 * L], rt_s[i], jnp.where(eye, cins[j]["g_end"][:, lsl[p]], 0.0)],
                axis=0), h[p]) for p, i in zip(pairs, sel)]
            uv = [jnp.concatenate([wrg[p][:2 * L] + wu[i][:, 2 * L:], v_s[i]],
                                  axis=0).astype(BF16) for p, i in zip(pairs, sel)]
            for p, i in zip(pairs, sel):
                y_st = wrg[p][2 * L:4 * L] + jnp.dot(a_r[i].astype(BF16), uv[p],
                                                     preferred_element_type=F32)
                y_sc[cins[j]["rows"], lsl[p]] = y_st[:L] + y_st[L:]
            h = [wrg[p][4 * L:] + jnp.dot(bk_t[i], uv[p], preferred_element_type=F32)
                 for p, i in zip(pairs, sel)]
        for p in pairs:
            h_sc[p] = h[p]

    n_grp = tb // grp_rows
    prep(0)
    for grp in range(n_grp):
        chunk_group(grp)
        if grp + 1 < n_grp:
            prep((grp + 1) * grp_rows)
        if grp > 0:
            post((grp - 1) * grp_rows)
    post((n_grp - 1) * grp_rows)


def _rwkv_mix(pr, v_first, prm, batch, seq):
    tb = RW_TB
    nt = seq // tb
    first = v_first is None
    c = D_RWKV
    row_spec = lambda w: pl.BlockSpec((tb, w), lambda b, t: (b * nt + t, 0))
    vec_spec = lambda w: pl.BlockSpec((1, w), lambda b, t: (0, 0))
    mat_spec = lambda: pl.BlockSpec((LANES, c), lambda b, t: (0, 0))
    vec = lambda x: x.reshape(1, -1)
    if first:
        args = [pr, vec(prm["w0"]), vec(prm["a0"]), vec(prm["k_k"]), vec(prm["k_a"]),
                vec(prm["r_k"]), vec(prm["gn_g"]), vec(prm["gn_b"]),
                prm["w_up"], prm["a_up"], prm["g_up1"], prm["g_up2"]]
        in_specs = [row_spec(RW_PAD)] + [vec_spec(c)] * 7 + [mat_spec()] * 4
        out_shape = [jax.ShapeDtypeStruct((batch * seq, c), BF16),
                     jax.ShapeDtypeStruct((batch * seq, c), F32)]
        out_specs = [row_spec(c), row_spec(c)]
    else:
        args = [pr, v_first, vec(prm["w0"]), vec(prm["a0"]), vec(prm["v0"]),
                vec(prm["k_k"]), vec(prm["k_a"]), vec(prm["r_k"]), vec(prm["gn_g"]), vec(prm["gn_b"]),
                prm["w_up"], prm["a_up"], prm["g_up1"], prm["g_up2"], prm["v_up"]]
        in_specs = [row_spec(RW_PAD), row_spec(c)] + [vec_spec(c)] * 8 + [mat_spec()] * 5
        out_shape = jax.ShapeDtypeStruct((batch * seq, c), BF16)
        out_specs = row_spec(c)
    scratch = [pltpu.VMEM((c // LANES, LANES, LANES), F32)]
    scratch += [pltpu.VMEM((tb, c), F32)] * 9
    return pl.pallas_call(
        functools.partial(_rwkv_body, tb=tb, first=first),
        grid=(batch, nt),
        in_specs=in_specs,
        out_specs=out_specs,
        out_shape=out_shape,
        scratch_shapes=scratch,
        compiler_params=_cparams(("parallel", "arbitrary")),
        name="rwkv7_mix_first" if first else "rwkv7_mix_rest",
    )(*args)


def _out_body(yd_ref, yr_ref, x_ref, w_ref, g_ref, b_ref, o_ref):
    mix = (jnp.dot(yd_ref[...], w_ref[:D_DIFF, :], preferred_element_type=F32)
           + jnp.dot(yr_ref[...], w_ref[D_DIFF:, :], preferred_element_type=F32))
    o_ref[...] = _layer_norm(ALPHA * x_ref[...] + mix, g_ref[...], b_ref[...])


def _out_proj(yd, yr, x, w, g, b, tm=512):
    t, d = x.shape
    return pl.pallas_call(
        _out_body,
        grid=(t // tm,),
        in_specs=[pl.BlockSpec((tm, D_DIFF), lambda i: (i, 0)),
                  pl.BlockSpec((tm, D_RWKV), lambda i: (i, 0)),
                  pl.BlockSpec((tm, d), lambda i: (i, 0)),
                  pl.BlockSpec((D_DIFF + D_RWKV, d), lambda i: (0, 0)),
                  pl.BlockSpec((1, d), lambda i: (0, 0)),
                  pl.BlockSpec((1, d), lambda i: (0, 0))],
        out_specs=pl.BlockSpec((tm, d), lambda i: (i, 0)),
        out_shape=jax.ShapeDtypeStruct((t, d), F32),
        compiler_params=_cparams(("parallel",)),
        name="out_proj_norm",
    )(yd, yr, x, w, g.reshape(1, d), b.reshape(1, d))


def _mlp_body(x_ref, wu_ref, wd_ref, g_ref, b_ref, o_ref, xb_sc, acc_sc):
    f = pl.program_id(1)

    @pl.when(f == 0)
    def _():
        xb_sc[...] = x_ref[...].astype(BF16)
        acc_sc[...] = jnp.zeros(acc_sc.shape, F32)

    h = jnp.maximum(jnp.dot(xb_sc[...], wu_ref[...], preferred_element_type=F32), 0.0)
    acc_sc[...] += jnp.dot((h * h).astype(BF16), wd_ref[...], preferred_element_type=F32)

    @pl.when(f == pl.num_programs(1) - 1)
    def _():
        o_ref[...] = _layer_norm(ALPHA * x_ref[...] + acc_sc[...], g_ref[...], b_ref[...])


def _mlp(x, wu, wd, g, b, tm=1024, tf=1024):
    t, d = x.shape
    ff = wu.shape[1]
    return pl.pallas_call(
        _mlp_body,
        grid=(t // tm, ff // tf),
        in_specs=[pl.BlockSpec((tm, d), lambda i, f: (i, 0)),
                  pl.BlockSpec((d, tf), lambda i, f: (0, f)),
                  pl.BlockSpec((tf, d), lambda i, f: (f, 0)),
                  pl.BlockSpec((1, d), lambda i, f: (0, 0)),
                  pl.BlockSpec((1, d), lambda i, f: (0, 0))],
        out_specs=pl.BlockSpec((tm, d), lambda i, f: (i, 0)),
        out_shape=jax.ShapeDtypeStruct((t, d), F32),
        scratch_shapes=[pltpu.VMEM((tm, d), BF16), pltpu.VMEM((tm, d), F32)],
        compiler_params=_cparams(("parallel", "arbitrary")),
        name="mlp_norm",
    )(x, wu, wd, g.reshape(1, d), b.reshape(1, d))


def _pad_rows(w, before, total):
    return jnp.pad(w, ((before, total - before - w.shape[0]), (0, 0)))


def _rwkv_params(l, rw_w0, rw_w_up, rw_a0, rw_a_up, rw_g_up, rw_v0, rw_v_up,
                 rw_k_k, rw_k_a, rw_r_k, rw_gn_g, rw_gn_b):
    prm = {
        "w0": rw_w0[l], "a0": rw_a0[l], "k_k": rw_k_k[l], "k_a": rw_k_a[l],
        "r_k": rw_r_k[l].reshape(-1), "gn_g": rw_gn_g[l], "gn_b": rw_gn_b[l],
        "w_up": _pad_rows(rw_w_up[l], 0, LANES).astype(BF16),
        "a_up": _pad_rows(rw_a_up[l], LORA_W, LANES).astype(BF16),
        "g_up1": rw_g_up[l][:LANES].astype(BF16),
        "g_up2": _pad_rows(rw_g_up[l][LANES:], 0, LANES).astype(BF16),
    }
    if l > 0:
        prm["v0"] = rw_v0[l - 1]
        prm["v_up"] = _pad_rows(rw_v_up[l - 1], LORA_G - LANES, LANES).astype(BF16)
    return prm


def _forward(x, ln_in_g, ln_in_b, w_in_first, w_in_rest, mu_first, mu_rest, rel_bias,
             lambda_q1, lambda_k1, lambda_q2, lambda_k2, subln_g,
             rw_w0, rw_w_up, rw_a0, rw_a_up, rw_g_up, rw_v0, rw_v_up,
             rw_k_k, rw_k_a, rw_r_k, rw_gn_g, rw_gn_b,
             w_out, ln_mix_g, ln_mix_b, w_up, w_down, ln_ffn_g, ln_ffn_b):
    batch, seq, d = x.shape
    xs = x.reshape(batch * seq, d)
    bias_tiles = _bias_tiles(rel_bias, ATT_T)
    v_first = None
    for l in range(DEPTH):
        w_in = w_in_first if l == 0 else w_in_rest[l - 1]
        mu = mu_first if l == 0 else mu_rest[l - 1]
        w_pad = jnp.pad(w_in, ((0, 0), (0, N_DIFF + RW_PAD - w_in.shape[1]))).astype(BF16)
        mu_pad = jnp.pad(mu, (0, RW_PAD - mu.shape[0]))
        if l == 0:
            xs, qkv, pr = _in_proj(xs, w_pad, mu_pad, (ln_in_g, ln_in_b), seq)
        else:
            qkv, pr = _in_proj(xs, w_pad, mu_pad, None, seq)

        lam_init = 0.8 - 0.6 * math.exp(-0.3 * l)
        lam_params = jnp.stack([lambda_q1[l], lambda_k1[l], lambda_q2[l], lambda_k2[l]])
        y_diff = _diff_attention(qkv, bias_tiles, rel_bias, lam_params, subln_g[l], lam_init, batch, seq)

        prm = _rwkv_params(l, rw_w0, rw_w_up, rw_a0, rw_a_up, rw_g_up, rw_v0, rw_v_up,
                           rw_k_k, rw_k_a, rw_r_k, rw_gn_g, rw_gn_b)
        if l == 0:
            y_rw, v_first = _rwkv_mix(pr, None, prm, batch, seq)
        else:
            y_rw = _rwkv_mix(pr, v_first, prm, batch, seq)

        xs = _out_proj(y_diff, y_rw, xs, w_out[l].astype(BF16), ln_mix_g[l], ln_mix_b[l])
        xs = _mlp(xs, w_up[l].astype(BF16), w_down[l].astype(BF16), ln_ffn_g[l], ln_ffn_b[l])
    return xs.reshape(batch, seq, d)


def kernel(x, ln_in_g, ln_in_b, w_in_first, w_in_rest, mu_first, mu_rest, rel_bias, lambda_q1, lambda_k1, lambda_q2, lambda_k2, subln_g, rw_w0, rw_w_up, rw_a0, rw_a_up, rw_g_up, rw_v0, rw_v_up, rw_k_k, rw_k_a, rw_r_k, rw_gn_g, rw_gn_b, w_out, ln_mix_g, ln_mix_b, w_up, w_down, ln_ffn_g, ln_ffn_b):
    return _forward(x, ln_in_g, ln_in_b, w_in_first, w_in_rest, mu_first, mu_rest, rel_bias,
                    lambda_q1, lambda_k1, lambda_q2, lambda_k2, subln_g,
                    rw_w0, rw_w_up, rw_a0, rw_a_up, rw_g_up, rw_v0, rw_v_up,
                    rw_k_k, rw_k_a, rw_r_k, rw_gn_g, rw_gn_b,
                    w_out, ln_mix_g, ln_mix_b, w_up, w_down, ln_ffn_g, ln_ffn_b)
```

```python
import functools
import math

import jax
import jax.numpy as jnp
from jax import lax
from jax.experimental import pallas as pl
from jax.experimental.pallas import tpu as pltpu

F32 = jnp.float32
BF16 = jnp.bfloat16

D_MODEL = 1024
DEPTH = 2
D_DIFF = 512
D_RWKV = 512
DIFF_HEADS = 4
DIFF_HEAD_DIM = 64
DIFF_V_DIM = 128
RWKV_HEAD = 64
LORA_W = 64
LORA_A = 64
LORA_V = 32
LORA_G = 160
D_FF = 4 * D_MODEL
N_BUCKETS = 32
MAX_DISTANCE = 128
LN_EPS = 1e-5
SUBLN_EPS = 1e-5
GN_EPS = 64e-5
ALPHA = (2 * DEPTH) ** 0.25
N_DIFF = 3 * D_DIFF
RW_W = 3 * D_RWKV

LANES = 128
MXU_W = 256
VMEM_LIMIT = 56 * 1024 * 1024

RW_PAD = 2048
ATT_T = 512
RW_TB = 512
RW_L = 64
RW_UNROLL = 4
NEG = -1e30


def _cparams(sem):
    return pltpu.CompilerParams(dimension_semantics=sem, vmem_limit_bytes=VMEM_LIMIT)


def _layer_norm(z, g, b):
    mu = jnp.mean(z, -1, keepdims=True)
    zc = z - mu
    var = jnp.mean(zc * zc, -1, keepdims=True)
    return zc * lax.rsqrt(var + LN_EPS) * g + b


def _proj_body(*refs, with_ln, tn, tiles_per_seq):
    if with_ln:
        x_ref, g_ref, b_ref, w_ref, mu_ref, xn_ref, qkv_ref, pr_ref, carry_sc = refs
        xn = _layer_norm(x_ref[...], g_ref[...], b_ref[...])
        xn_ref[...] = xn
    else:
        x_ref, w_ref, mu_ref, qkv_ref, pr_ref, carry_sc = refs
        xn = x_ref[...]
    tm = x_ref.shape[0]
    xb = xn.astype(BF16)
    seq_start = (pl.program_id(0) % tiles_per_seq) == 0
    row8 = lax.broadcasted_iota(jnp.int32, (8, 1), 0)
    for j in range(w_ref.shape[1] // tn):
        o = jnp.dot(xb, w_ref[:, j * tn:(j + 1) * tn], preferred_element_type=F32)
        if (j + 1) * tn <= N_DIFF:
            qkv_ref[:, j * tn:(j + 1) * tn] = o.astype(BF16)
        else:
            cols = slice(j * tn - N_DIFF, (j + 1) * tn - N_DIFF)
            carry = jnp.where(seq_start, 0.0, carry_sc[:, cols])
            rolled = pltpu.roll(o, 1, 0)
            prev = jnp.concatenate([jnp.where(row8 == 0, carry, rolled[:8]), rolled[8:]], axis=0)
            carry_sc[:, cols] = o[tm - 1:tm, :]
            pr_ref[:, cols] = o + (prev - o) * mu_ref[:, cols]


def _in_proj(x, w, mu, ln, seq, tm=512, tn=512):
    t, d = x.shape
    n = w.shape[1]
    with_ln = ln is not None
    row = lambda wd: pl.BlockSpec((tm, wd), lambda i: (i, 0))
    const = lambda r, wd: pl.BlockSpec((r, wd), lambda i: (0, 0))
    in_specs = ([row(d)] + ([const(1, d), const(1, d)] if with_ln else [])
                + [const(d, n), const(1, n - N_DIFF)])
    args = ([x] + ([ln[0].reshape(1, d), ln[1].reshape(1, d)] if with_ln else [])
            + [w, mu.reshape(1, n - N_DIFF)])
    out_shape = [jax.ShapeDtypeStruct((t, N_DIFF), BF16), jax.ShapeDtypeStruct((t, n - N_DIFF), F32)]
    out_specs = [row(N_DIFF), row(n - N_DIFF)]
    if with_ln:
        out_shape = [jax.ShapeDtypeStruct((t, d), F32)] + out_shape
        out_specs = [row(d)] + out_specs
    return pl.pallas_call(
        functools.partial(_proj_body, with_ln=with_ln, tn=tn, tiles_per_seq=seq // tm),
        grid=(t // tm,),
        in_specs=in_specs,
        out_specs=out_specs,
        out_shape=out_shape,
        scratch_shapes=[pltpu.VMEM((1, n - N_DIFF), F32)],
        compiler_params=_cparams(("arbitrary",)),
        name="ln_in_proj" if with_ln else "in_proj",
    )(*args)


def _bias_body(rb_ref, o_ref, *, t):
    h = pl.program_id(0)
    which = pl.program_id(1)
    i = lax.broadcasted_iota(jnp.int32, (t, t), 0)
    j = lax.broadcasted_iota(jnp.int32, (t, t), 1)
    n = jnp.maximum(which * t + i - j, 0)
    max_exact = N_BUCKETS // 2
    nf = jnp.maximum(n, 1).astype(F32)
    large = max_exact + (jnp.log(nf / max_exact) / math.log(MAX_DISTANCE / max_exact)
                         * (N_BUCKETS - max_exact)).astype(jnp.int32)
    large = jnp.minimum(large, N_BUCKETS - 1)
    bucket = jnp.where(n < max_exact, n, large)
    acc = jnp.zeros((t, t), F32)
    for b in range(N_BUCKETS):
        acc = jnp.where(bucket == b, rb_ref[b, h], acc)
    o_ref[0, 0] = acc


def _bias_tiles(rel_bias, t):
    return pl.pallas_call(
        functools.partial(_bias_body, t=t),
        grid=(DIFF_HEADS, 2),
        in_specs=[pl.BlockSpec(memory_space=pltpu.SMEM)],
        out_specs=pl.BlockSpec((1, 1, t, t), lambda h, w: (h, w, 0, 0)),
        out_shape=jax.ShapeDtypeStruct((DIFF_HEADS, 2, t, t), F32),
        compiler_params=_cparams(("parallel", "parallel")),
        name="t5_bias_tiles",
    )(rel_bias)


def _attn_body(rb_ref, lamp_ref, q_ref, k_ref, v_ref, bias_ref, g_ref, o_ref,
               m_sc, acc_sc, *, t, nq, lam_init):
    h = pl.program_id(1)
    lane = lax.broadcasted_iota(jnp.int32, (t, LANES), 1)
    c_far = rb_ref[N_BUCKETS - 1, h]
    lp = lamp_ref[...]
    lam = (jnp.exp(jnp.sum(lp[0:1] * lp[1:2], -1, keepdims=True))
           - jnp.exp(jnp.sum(lp[2:3] * lp[3:4], -1, keepdims=True)) + lam_init)

    def q_tile(iq):
        qrows = slice(iq * t, (iq + 1) * t)
        q = q_ref[qrows, :].astype(F32) * (DIFF_HEAD_DIM ** -0.5)
        qm = (jnp.where(lane < DIFF_HEAD_DIM, q, 0.0).astype(BF16),
              jnp.where(lane < DIFF_HEAD_DIM, 0.0, q).astype(BF16))
        m_sc[...] = jnp.full(m_sc.shape, NEG, F32)
        acc_sc[...] = jnp.zeros(acc_sc.shape, F32)

        def tile(row0, width, bias, diag_col0):
            rows = slice(row0, row0 + width)
            kt = k_ref[rows, :]
            vt = jnp.concatenate([v_ref[rows, :], jnp.ones((width, LANES), BF16)], axis=1)
            maps = range(2)
            s = [lax.dot_general(qm[m], kt, (((1,), (1,)), ((), ())), preferred_element_type=F32)
                 for m in maps]
            if bias is None:
                m_cur = [jnp.max(s[m], -1, keepdims=True) + c_far for m in maps]
            else:
                s = [s[m] + bias for m in maps]
                if diag_col0 is not None:
                    ri = lax.broadcasted_iota(jnp.int32, (t, width), 0)
                    ci = lax.broadcasted_iota(jnp.int32, (t, width), 1)
                    s = [jnp.where(ri >= ci - diag_col0, s[m], NEG) for m in maps]
                m_cur = [jnp.max(s[m], -1, keepdims=True) for m in maps]
            m_old = [m_sc[m] for m in maps]
            m_new = [jnp.maximum(m_old[m], m_cur[m]) for m in maps]
            a = [jnp.exp(m_old[m] - m_new[m]) for m in maps]
            sub = [m_new[m] - c_far if bias is None else m_new[m] for m in maps]
            p = [jnp.exp(s[m] - jnp.concatenate([sub[m]] * (width // LANES), axis=1)).astype(BF16)
                 for m in maps]
            pv = [jnp.dot(p[m], vt, preferred_element_type=F32) for m in maps]
            for m in maps:
                acc_sc[m] = jnp.concatenate([a[m], a[m]], axis=1) * acc_sc[m] + pv[m]
                m_sc[m] = m_new[m]

        n_far = max(iq - 1, 0)
        for j in range(n_far // 2):
            tile(j * 2 * t, 2 * t, None, None)
        if n_far % 2 == 1:
            tile((iq - 2) * t, 3 * t,
                 jnp.concatenate([jnp.full((t, t), c_far, F32), bias_ref[0, 1], bias_ref[0, 0]],
                                 axis=1), 2 * t)
        elif iq > 0:
            tile((iq - 1) * t, 2 * t,
                 jnp.concatenate([bias_ref[0, 1], bias_ref[0, 0]], axis=1), t)
        else:
            tile(0, t, bias_ref[0, 0], 0)

        a0 = acc_sc[0]
        a1 = acc_sc[1]
        out = a0[:, :LANES] / a0[:, LANES:] - lam * (a1[:, :LANES] / a1[:, LANES:])
        out = out * lax.rsqrt(jnp.mean(out * out, -1, keepdims=True) + SUBLN_EPS) * g_ref[...]
        o_ref[qrows, :] = (out * (1.0 - lam_init)).astype(o_ref.dtype)

    for iq in range(nq):
        pl.when(h >= 0)(functools.partial(q_tile, iq))


def _diff_attention(qkv, bias_tiles, rel_bias, lam_params, subln_g, lam_init, batch, seq):
    t = ATT_T
    hb = D_DIFF // LANES
    return pl.pallas_call(
        functools.partial(_attn_body, t=t, nq=seq // t, lam_init=lam_init),
        grid=(batch, DIFF_HEADS),
        in_specs=[pl.BlockSpec(memory_space=pltpu.SMEM),
                  pl.BlockSpec((4, DIFF_HEAD_DIM), lambda b, h: (0, 0)),
                  pl.BlockSpec((seq, LANES), lambda b, h: (b, h)),
                  pl.BlockSpec((seq, LANES), lambda b, h: (b, hb + h)),
                  pl.BlockSpec((seq, LANES), lambda b, h: (b, 2 * hb + h)),
                  pl.BlockSpec((1, 2, t, t), lambda b, h: (h, 0, 0, 0)),
                  pl.BlockSpec((1, DIFF_V_DIM), lambda b, h: (0, 0))],
        out_specs=pl.BlockSpec((seq, LANES), lambda b, h: (b, h)),
        out_shape=jax.ShapeDtypeStruct((batch * seq, D_DIFF), BF16),
        scratch_shapes=[pltpu.VMEM((2, t, LANES), F32),
                        pltpu.VMEM((2, t, DIFF_V_DIM + LANES), F32)],
        compiler_params=_cparams(("parallel", "parallel")),
        name="diff_attention",
    )(rel_bias, lam_params, qkv, qkv, qkv, bias_tiles, subln_g.reshape(1, DIFF_V_DIM))


def _split_dot(x, m):
    hi = x.astype(BF16)
    lo = (x - hi.astype(F32)).astype(BF16)
    return (jnp.dot(hi, m, preferred_element_type=F32)
            + jnp.dot(lo, m, preferred_element_type=F32))


def _stack_heads(x):
    lane = lax.broadcasted_iota(jnp.int32, x.shape, 1)
    return jnp.concatenate([jnp.where(lane < RWKV_HEAD, x, 0.0),
                            jnp.where(lane < RWKV_HEAD, 0.0, x)], axis=0)


def _bdot(a, b):
    return jnp.dot(a.astype(BF16), b.astype(BF16), preferred_element_type=F32)


def _rwkv_body(*refs, tb, first):
    if first:
        (pr_ref, w0_ref, a0_ref, kk_ref, ka_ref, rk_ref, gng_ref, gnb_ref,
         wup_ref, aup_ref, gup1_ref, gup2_ref,
         y_ref, vout_ref,
         h_sc, r_sc, k_sc, v_sc, as_sc, bs_sc, ci_sc, lw_sc, g_sc, y_sc) = refs
    else:
        (pr_ref, vf_ref, w0_ref, a0_ref, v0_ref, kk_ref, ka_ref, rk_ref, gng_ref, gnb_ref,
         wup_ref, aup_ref, gup1_ref, gup2_ref, vup_ref,
         y_ref,
         h_sc, r_sc, k_sc, v_sc, as_sc, bs_sc, ci_sc, lw_sc, g_sc, y_sc) = refs
    L = RW_L
    c = D_RWKV
    grp_rows = L * RW_UNROLL
    assert grp_rows == MXU_W and tb % grp_rows == 0

    @pl.when(pl.program_id(1) == 0)
    def _():
        h_sc[...] = jnp.zeros(h_sc.shape, F32)

    bi = lax.broadcasted_iota(jnp.int32, (MXU_W, MXU_W), 0)
    bj = lax.broadcasted_iota(jnp.int32, (MXU_W, MXU_W), 1)
    hs = RWKV_HEAD.bit_length() - 1
    head_ones = jnp.where((bi >> hs) == (bj >> hs), 1.0, 0.0).astype(BF16)
    ls_ = L.bit_length() - 1
    tri = jnp.where(((bi >> ls_) == (bj >> ls_)) & (bj <= bi), 1.0, 0.0).astype(BF16)

    def head_sum(x):
        return jnp.concatenate([_split_dot(x[:, i:i + MXU_W], head_ones)
                                for i in range(0, c, MXU_W)], axis=1)

    def prep(r0):
        rows = slice(r0, r0 + MXU_W)
        r = pr_ref[rows, 0:c]
        k = pr_ref[rows, c:2 * c]
        v = pr_ref[rows, 2 * c:3 * c]
        t1 = pr_ref[rows, RW_W:RW_W + LANES]
        t2 = pr_ref[rows, RW_W + LANES:RW_W + 2 * LANES]
        t3 = pr_ref[rows, RW_W + 2 * LANES:RW_W + 3 * LANES]

        lw = _bdot(jnp.tanh(t1), wup_ref[...])
        la = _bdot(t1, aup_ref[...])
        g_sc[rows, :] = (_bdot(jax.nn.sigmoid(t2), gup1_ref[...])
                         + _bdot(jax.nn.sigmoid(t3), gup2_ref[...]))

        z = -(w0_ref[...] + lw)
        softplus = jnp.maximum(z, 0.0) + jnp.log(1.0 + jnp.exp(-jnp.abs(z)))
        logw = -jnp.exp(-softplus - 0.5)
        a = jax.nn.sigmoid(a0_ref[...] + la)
        if first:
            vout_ref[rows, :] = v
        else:
            v = v + (vf_ref[rows, :] - v) * jax.nn.sigmoid(v0_ref[...] + _bdot(t3, vup_ref[...]))

        kk = k * kk_ref[...]
        kk = kk * jnp.minimum(lax.rsqrt(head_sum(kk * kk)), 1e12)

        hi = logw.astype(BF16)
        rem = logw - hi.astype(F32)
        mid = rem.astype(BF16)
        lo = (rem - mid.astype(F32)).astype(BF16)
        ci_sc[rows, :] = (jnp.dot(tri, hi, preferred_element_type=F32)
                          + jnp.dot(tri, mid, preferred_element_type=F32)
                          + jnp.dot(tri, lo, preferred_element_type=F32))
        r_sc[rows, :] = r
        k_sc[rows, :] = k * (1.0 + (a - 1.0) * ka_ref[...])
        v_sc[rows, :] = v
        as_sc[rows, :] = -kk
        bs_sc[rows, :] = kk * a
        lw_sc[rows, :] = logw

    def post(r0):
        rows = slice(r0, r0 + MXU_W)
        y = y_sc[rows, :]
        mean = head_sum(y) * (1.0 / RWKV_HEAD)
        yc = y - mean
        var = head_sum(yc * yc) * (1.0 / RWKV_HEAD)
        yn = yc * lax.rsqrt(var + GN_EPS) * gng_ref[...] + gnb_ref[...]
        bonus = head_sum(r_sc[rows, :] * k_sc[rows, :] * rk_ref[...]) * v_sc[rows, :]
        y_ref[rows, :] = ((yn + bonus) * g_sc[rows, :]).astype(y_ref.dtype)

    i2 = lax.broadcasted_iota(jnp.int32, (4 * L, 4 * L), 0)
    j2 = lax.broadcasted_iota(jnp.int32, (4 * L, 4 * L), 1)
    incl = jnp.where(i2 >= 2 * L, 1, 0)
    amask = ((((i2 >> ls_) & 1) == ((j2 >> ls_) & 1))
             & ((j2 & (L - 1)) < (i2 & (L - 1)) + incl))
    e_i = lax.broadcasted_iota(jnp.int32, (2 * L, 2 * L), 0)
    e_j = lax.broadcasted_iota(jnp.int32, (2 * L, 2 * L), 1)
    eye = e_i == e_j
    eye_f = jnp.where(eye, 1.0, 0.0)

    pairs = range(c // LANES)
    lsl = [slice(p * LANES, (p + 1) * LANES) for p in pairs]
    n_sq = int(math.log2(L)) - 2

    def chunk_group(grp):
        cins = []
        for j in range(RW_UNROLL):
            ch = grp * RW_UNROLL + j
            rows = slice(ch * L, (ch + 1) * L)
            ci_c = ci_sc[rows, :]
            last = ci_sc[ch * L + L - 1:ch * L + L, :]
            e_neg = jnp.exp(-ci_c)
            e_end = jnp.exp(last - ci_c)
            bs_c = bs_sc[rows, :]
            k_c = k_sc[rows, :]
            cins.append(dict(
                rows=rows, g_end=jnp.exp(last),
                at=as_sc[rows, :] * jnp.exp(ci_c - lw_sc[rows, :]),
                bt=bs_c * e_neg, kt=k_c * e_neg, rt=r_sc[rows, :] * jnp.exp(ci_c),
                bh=bs_c * e_end, kh=k_c * e_end, v=v_sc[rows, :]))
        combos = [(j, p) for j in range(RW_UNROLL) for p in pairs]
        at_s = [_stack_heads(cins[j]["at"][:, lsl[p]]) for j, p in combos]
        rt_s = [_stack_heads(cins[j]["rt"][:, lsl[p]]) for j, p in combos]
        v_s = [_stack_heads(cins[j]["v"][:, lsl[p]]) for j, p in combos]
        a_all = []
        for i, (j, p) in enumerate(combos):
            lhs = jnp.concatenate([at_s[i], rt_s[i]], axis=0).astype(BF16)
            rhs = jnp.concatenate([_stack_heads(cins[j]["bt"][:, lsl[p]]),
                                   _stack_heads(cins[j]["kt"][:, lsl[p]])], axis=0).astype(BF16)
            a = lax.dot_general(lhs, rhs, (((1,), (1,)), ((), ())), preferred_element_type=F32)
            a_all.append(jnp.where(amask, a, 0.0))
        ids = range(len(combos))
        a_ab = [a[:2 * L, :2 * L] for a in a_all]
        a_r = [a[2 * L:, :] for a in a_all]
        pw = [_bdot(a, a) for a in a_ab]
        tinv = [eye_f + a for a in a_ab]
        akv = [_bdot(a_all[i][:2 * L, 2 * L:], v_s[i]) for i in ids]
        for _ in range(n_sq):
            for i in ids:
                pw_b = pw[i].astype(BF16)
                both = jnp.dot(jnp.concatenate([pw_b, tinv[i].astype(BF16)], axis=0), pw_b,
                               preferred_element_type=F32)
                pw[i] = both[:2 * L]
                tinv[i] = tinv[i] + both[2 * L:]
        tinv = [tinv[i] + _bdot(tinv[i], pw[i]) for i in ids]
        wu = [_bdot(tinv[i], jnp.concatenate([at_s[i], akv[i]], axis=1)) for i in ids]
        bk_t = [jnp.concatenate([_stack_heads(cins[j]["bh"][:, lsl[p]]),
                                 _stack_heads(cins[j]["kh"][:, lsl[p]])], axis=0).T.astype(BF16)
                for j, p in combos]
        h = [h_sc[p] for p in pairs]
        for j in range(RW_UNROLL):
            sel = [j * len(pairs) + p for p in pairs]
            wrg = [_bdot(jnp.concatenate(
                [wu[i][:, :2 * L], rt_s[i], jnp.where(eye, cins[j]["g_end"][:, lsl[p]], 0.0)],
                axis=0), h[p]) for p, i in zip(pairs, sel)]
            uv = [jnp.concatenate([wrg[p][:2 * L] + wu[i][:, 2 * L:], v_s[i]],
                                  axis=0).astype(BF16) for p, i in zip(pairs, sel)]
            for p, i in zip(pairs, sel):
                y_st = wrg[p][2 * L:4 * L] + jnp.dot(a_r[i].astype(BF16), uv[p],
                                                     preferred_element_type=F32)
                y_sc[cins[j]["rows"], lsl[p]] = y_st[:L] + y_st[L:]
            h = [wrg[p][4 * L:] + jnp.dot(bk_t[i], uv[p], preferred_element_type=F32)
                 for p, i in zip(pairs, sel)]
        for p in pairs:
            h_sc[p] = h[p]

    n_grp = tb // grp_rows
    prep(0)
    for grp in range(n_grp):
        chunk_group(grp)
        if grp + 1 < n_grp:
            prep((grp + 1) * grp_rows)
        if grp > 0:
            post((grp - 1) * grp_rows)
    post((n_grp - 1) * grp_rows)


def _rwkv_mix(pr, v_first, prm, batch, seq):
    tb = RW_TB
    nt = seq // tb
    first = v_first is None
    c = D_RWKV
    row_spec = lambda w: pl.BlockSpec((tb, w), lambda b, t: (b * nt + t, 0))
    vec_spec = lambda w: pl.BlockSpec((1, w), lambda b, t: (0, 0))
    mat_spec = lambda: pl.BlockSpec((LANES, c), lambda b, t: (0, 0))
    vec = lambda x: x.reshape(1, -1)
    if first:
        args = [pr, vec(prm["w0"]), vec(prm["a0"]), vec(prm["k_k"]), vec(prm["k_a"]),
                vec(prm["r_k"]), vec(prm["gn_g"]), vec(prm["gn_b"]),
                prm["w_up"], prm["a_up"], prm["g_up1"], prm["g_up2"]]
        in_specs = [row_spec(RW_PAD)] + [vec_spec(c)] * 7 + [mat_spec()] * 4
        out_shape = [jax.ShapeDtypeStruct((batch * seq, c), BF16),
                     jax.ShapeDtypeStruct((batch * seq, c), F32)]
        out_specs = [row_spec(c), row_spec(c)]
    else:
        args = [pr, v_first, vec(prm["w0"]), vec(prm["a0"]), vec(prm["v0"]),
                vec(prm["k_k"]), vec(prm["k_a"]), vec(prm["r_k"]), vec(prm["gn_g"]), vec(prm["gn_b"]),
                prm["w_up"], prm["a_up"], prm["g_up1"], prm["g_up2"], prm["v_up"]]
        in_specs = [row_spec(RW_PAD), row_spec(c)] + [vec_spec(c)] * 8 + [mat_spec()] * 5
        out_shape = jax.ShapeDtypeStruct((batch * seq, c), BF16)
        out_specs = row_spec(c)
    scratch = [pltpu.VMEM((c // LANES, LANES, LANES), F32)]
    scratch += [pltpu.VMEM((tb, c), F32)] * 9
    return pl.pallas_call(
        functools.partial(_rwkv_body, tb=tb, first=first),
        grid=(batch, nt),
        in_specs=in_specs,
        out_specs=out_specs,
        out_shape=out_shape,
        scratch_shapes=scratch,
        compiler_params=_cparams(("parallel", "arbitrary")),
        name="rwkv7_mix_first" if first else "rwkv7_mix_rest",
    )(*args)


def _out_body(yd_ref, yr_ref, x_ref, w_ref, g_ref, b_ref, o_ref):
    mix = (jnp.dot(yd_ref[...], w_ref[:D_DIFF, :], preferred_element_type=F32)
           + jnp.dot(yr_ref[...], w_ref[D_DIFF:, :], preferred_element_type=F32))
    o_ref[...] = _layer_norm(ALPHA * x_ref[...] + mix, g_ref[...], b_ref[...])


def _out_proj(yd, yr, x, w, g, b, tm=512):
    t, d = x.shape
    return pl.pallas_call(
        _out_body,
        grid=(t // tm,),
        in_specs=[pl.BlockSpec((tm, D_DIFF), lambda i: (i, 0)),
                  pl.BlockSpec((tm, D_RWKV), lambda i: (i, 0)),
                  pl.BlockSpec((tm, d), lambda i: (i, 0)),
                  pl.BlockSpec((D_DIFF + D_RWKV, d), lambda i: (0, 0)),
                  pl.BlockSpec((1, d), lambda i: (0, 0)),
                  pl.BlockSpec((1, d), lambda i: (0, 0))],
        out_specs=pl.BlockSpec((tm, d), lambda i: (i, 0)),
        out_shape=jax.ShapeDtypeStruct((t, d), F32),
        compiler_params=_cparams(("parallel",)),
        name="out_proj_norm",
    )(yd, yr, x, w, g.reshape(1, d), b.reshape(1, d))


def _mlp_body(x_ref, wu_ref, wd_ref, g_ref, b_ref, o_ref, xb_sc, acc_sc):
    f = pl.program_id(1)

    @pl.when(f == 0)
    def _():
        xb_sc[...] = x_ref[...].astype(BF16)
        acc_sc[...] = jnp.zeros(acc_sc.shape, F32)

    h = jnp.maximum(jnp.dot(xb_sc[...], wu_ref[...], preferred_element_type=F32), 0.0)
    acc_sc[...] += jnp.dot((h * h).astype(BF16), wd_ref[...], preferred_element_type=F32)

    @pl.when(f == pl.num_programs(1) - 1)
    def _():
        o_ref[...] = _layer_norm(ALPHA * x_ref[...] + acc_sc[...], g_ref[...], b_ref[...])


def _mlp(x, wu, wd, g, b, tm=1024, tf=1024):
    t, d = x.shape
    ff = wu.shape[1]
    return pl.pallas_call(
        _mlp_body,
        grid=(t // tm, ff // tf),
        in_specs=[pl.BlockSpec((tm, d), lambda i, f: (i, 0)),
                  pl.BlockSpec((d, tf), lambda i, f: (0, f)),
                  pl.BlockSpec((tf, d), lambda i, f: (f, 0)),
                  pl.BlockSpec((1, d), lambda i, f: (0, 0)),
                  pl.BlockSpec((1, d), lambda i, f: (0, 0))],
        out_specs=pl.BlockSpec((tm, d), lambda i, f: (i, 0)),
        out_shape=jax.ShapeDtypeStruct((t, d), F32),
        scratch_shapes=[pltpu.VMEM((tm, d), BF16), pltpu.VMEM((tm, d), F32)],
        compiler_params=_cparams(("parallel", "arbitrary")),
        name="mlp_norm",
    )(x, wu, wd, g.reshape(1, d), b.reshape(1, d))


def _pad_rows(w, before, total):
    return jnp.pad(w, ((before, total - before - w.shape[0]), (0, 0)))


def _rwkv_params(l, rw_w0, rw_w_up, rw_a0, rw_a_up, rw_g_up, rw_v0, rw_v_up,
                 rw_k_k, rw_k_a, rw_r_k, rw_gn_g, rw_gn_b):
    prm = {
        "w0": rw_w0[l], "a0": rw_a0[l], "k_k": rw_k_k[l], "k_a": rw_k_a[l],
        "r_k": rw_r_k[l].reshape(-1), "gn_g": rw_gn_g[l], "gn_b": rw_gn_b[l],
        "w_up": _pad_rows(rw_w_up[l], 0, LANES).astype(BF16),
        "a_up": _pad_rows(rw_a_up[l], LORA_W, LANES).astype(BF16),
        "g_up1": rw_g_up[l][:LANES].astype(BF16),
        "g_up2": _pad_rows(rw_g_up[l][LANES:], 0, LANES).astype(BF16),
    }
    if l > 0:
        prm["v0"] = rw_v0[l - 1]
        prm["v_up"] = _pad_rows(rw_v_up[l - 1], LORA_G - LANES, LANES).astype(BF16)
    return prm


def _forward(x, ln_in_g, ln_in_b, w_in_first, w_in_rest, mu_first, mu_rest, rel_bias,
             lambda_q1, lambda_k1, lambda_q2, lambda_k2, subln_g,
             rw_w0, rw_w_up, rw_a0, rw_a_up, rw_g_up, rw_v0, rw_v_up,
             rw_k_k, rw_k_a, rw_r_k, rw_gn_g, rw_gn_b,
             w_out, ln_mix_g, ln_mix_b, w_up, w_down, ln_ffn_g, ln_ffn_b):
    batch, seq, d = x.shape
    xs = x.reshape(batch * seq, d)
    bias_tiles = _bias_tiles(rel_bias, ATT_T)
    v_first = None
    for l in range(DEPTH):
        w_in = w_in_first if l == 0 else w_in_rest[l - 1]
        mu = mu_first if l == 0 else mu_rest[l - 1]
        w_pad = jnp.pad(w_in, ((0, 0), (0, N_DIFF + RW_PAD - w_in.shape[1]))).astype(BF16)
        mu_pad = jnp.pad(mu, (0, RW_PAD - mu.shape[0]))
        if l == 0:
            xs, qkv, pr = _in_proj(xs, w_pad, mu_pad, (ln_in_g, ln_in_b), seq)
        else:
            qkv, pr = _in_proj(xs, w_pad, mu_pad, None, seq)

        lam_init = 0.8 - 0.6 * math.exp(-0.3 * l)
        lam_params = jnp.stack([lambda_q1[l], lambda_k1[l], lambda_q2[l], lambda_k2[l]])
        y_diff = _diff_attention(qkv, bias_tiles, rel_bias, lam_params, subln_g[l], lam_init, batch, seq)

        prm = _rwkv_params(l, rw_w0, rw_w_up, rw_a0, rw_a_up, rw_g_up, rw_v0, rw_v_up,
                           rw_k_k, rw_k_a, rw_r_k, rw_gn_g, rw_gn_b)
        if l == 0:
            y_rw, v_first = _rwkv_mix(pr, None, prm, batch, seq)
        else:
            y_rw = _rwkv_mix(pr, v_first, prm, batch, seq)

        xs = _out_proj(y_diff, y_rw, xs, w_out[l].astype(BF16), ln_mix_g[l], ln_mix_b[l])
        xs = _mlp(xs, w_up[l].astype(BF16), w_down[l].astype(BF16), ln_ffn_g[l], ln_ffn_b[l])
    return xs.reshape(batch, seq, d)


def kernel(x, ln_in_g, ln_in_b, w_in_first, w_in_rest, mu_first, mu_rest, rel_bias, lambda_q1, lambda_k1, lambda_q2, lambda_k2, subln_g, rw_w0, rw_w_up, rw_a0, rw_a_up, rw_g_up, rw_v0, rw_v_up, rw_k_k, rw_k_a, rw_r_k, rw_gn_g, rw_gn_b, w_out, ln_mix_g, ln_mix_b, w_up, w_down, ln_ffn_g, ln_ffn_b):
    return _forward(x, ln_in_g, ln_in_b, w_in_first, w_in_rest, mu_first, mu_rest, rel_bias,
                    lambda_q1, lambda_k1, lambda_q2, lambda_k2, subln_g,
                    rw_w0, rw_w_up, rw_a0, rw_a_up, rw_g_up, rw_v0, rw_v_up,
                    rw_k_k, rw_k_a, rw_r_k, rw_gn_g, rw_gn_b,
                    w_out, ln_mix_g, ln_mix_b, w_up, w_down, ln_ffn_g, ln_ffn_b)
```

```python
import functools
import math

import jax
import jax.numpy as jnp
from jax import lax
from jax.experimental import pallas as pl
from jax.experimental.pallas import tpu as pltpu

F32 = jnp.float32
BF16 = jnp.bfloat16

D_MODEL = 1024
DEPTH = 2
D_DIFF = 512
D_RWKV = 512
DIFF_HEADS = 4
DIFF_HEAD_DIM = 64
DIFF_V_DIM = 128
RWKV_HEAD = 64
LORA_W = 64
LORA_A = 64
LORA_V = 32
LORA_G = 160
D_FF = 4 * D_MODEL
N_BUCKETS = 32
MAX_DISTANCE = 128
LN_EPS = 1e-5
SUBLN_EPS = 1e-5
GN_EPS = 64e-5
ALPHA = (2 * DEPTH) ** 0.25
N_DIFF = 3 * D_DIFF
RW_W = 3 * D_RWKV

LANES = 128
MXU_W = 256
VMEM_LIMIT = 56 * 1024 * 1024

RW_PAD = 2048
ATT_T = 512
RW_TB = 512
RW_L = 64
RW_UNROLL = 4
NEG = -1e30


def _cparams(sem):
    return pltpu.CompilerParams(dimension_semantics=sem, vmem_limit_bytes=VMEM_LIMIT)


def _layer_norm(z, g, b):
    mu = jnp.mean(z, -1, keepdims=True)
    zc = z - mu
    var = jnp.mean(zc * zc, -1, keepdims=True)
    return zc * lax.rsqrt(var + LN_EPS) * g + b


def _proj_body(*refs, with_ln, tn, tiles_per_seq):
    if with_ln:
        x_ref, g_ref, b_ref, w_ref, mu_ref, xn_ref, qkv_ref, pr_ref, carry_sc = refs
        xn = _layer_norm(x_ref[...], g_ref[...], b_ref[...])
        xn_ref[...] = xn
    else:
        x_ref, w_ref, mu_ref, qkv_ref, pr_ref, carry_sc = refs
        xn = x_ref[...]
    tm = x_ref.shape[0]
    xb = xn.astype(BF16)
    seq_start = (pl.program_id(0) % tiles_per_seq) == 0
    row8 = lax.broadcasted_iota(jnp.int32, (8, 1), 0)
    for j in range(w_ref.shape[1] // tn):
        o = jnp.dot(xb, w_ref[:, j * tn:(j + 1) * tn], preferred_element_type=F32)
        if (j + 1) * tn <= N_DIFF:
            qkv_ref[:, j * tn:(j + 1) * tn] = o.astype(BF16)
        else:
            cols = slice(j * tn - N_DIFF, (j + 1) * tn - N_DIFF)
            carry = jnp.where(seq_start, 0.0, carry_sc[:, cols])
            rolled = pltpu.roll(o, 1, 0)
            prev = jnp.concatenate([jnp.where(row8 == 0, carry, rolled[:8]), rolled[8:]], axis=0)
            carry_sc[:, cols] = o[tm - 1:tm, :]
            pr_ref[:, cols] = o + (prev - o) * mu_ref[:, cols]


def _in_proj(x, w, mu, ln, seq, tm=512, tn=512):
    t, d = x.shape
    n = w.shape[1]
    with_ln = ln is not None
    row = lambda wd: pl.BlockSpec((tm, wd), lambda i: (i, 0))
    const = lambda r, wd: pl.BlockSpec((r, wd), lambda i: (0, 0))
    in_specs = ([row(d)] + ([const(1, d), const(1, d)] if with_ln else [])
                + [const(d, n), const(1, n - N_DIFF)])
    args = ([x] + ([ln[0].reshape(1, d), ln[1].reshape(1, d)] if with_ln else [])
            + [w, mu.reshape(1, n - N_DIFF)])
    out_shape = [jax.ShapeDtypeStruct((t, N_DIFF), BF16), jax.ShapeDtypeStruct((t, n - N_DIFF), F32)]
    out_specs = [row(N_DIFF), row(n - N_DIFF)]
    if with_ln:
        out_shape = [jax.ShapeDtypeStruct((t, d), F32)] + out_shape
        out_specs = [row(d)] + out_specs
    return pl.pallas_call(
        functools.partial(_proj_body, with_ln=with_ln, tn=tn, tiles_per_seq=seq // tm),
        grid=(t // tm,),
        in_specs=in_specs,
        out_specs=out_specs,
        out_shape=out_shape,
        scratch_shapes=[pltpu.VMEM((1, n - N_DIFF), F32)],
        compiler_params=_cparams(("arbitrary",)),
        name="ln_in_proj" if with_ln else "in_proj",
    )(*args)


def _bias_body(rb_ref, o_ref, *, t):
    h = pl.program_id(0)
    which = pl.program_id(1)
    i = lax.broadcasted_iota(jnp.int32, (t, t), 0)
    j = lax.broadcasted_iota(jnp.int32, (t, t), 1)
    n = jnp.maximum(which * t + i - j, 0)
    max_exact = N_BUCKETS // 2
    nf = jnp.maximum(n, 1).astype(F32)
    large = max_exact + (jnp.log(nf / max_exact) / math.log(MAX_DISTANCE / max_exact)
                         * (N_BUCKETS - max_exact)).astype(jnp.int32)
    large = jnp.minimum(large, N_BUCKETS - 1)
    bucket = jnp.where(n < max_exact, n, large)
    acc = jnp.zeros((t, t), F32)
    for b in range(N_BUCKETS):
        acc = jnp.where(bucket == b, rb_ref[b, h], acc)
    o_ref[0, 0] = acc


def _bias_tiles(rel_bias, t):
    return pl.pallas_call(
        functools.partial(_bias_body, t=t),
        grid=(DIFF_HEADS, 2),
        in_specs=[pl.BlockSpec(memory_space=pltpu.SMEM)],
        out_specs=pl.BlockSpec((1, 1, t, t), lambda h, w: (h, w, 0, 0)),
        out_shape=jax.ShapeDtypeStruct((DIFF_HEADS, 2, t, t), F32),
        compiler_params=_cparams(("parallel", "parallel")),
        name="t5_bias_tiles",
    )(rel_bias)


def _attn_body(rb_ref, lamp_ref, q_ref, k_ref, v_ref, bias_ref, g_ref, o_ref,
               m_sc, acc_sc, *, t, nq, lam_init):
    h = pl.program_id(1)
    lane = lax.broadcasted_iota(jnp.int32, (t, LANES), 1)
    c_far = rb_ref[N_BUCKETS - 1, h]
    lp = lamp_ref[...]
    lam = (jnp.exp(jnp.sum(lp[0:1] * lp[1:2], -1, keepdims=True))
           - jnp.exp(jnp.sum(lp[2:3] * lp[3:4], -1, keepdims=True)) + lam_init)

    def q_tile(iq):
        qrows = slice(iq * t, (iq + 1) * t)
        q = q_ref[qrows, :].astype(F32) * (DIFF_HEAD_DIM ** -0.5)
        qm = (jnp.where(lane < DIFF_HEAD_DIM, q, 0.0).astype(BF16),
              jnp.where(lane < DIFF_HEAD_DIM, 0.0, q).astype(BF16))
        m_sc[...] = jnp.full(m_sc.shape, NEG, F32)
        acc_sc[...] = jnp.zeros(acc_sc.shape, F32)

        def tile(row0, width, bias, diag_col0):
            rows = slice(row0, row0 + width)
            kt = k_ref[rows, :]
            vt = jnp.concatenate([v_ref[rows, :], jnp.ones((width, LANES), BF16)], axis=1)
            maps = range(2)
            s = [lax.dot_general(qm[m], kt, (((1,), (1,)), ((), ())), preferred_element_type=F32)
                 for m in maps]
            if bias is None:
                m_cur = [jnp.max(s[m], -1, keepdims=True) + c_far for m in maps]
            else:
                s = [s[m] + bias for m in maps]
                if diag_col0 is not None:
                    ri = lax.broadcasted_iota(jnp.int32, (t, width), 0)
                    ci = lax.broadcasted_iota(jnp.int32, (t, width), 1)
                    s = [jnp.where(ri >= ci - diag_col0, s[m], NEG) for m in maps]
                m_cur = [jnp.max(s[m], -1, keepdims=True) for m in maps]
            m_old = [m_sc[m] for m in maps]
            m_new = [jnp.maximum(m_old[m], m_cur[m]) for m in maps]
            a = [jnp.exp(m_old[m] - m_new[m]) for m in maps]
            sub = [m_new[m] - c_far if bias is None else m_new[m] for m in maps]
            p = [jnp.exp(s[m] - jnp.concatenate([sub[m]] * (width // LANES), axis=1)).astype(BF16)
                 for m in maps]
            pv = [jnp.dot(p[m], vt, preferred_element_type=F32) for m in maps]
            for m in maps:
                acc_sc[m] = jnp.concatenate([a[m], a[m]], axis=1) * acc_sc[m] + pv[m]
                m_sc[m] = m_new[m]

        n_far = max(iq - 1, 0)
        for j in range(n_far // 2):
            tile(j * 2 * t, 2 * t, None, None)
        if n_far % 2 == 1:
            tile((iq - 2) * t, 3 * t,
                 jnp.concatenate([jnp.full((t, t), c_far, F32), bias_ref[0, 1], bias_ref[0, 0]],
                                 axis=1), 2 * t)
        elif iq > 0:
            tile((iq - 1) * t, 2 * t,
                 jnp.concatenate([bias_ref[0, 1], bias_ref[0, 0]], axis=1), t)
        else:
            tile(0, t, bias_ref[0, 0], 0)

        a0 = acc_sc[0]
        a1 = acc_sc[1]
        out = a0[:, :LANES] / a0[:, LANES:] - lam * (a1[:, :LANES] / a1[:, LANES:])
        out = out * lax.rsqrt(jnp.mean(out * out, -1, keepdims=True) + SUBLN_EPS) * g_ref[...]
        o_ref[qrows, :] = (out * (1.0 - lam_init)).astype(o_ref.dtype)

    for iq in range(nq):
        pl.when(h >= 0)(functools.partial(q_tile, iq))


def _diff_attention(qkv, bias_tiles, rel_bias, lam_params, subln_g, lam_init, batch, seq):
    t = ATT_T
    hb = D_DIFF // LANES
    return pl.pallas_call(
        functools.partial(_attn_body, t=t, nq=seq // t, lam_init=lam_init),
        grid=(batch, DIFF_HEADS),
        in_specs=[pl.BlockSpec(memory_space=pltpu.SMEM),
                  pl.BlockSpec((4, DIFF_HEAD_DIM), lambda b, h: (0, 0)),
                  pl.BlockSpec((seq, LANES), lambda b, h: (b, h)),
                  pl.BlockSpec((seq, LANES), lambda b, h: (b, hb + h)),
                  pl.BlockSpec((seq, LANES), lambda b, h: (b, 2 * hb + h)),
                  pl.BlockSpec((1, 2, t, t), lambda b, h: (h, 0, 0, 0)),
                  pl.BlockSpec((1, DIFF_V_DIM), lambda b, h: (0, 0))],
        out_specs=pl.BlockSpec((seq, LANES), lambda b, h: (b, h)),
        out_shape=jax.ShapeDtypeStruct((batch * seq, D_DIFF), BF16),
        scratch_shapes=[pltpu.VMEM((2, t, LANES), F32),
                        pltpu.VMEM((2, t, DIFF_V_DIM + LANES), F32)],
        compiler_params=_cparams(("parallel", "parallel")),
        name="diff_attention",
    )(rel_bias, lam_params, qkv, qkv, qkv, bias_tiles, subln_g.reshape(1, DIFF_V_DIM))


def _split_dot(x, m):
    hi = x.astype(BF16)
    lo = (x - hi.astype(F32)).astype(BF16)
    return (jnp.dot(hi, m, preferred_element_type=F32)
            + jnp.dot(lo, m, preferred_element_type=F32))


def _stack_heads(x):
    lane = lax.broadcasted_iota(jnp.int32, x.shape, 1)
    return jnp.concatenate([jnp.where(lane < RWKV_HEAD, x, 0.0),
                            jnp.where(lane < RWKV_HEAD, 0.0, x)], axis=0)


def _bdot(a, b):
    return jnp.dot(a.astype(BF16), b.astype(BF16), preferred_element_type=F32)


def _rwkv_body(*refs, tb, first):
    if first:
        (pr_ref, w0_ref, a0_ref, kk_ref, ka_ref, rk_ref, gng_ref, gnb_ref,
         wup_ref, aup_ref, gup1_ref, gup2_ref,
         y_ref, vout_ref,
         h_sc, r_sc, k_sc, v_sc, as_sc, bs_sc, ci_sc, lw_sc, g_sc, y_sc) = refs
    else:
        (pr_ref, vf_ref, w0_ref, a0_ref, v0_ref, kk_ref, ka_ref, rk_ref, gng_ref, gnb_ref,
         wup_ref, aup_ref, gup1_ref, gup2_ref, vup_ref,
         y_ref,
         h_sc, r_sc, k_sc, v_sc, as_sc, bs_sc, ci_sc, lw_sc, g_sc, y_sc) = refs
    L = RW_L
    c = D_RWKV
    grp_rows = L * RW_UNROLL
    assert grp_rows == MXU_W and tb % grp_rows == 0

    @pl.when(pl.program_id(1) == 0)
    def _():
        h_sc[...] = jnp.zeros(h_sc.shape, F32)

    bi = lax.broadcasted_iota(jnp.int32, (MXU_W, MXU_W), 0)
    bj = lax.broadcasted_iota(jnp.int32, (MXU_W, MXU_W), 1)
    hs = RWKV_HEAD.bit_length() - 1
    head_ones = jnp.where((bi >> hs) == (bj >> hs), 1.0, 0.0).astype(BF16)
    ls_ = L.bit_length() - 1
    tri = jnp.where(((bi >> ls_) == (bj >> ls_)) & (bj <= bi), 1.0, 0.0).astype(BF16)

    def head_sum(x):
        return jnp.concatenate([_split_dot(x[:, i:i + MXU_W], head_ones)
                                for i in range(0, c, MXU_W)], axis=1)

    def prep(r0):
        rows = slice(r0, r0 + MXU_W)
        r = pr_ref[rows, 0:c]
        k = pr_ref[rows, c:2 * c]
        v = pr_ref[rows, 2 * c:3 * c]
        t1 = pr_ref[rows, RW_W:RW_W + LANES]
        t2 = pr_ref[rows, RW_W + LANES:RW_W + 2 * LANES]
        t3 = pr_ref[rows, RW_W + 2 * LANES:RW_W + 3 * LANES]

        lw = _bdot(jnp.tanh(t1), wup_ref[...])
        la = _bdot(t1, aup_ref[...])
        g_sc[rows, :] = (_bdot(jax.nn.sigmoid(t2), gup1_ref[...])
                         + _bdot(jax.nn.sigmoid(t3), gup2_ref[...]))

        z = -(w0_ref[...] + lw)
        softplus = jnp.maximum(z, 0.0) + jnp.log(1.0 + jnp.exp(-jnp.abs(z)))
        logw = -jnp.exp(-softplus - 0.5)
        a = jax.nn.sigmoid(a0_ref[...] + la)
        if first:
            vout_ref[rows, :] = v
        else:
            v = v + (vf_ref[rows, :] - v) * jax.nn.sigmoid(v0_ref[...] + _bdot(t3, vup_ref[...]))

        kk = k * kk_ref[...]
        kk = kk * jnp.minimum(lax.rsqrt(head_sum(kk * kk)), 1e12)

        hi = logw.astype(BF16)
        rem = logw - hi.astype(F32)
        mid = rem.astype(BF16)
        lo = (rem - mid.astype(F32)).astype(BF16)
        ci_sc[rows, :] = (jnp.dot(tri, hi, preferred_element_type=F32)
                          + jnp.dot(tri, mid, preferred_element_type=F32)
                          + jnp.dot(tri, lo, preferred_element_type=F32))
        r_sc[rows, :] = r
        k_sc[rows, :] = k * (1.0 + (a - 1.0) * ka_ref[...])
        v_sc[rows, :] = v
        as_sc[rows, :] = -kk
        bs_sc[rows, :] = kk * a
        lw_sc[rows, :] = logw

    def post(r0):
        rows = slice(r0, r0 + MXU_W)
        y = y_sc[rows, :]
        mean = head_sum(y) * (1.0 / RWKV_HEAD)
        yc = y - mean
        var = head_sum(yc * yc) * (1.0 / RWKV_HEAD)
        yn = yc * lax.rsqrt(var + GN_EPS) * gng_ref[...] + gnb_ref[...]
        bonus = head_sum(r_sc[rows, :] * k_sc[rows, :] * rk_ref[...]) * v_sc[rows, :]
        y_ref[rows, :] = ((yn + bonus) * g_sc[rows, :]).astype(y_ref.dtype)

    i2 = lax.broadcasted_iota(jnp.int32, (4 * L, 4 * L), 0)
    j2 = lax.broadcasted_iota(jnp.int32, (4 * L, 4 * L), 1)
    incl = jnp.where(i2 >= 2 * L, 1, 0)
    amask = ((((i2 >> ls_) & 1) == ((j2 >> ls_) & 1))
             & ((j2 & (L - 1)) < (i2 & (L - 1)) + incl))
    e_i = lax.broadcasted_iota(jnp.int32, (2 * L, 2 * L), 0)
    e_j = lax.broadcasted_iota(jnp.int32, (2 * L, 2 * L), 1)
    eye = e_i == e_j
    eye_f = jnp.where(eye, 1.0, 0.0)

    pairs = range(c // LANES)
    lsl = [slice(p * LANES, (p + 1) * LANES) for p in pairs]
    n_sq = int(math.log2(L)) - 2

    def chunk_group(grp):
        cins = []
        for j in range(RW_UNROLL):
            ch = grp * RW_UNROLL + j
            rows = slice(ch * L, (ch + 1) * L)
            ci_c = ci_sc[rows, :]
            last = ci_sc[ch * L + L - 1:ch * L + L, :]
            e_neg = jnp.exp(-ci_c)
            e_end = jnp.exp(last - ci_c)
            bs_c = bs_sc[rows, :]
            k_c = k_sc[rows, :]
            cins.append(dict(
                rows=rows, g_end=jnp.exp(last),
                at=as_sc[rows, :] * jnp.exp(ci_c - lw_sc[rows, :]),
                bt=bs_c * e_neg, kt=k_c * e_neg, rt=r_sc[rows, :] * jnp.exp(ci_c),
                bh=bs_c * e_end, kh=k_c * e_end, v=v_sc[rows, :]))
        combos = [(j, p) for j in range(RW_UNROLL) for p in pairs]
        at_s = [_stack_heads(cins[j]["at"][:, lsl[p]]) for j, p in combos]
        rt_s = [_stack_heads(cins[j]["rt"][:, lsl[p]]) for j, p in combos]
        v_s = [_stack_heads(cins[j]["v"][:, lsl[p]]) for j, p in combos]
        a_all = []
        for i, (j, p) in enumerate(combos):
            lhs = jnp.concatenate([at_s[i], rt_s[i]], axis=0).astype(BF16)
            rhs = jnp.concatenate([_stack_heads(cins[j]["bt"][:, lsl[p]]),
                                   _stack_heads(cins[j]["kt"][:, lsl[p]])], axis=0).astype(BF16)
            a = lax.dot_general(lhs, rhs, (((1,), (1,)), ((), ())), preferred_element_type=F32)
            a_all.append(jnp.where(amask, a, 0.0))
        ids = range(len(combos))
        a_ab = [a[:2 * L, :2 * L] for a in a_all]
        a_r = [a[2 * L:, :] for a in a_all]
        pw = [_bdot(a, a) for a in a_ab]
        tinv = [eye_f + a for a in a_ab]
        akv = [_bdot(a_all[i][:2 * L, 2 * L:], v_s[i]) for i in ids]
        for _ in range(n_sq):
            for i in ids:
                pw_b = pw[i].astype(BF16)
                both = jnp.dot(jnp.concatenate([pw_b, tinv[i].astype(BF16)], axis=0), pw_b,
                               preferred_element_type=F32)
                pw[i] = both[:2 * L]
                tinv[i] = tinv[i] + both[2 * L:]
        tinv = [tinv[i] + _bdot(tinv[i], pw[i]) for i in ids]
        wu = [_bdot(tinv[i], jnp.concatenate([at_s[i], akv[i]], axis=1)) for i in ids]
        bk_t = [jnp.concatenate([_stack_heads(cins[j]["bh"][:, lsl[p]]),
                                 _stack_heads(cins[j]["kh"][:, lsl[p]])], axis=0).T.astype(BF16)
                for j, p in combos]
        h = [h_sc[p] for p in pairs]
        for j in range(RW_UNROLL):
            sel = [j * len(pairs) + p for p in pairs]
            wrg = [_bdot(jnp.concatenate(
                [wu[i][:, :2 * L], rt_s[i], jnp.where(eye, cins[j]["g_end"][:, lsl[p]], 0.0)],
                axis=0), h[p]) for p, i in zip(pairs, sel)]
            uv = [jnp.concatenate([wrg[p][:2 * L] + wu[i][:, 2 * L:], v_s[i]],
                                  axis=0).astype(BF16) for p, i in zip(pairs, sel)]
            for p, i in zip(pairs, sel):
                y_st = wrg[p][2 * L:4 * L] + jnp.dot(a_r[i].astype(BF16), uv[p],
                                                     preferred_element_type=F32)
                y_sc[cins[j]["rows"], lsl[p]] = y_st[:L] + y_st[L:]
            h = [wrg[p][4 * L:] + jnp.dot(bk_t[i], uv[p], preferred_element_type=F32)
                 for p, i in zip(pairs, sel)]
        for p in pairs:
            h_sc[p] = h[p]

    n_grp = tb // grp_rows
    prep(0)
    for grp in range(n_grp):
        chunk_group(grp)
        if grp + 1 < n_grp:
            prep((grp + 1) * grp_rows)
        if grp > 0:
            post((grp - 1) * grp_rows)
    post((n_grp - 1) * grp_rows)


def _rwkv_mix(pr, v_first, prm, batch, seq):
    tb = RW_TB
    nt = seq // tb
    first = v_first is None
    c = D_RWKV
    row_spec = lambda w: pl.BlockSpec((tb, w), lambda b, t: (b * nt + t, 0))
    vec_spec = lambda w: pl.BlockSpec((1, w), lambda b, t: (0, 0))
    mat_spec = lambda: pl.BlockSpec((LANES, c), lambda b, t: (0, 0))
    vec = lambda x: x.reshape(1, -1)
    if first:
        args = [pr, vec(prm["w0"]), vec(prm["a0"]), vec(prm["k_k"]), vec(prm["k_a"]),
                vec(prm["r_k"]), vec(prm["gn_g"]), vec(prm["gn_b"]),
                prm["w_up"], prm["a_up"], prm["g_up1"], prm["g_up2"]]
        in_specs = [row_spec(RW_PAD)] + [vec_spec(c)] * 7 + [mat_spec()] * 4
        out_shape = [jax.ShapeDtypeStruct((batch * seq, c), BF16),
                     jax.ShapeDtypeStruct((batch * seq, c), F32)]
        out_specs = [row_spec(c), row_spec(c)]
    else:
        args = [pr, v_first, vec(prm["w0"]), vec(prm["a0"]), vec(prm["v0"]),
                vec(prm["k_k"]), vec(prm["k_a"]), vec(prm["r_k"]), vec(prm["gn_g"]), vec(prm["gn_b"]),
                prm["w_up"], prm["a_up"], prm["g_up1"], prm["g_up2"], prm["v_up"]]
        in_specs = [row_spec(RW_PAD), row_spec(c)] + [vec_spec(c)] * 8 + [mat_spec()] * 5
        out_shape = jax.ShapeDtypeStruct((batch * seq, c), BF16)
        out_specs = row_spec(c)
    scratch = [pltpu.VMEM((c // LANES, LANES, LANES), F32)]
    scratch += [pltpu.VMEM((tb, c), F32)] * 9
    return pl.pallas_call(
        functools.partial(_rwkv_body, tb=tb, first=first),
        grid=(batch, nt),
        in_specs=in_specs,
        out_specs=out_specs,
        out_shape=out_shape,
        scratch_shapes=scratch,
        compiler_params=_cparams(("parallel", "arbitrary")),
        name="rwkv7_mix_first" if first else "rwkv7_mix_rest",
    )(*args)


def _tail_body(yd_ref, yr_ref, x_ref, wo_ref, gm_ref, bm_ref, wu_ref, wd_ref, gf_ref, bf_ref,
               o_ref, z_sc, *, tf):
    s = pl.program_id(0)

    @pl.when(s == 0)
    def _():
        z_sc[...] = jnp.zeros(z_sc.shape, F32)

    slot = s % 2
    o_ref[...] = _layer_norm(z_sc[1 - slot], gf_ref[...], bf_ref[...])

    mix = (jnp.dot(yd_ref[...], wo_ref[:D_DIFF, :], preferred_element_type=F32)
           + jnp.dot(yr_ref[...], wo_ref[D_DIFF:, :], preferred_element_type=F32))
    x1 = _layer_norm(ALPHA * x_ref[...] + mix, gm_ref[...], bm_ref[...])
    xb = x1.astype(BF16)
    acc = ALPHA * x1
    for f in range(0, wu_ref.shape[1], tf):
        h = jnp.maximum(jnp.dot(xb, wu_ref[:, f:f + tf], preferred_element_type=F32), 0.0)
        acc = acc + jnp.dot((h * h).astype(BF16), wd_ref[f:f + tf, :], preferred_element_type=F32)
    z_sc[slot] = acc


def _layer_tail(yd, yr, x, wo, gm, bm, wu, wd, gf, bf, tm=512, tf=1024):
    t, d = x.shape
    ff = wu.shape[1]
    n = t // tm
    row_in = lambda wd_: pl.BlockSpec((tm, wd_), lambda s: (jnp.minimum(s, n - 1), 0))
    const = lambda r, c: pl.BlockSpec((r, c), lambda s: (0, 0), pipeline_mode=pl.Buffered(1))
    vec = lambda v: v.reshape(1, d)
    return pl.pallas_call(
        functools.partial(_tail_body, tf=tf),
        grid=(n + 1,),
        in_specs=[row_in(D_DIFF), row_in(D_RWKV), row_in(d), const(D_DIFF + D_RWKV, d),
                  const(1, d), const(1, d), const(d, ff), const(ff, d), const(1, d), const(1, d)],
        out_specs=pl.BlockSpec((tm, d), lambda s: (jnp.maximum(s - 1, 0), 0)),
        out_shape=jax.ShapeDtypeStruct((t, d), F32),
        scratch_shapes=[pltpu.VMEM((2, tm, d), F32)],
        compiler_params=_cparams(("arbitrary",)),
        name="out_proj_mlp_norm",
    )(yd, yr, x, wo, vec(gm), vec(bm), wu, wd, vec(gf), vec(bf))


def _pad_rows(w, before, total):
    return jnp.pad(w, ((before, total - before - w.shape[0]), (0, 0)))


def _rwkv_params(l, rw_w0, rw_w_up, rw_a0, rw_a_up, rw_g_up, rw_v0, rw_v_up,
                 rw_k_k, rw_k_a, rw_r_k, rw_gn_g, rw_gn_b):
    prm = {
        "w0": rw_w0[l], "a0": rw_a0[l], "k_k": rw_k_k[l], "k_a": rw_k_a[l],
        "r_k": rw_r_k[l].reshape(-1), "gn_g": rw_gn_g[l], "gn_b": rw_gn_b[l],
        "w_up": _pad_rows(rw_w_up[l], 0, LANES).astype(BF16),
        "a_up": _pad_rows(rw_a_up[l], LORA_W, LANES).astype(BF16),
        "g_up1": rw_g_up[l][:LANES].astype(BF16),
        "g_up2": _pad_rows(rw_g_up[l][LANES:], 0, LANES).astype(BF16),
    }
    if l > 0:
        prm["v0"] = rw_v0[l - 1]
        prm["v_up"] = _pad_rows(rw_v_up[l - 1], LORA_G - LANES, LANES).astype(BF16)
    return prm


def _forward(x, ln_in_g, ln_in_b, w_in_first, w_in_rest, mu_first, mu_rest, rel_bias,
             lambda_q1, lambda_k1, lambda_q2, lambda_k2, subln_g,
             rw_w0, rw_w_up, rw_a0, rw_a_up, rw_g_up, rw_v0, rw_v_up,
             rw_k_k, rw_k_a, rw_r_k, rw_gn_g, rw_gn_b,
             w_out, ln_mix_g, ln_mix_b, w_up, w_down, ln_ffn_g, ln_ffn_b):
    batch, seq, d = x.shape
    xs = x.reshape(batch * seq, d)
    bias_tiles = _bias_tiles(rel_bias, ATT_T)
    v_first = None
    for l in range(DEPTH):
        w_in = w_in_first if l == 0 else w_in_rest[l - 1]
        mu = mu_first if l == 0 else mu_rest[l - 1]
        w_pad = jnp.pad(w_in, ((0, 0), (0, N_DIFF + RW_PAD - w_in.shape[1]))).astype(BF16)
        mu_pad = jnp.pad(mu, (0, RW_PAD - mu.shape[0]))
        if l == 0:
            xs, qkv, pr = _in_proj(xs, w_pad, mu_pad, (ln_in_g, ln_in_b), seq)
        else:
            qkv, pr = _in_proj(xs, w_pad, mu_pad, None, seq)

        lam_init = 0.8 - 0.6 * math.exp(-0.3 * l)
        lam_params = jnp.stack([lambda_q1[l], lambda_k1[l], lambda_q2[l], lambda_k2[l]])
        y_diff = _diff_attention(qkv, bias_tiles, rel_bias, lam_params, subln_g[l], lam_init, batch, seq)

        prm = _rwkv_params(l, rw_w0, rw_w_up, rw_a0, rw_a_up, rw_g_up, rw_v0, rw_v_up,
                           rw_k_k, rw_k_a, rw_r_k, rw_gn_g, rw_gn_b)
        if l == 0:
            y_rw, v_first = _rwkv_mix(pr, None, prm, batch, seq)
        else:
            y_rw = _rwkv_mix(pr, v_first, prm, batch, seq)

        xs = _layer_tail(y_diff, y_rw, xs, w_out[l].astype(BF16), ln_mix_g[l], ln_mix_b[l],
                         w_up[l].astype(BF16), w_down[l].astype(BF16), ln_ffn_g[l], ln_ffn_b[l])
    return xs.reshape(batch, seq, d)


def kernel(x, ln_in_g, ln_in_b, w_in_first, w_in_rest, mu_first, mu_rest, rel_bias, lambda_q1, lambda_k1, lambda_q2, lambda_k2, subln_g, rw_w0, rw_w_up, rw_a0, rw_a_up, rw_g_up, rw_v0, rw_v_up, rw_k_k, rw_k_a, rw_r_k, rw_gn_g, rw_gn_b, w_out, ln_mix_g, ln_mix_b, w_up, w_down, ln_ffn_g, ln_ffn_b):
    return _forward(x, ln_in_g, ln_in_b, w_in_first, w_in_rest, mu_first, mu_rest, rel_bias,
                    lambda_q1, lambda_k1, lambda_q2, lambda_k2, subln_g,
                    rw_w0, rw_w_up, rw_a0, rw_a_up, rw_g_up, rw_v0, rw_v_up,
                    rw_k_k, rw_k_a, rw_r_k, rw_gn_g, rw_gn_b,
                    w_out, ln_mix_g, ln_mix_b, w_up, w_down, ln_ffn_g, ln_ffn_b)
```

```python
import functools
import math

import jax
import jax.numpy as jnp
from jax import lax
from jax.experimental import pallas as pl
from jax.experimental.pallas import tpu as pltpu

F32 = jnp.float32
BF16 = jnp.bfloat16

D_MODEL = 1024
DEPTH = 2
D_DIFF = 512
D_RWKV = 512
DIFF_HEADS = 4
DIFF_HEAD_DIM = 64
DIFF_V_DIM = 128
RWKV_HEAD = 64
LORA_W = 64
LORA_A = 64
LORA_V = 32
LORA_G = 160
D_FF = 4 * D_MODEL
N_BUCKETS = 32
MAX_DISTANCE = 128
LN_EPS = 1e-5
SUBLN_EPS = 1e-5
GN_EPS = 64e-5
ALPHA = (2 * DEPTH) ** 0.25
N_DIFF = 3 * D_DIFF
RW_W = 3 * D_RWKV

LANES = 128
MXU_W = 256
VMEM_LIMIT = 56 * 1024 * 1024

RW_PAD = 2048
ATT_T = 512
RW_TB = 512
RW_L = 64
RW_UNROLL = 4
NEG = -1e30


def _cparams(sem):
    return pltpu.CompilerParams(dimension_semantics=sem, vmem_limit_bytes=VMEM_LIMIT)


def _layer_norm(z, g, b):
    mu = jnp.mean(z, -1, keepdims=True)
    zc = z - mu
    var = jnp.mean(zc * zc, -1, keepdims=True)
    return zc * lax.rsqrt(var + LN_EPS) * g + b


def _proj_body(*refs, with_ln, tn, tiles_per_seq):
    if with_ln:
        x_ref, g_ref, b_ref, w_ref, mu_ref, xn_ref, qkv_ref, pr_ref, carry_sc = refs
        xn = _layer_norm(x_ref[...], g_ref[...], b_ref[...])
        xn_ref[...] = xn
    else:
        x_ref, w_ref, mu_ref, qkv_ref, pr_ref, carry_sc = refs
        xn = x_ref[...]
    tm = x_ref.shape[0]
    xb = xn.astype(BF16)
    seq_start = (pl.program_id(0) % tiles_per_seq) == 0
    row8 = lax.broadcasted_iota(jnp.int32, (8, 1), 0)
    for j in range(w_ref.shape[1] // tn):
        o = jnp.dot(xb, w_ref[:, j * tn:(j + 1) * tn], preferred_element_type=F32)
        if (j + 1) * tn <= N_DIFF:
            qkv_ref[:, j * tn:(j + 1) * tn] = o.astype(BF16)
        else:
            cols = slice(j * tn - N_DIFF, (j + 1) * tn - N_DIFF)
            carry = jnp.where(seq_start, 0.0, carry_sc[:, cols])
            rolled = pltpu.roll(o, 1, 0)
            prev = jnp.concatenate([jnp.where(row8 == 0, carry, rolled[:8]), rolled[8:]], axis=0)
            carry_sc[:, cols] = o[tm - 1:tm, :]
            pr_ref[:, cols] = o + (prev - o) * mu_ref[:, cols]


def _in_proj(x, w, mu, ln, seq, tm=512, tn=512):
    t, d = x.shape
    n = w.shape[1]
    with_ln = ln is not None
    row = lambda wd: pl.BlockSpec((tm, wd), lambda i: (i, 0))
    const = lambda r, wd: pl.BlockSpec((r, wd), lambda i: (0, 0))
    in_specs = ([row(d)] + ([const(1, d), const(1, d)] if with_ln else [])
                + [const(d, n), const(1, n - N_DIFF)])
    args = ([x] + ([ln[0].reshape(1, d), ln[1].reshape(1, d)] if with_ln else [])
            + [w, mu.reshape(1, n - N_DIFF)])
    out_shape = [jax.ShapeDtypeStruct((t, N_DIFF), BF16), jax.ShapeDtypeStruct((t, n - N_DIFF), F32)]
    out_specs = [row(N_DIFF), row(n - N_DIFF)]
    if with_ln:
        out_shape = [jax.ShapeDtypeStruct((t, d), F32)] + out_shape
        out_specs = [row(d)] + out_specs
    return pl.pallas_call(
        functools.partial(_proj_body, with_ln=with_ln, tn=tn, tiles_per_seq=seq // tm),
        grid=(t // tm,),
        in_specs=in_specs,
        out_specs=out_specs,
        out_shape=out_shape,
        scratch_shapes=[pltpu.VMEM((1, n - N_DIFF), F32)],
        compiler_params=_cparams(("arbitrary",)),
        name="ln_in_proj" if with_ln else "in_proj",
    )(*args)


def _bias_body(rb_ref, o_ref, *, t):
    h = pl.program_id(0)
    which = pl.program_id(1)
    i = lax.broadcasted_iota(jnp.int32, (t, t), 0)
    j = lax.broadcasted_iota(jnp.int32, (t, t), 1)
    n = jnp.maximum(which * t + i - j, 0)
    max_exact = N_BUCKETS // 2
    n_log = N_BUCKETS - max_exact
    steps = [math.ceil(max_exact * (MAX_DISTANCE / max_exact) ** (k / n_log))
             for k in range(1, n_log)]
    bucket = jnp.minimum(n, max_exact)
    for thr in steps:
        bucket = bucket + jnp.where(n >= thr, 1, 0)
    acc = jnp.zeros((t, t), F32)
    for b in range(N_BUCKETS):
        acc = jnp.where(bucket == b, rb_ref[b, h], acc)
    o_ref[0, 0] = acc


def _bias_tiles(rel_bias, t):
    return pl.pallas_call(
        functools.partial(_bias_body, t=t),
        grid=(DIFF_HEADS, 2),
        in_specs=[pl.BlockSpec(memory_space=pltpu.SMEM)],
        out_specs=pl.BlockSpec((1, 1, t, t), lambda h, w: (h, w, 0, 0)),
        out_shape=jax.ShapeDtypeStruct((DIFF_HEADS, 2, t, t), F32),
        compiler_params=_cparams(("parallel", "parallel")),
        name="t5_bias_tiles",
    )(rel_bias)


def _attn_body(rb_ref, lamp_ref, q_ref, k_ref, v_ref, bias_ref, g_ref, o_ref,
               m_sc, acc_sc, *, t, nq, lam_init):
    h = pl.program_id(1)
    lane = lax.broadcasted_iota(jnp.int32, (t, LANES), 1)
    c_far = rb_ref[N_BUCKETS - 1, h]
    lp = lamp_ref[...]
    lam = (jnp.exp(jnp.sum(lp[0:1] * lp[1:2], -1, keepdims=True))
           - jnp.exp(jnp.sum(lp[2:3] * lp[3:4], -1, keepdims=True)) + lam_init)

    def q_tile(iq):
        qrows = slice(iq * t, (iq + 1) * t)
        q = q_ref[qrows, :].astype(F32) * (DIFF_HEAD_DIM ** -0.5)
        qm = (jnp.where(lane < DIFF_HEAD_DIM, q, 0.0).astype(BF16),
              jnp.where(lane < DIFF_HEAD_DIM, 0.0, q).astype(BF16))
        m_sc[...] = jnp.full(m_sc.shape, NEG, F32)
        acc_sc[...] = jnp.zeros(acc_sc.shape, F32)

        def tile(row0, width, bias, diag_col0):
            rows = slice(row0, row0 + width)
            kt = k_ref[rows, :]
            vt = jnp.concatenate([v_ref[rows, :], jnp.ones((width, LANES), BF16)], axis=1)
            maps = range(2)
            s = [lax.dot_general(qm[m], kt, (((1,), (1,)), ((), ())), preferred_element_type=F32)
                 for m in maps]
            if bias is None:
                m_cur = [jnp.max(s[m], -1, keepdims=True) + c_far for m in maps]
            else:
                s = [s[m] + bias for m in maps]
                if diag_col0 is not None:
                    ri = lax.broadcasted_iota(jnp.int32, (t, width), 0)
                    ci = lax.broadcasted_iota(jnp.int32, (t, width), 1)
                    s = [jnp.where(ri >= ci - diag_col0, s[m], NEG) for m in maps]
                m_cur = [jnp.max(s[m], -1, keepdims=True) for m in maps]
            m_old = [m_sc[m] for m in maps]
            m_new = [jnp.maximum(m_old[m], m_cur[m]) for m in maps]
            a = [jnp.exp(m_old[m] - m_new[m]) for m in maps]
            sub = [m_new[m] - c_far if bias is None else m_new[m] for m in maps]
            p = [jnp.exp(s[m] - jnp.concatenate([sub[m]] * (width // LANES), axis=1)).astype(BF16)
                 for m in maps]
            pv = [jnp.dot(p[m], vt, preferred_element_type=F32) for m in maps]
            for m in maps:
                acc_sc[m] = jnp.concatenate([a[m], a[m]], axis=1) * acc_sc[m] + pv[m]
                m_sc[m] = m_new[m]

        n_far = max(iq - 1, 0)
        for j in range(n_far // 2):
            tile(j * 2 * t, 2 * t, None, None)
        if n_far % 2 == 1:
            tile((iq - 2) * t, 3 * t,
                 jnp.concatenate([jnp.full((t, t), c_far, F32), bias_ref[0, 1], bias_ref[0, 0]],
                                 axis=1), 2 * t)
        elif iq > 0:
            tile((iq - 1) * t, 2 * t,
                 jnp.concatenate([bias_ref[0, 1], bias_ref[0, 0]], axis=1), t)
        else:
            tile(0, t, bias_ref[0, 0], 0)

        a0 = acc_sc[0]
        a1 = acc_sc[1]
        out = a0[:, :LANES] / a0[:, LANES:] - lam * (a1[:, :LANES] / a1[:, LANES:])
        out = out * lax.rsqrt(jnp.mean(out * out, -1, keepdims=True) + SUBLN_EPS) * g_ref[...]
        o_ref[qrows, :] = (out * (1.0 - lam_init)).astype(o_ref.dtype)

    for iq in range(nq):
        pl.when(h >= 0)(functools.partial(q_tile, iq))


def _diff_attention(qkv, bias_tiles, rel_bias, lam_params, subln_g, lam_init, batch, seq):
    t = ATT_T
    hb = D_DIFF // LANES
    return pl.pallas_call(
        functools.partial(_attn_body, t=t, nq=seq // t, lam_init=lam_init),
        grid=(batch, DIFF_HEADS),
        in_specs=[pl.BlockSpec(memory_space=pltpu.SMEM),
                  pl.BlockSpec((4, DIFF_HEAD_DIM), lambda b, h: (0, 0)),
                  pl.BlockSpec((seq, LANES), lambda b, h: (b, h)),
                  pl.BlockSpec((seq, LANES), lambda b, h: (b, hb + h)),
                  pl.BlockSpec((seq, LANES), lambda b, h: (b, 2 * hb + h)),
                  pl.BlockSpec((1, 2, t, t), lambda b, h: (h, 0, 0, 0)),
                  pl.BlockSpec((1, DIFF_V_DIM), lambda b, h: (0, 0))],
        out_specs=pl.BlockSpec((seq, LANES), lambda b, h: (b, h)),
        out_shape=jax.ShapeDtypeStruct((batch * seq, D_DIFF), BF16),
        scratch_shapes=[pltpu.VMEM((2, t, LANES), F32),
                        pltpu.VMEM((2, t, DIFF_V_DIM + LANES), F32)],
        compiler_params=_cparams(("parallel", "parallel")),
        name="diff_attention",
    )(rel_bias, lam_params, qkv, qkv, qkv, bias_tiles, subln_g.reshape(1, DIFF_V_DIM))


def _split_dot(x, m):
    hi = x.astype(BF16)
    lo = (x - hi.astype(F32)).astype(BF16)
    return (jnp.dot(hi, m, preferred_element_type=F32)
            + jnp.dot(lo, m, preferred_element_type=F32))


def _stack_heads(x):
    lane = lax.broadcasted_iota(jnp.int32, x.shape, 1)
    return jnp.concatenate([jnp.where(lane < RWKV_HEAD, x, 0.0),
                            jnp.where(lane < RWKV_HEAD, 0.0, x)], axis=0)


def _bdot(a, b):
    return jnp.dot(a.astype(BF16), b.astype(BF16), preferred_element_type=F32)


def _rwkv_body(*refs, tb, first):
    if first:
        (pr_ref, w0_ref, a0_ref, kk_ref, ka_ref, rk_ref, gng_ref, gnb_ref,
         wup_ref, aup_ref, gup1_ref, gup2_ref,
         y_ref, vout_ref,
         h_sc, r_sc, k_sc, v_sc, as_sc, bs_sc, ci_sc, lw_sc, g_sc, y_sc) = refs
    else:
        (pr_ref, vf_ref, w0_ref, a0_ref, v0_ref, kk_ref, ka_ref, rk_ref, gng_ref, gnb_ref,
         wup_ref, aup_ref, gup1_ref, gup2_ref, vup_ref,
         y_ref,
         h_sc, r_sc, k_sc, v_sc, as_sc, bs_sc, ci_sc, lw_sc, g_sc, y_sc) = refs
    L = RW_L
    c = D_RWKV
    grp_rows = L * RW_UNROLL
    assert grp_rows == MXU_W and tb % grp_rows == 0

    @pl.when(pl.program_id(1) == 0)
    def _():
        h_sc[...] = jnp.zeros(h_sc.shape, F32)

    bi = lax.broadcasted_iota(jnp.int32, (MXU_W, MXU_W), 0)
    bj = lax.broadcasted_iota(jnp.int32, (MXU_W, MXU_W), 1)
    hs = RWKV_HEAD.bit_length() - 1
    head_ones = jnp.where((bi >> hs) == (bj >> hs), 1.0, 0.0).astype(BF16)
    ls_ = L.bit_length() - 1
    tri = jnp.where(((bi >> ls_) == (bj >> ls_)) & (bj <= bi), 1.0, 0.0).astype(BF16)

    def head_sum(x):
        return jnp.concatenate([_split_dot(x[:, i:i + MXU_W], head_ones)
                                for i in range(0, c, MXU_W)], axis=1)

    def prep(r0):
        rows = slice(r0, r0 + MXU_W)
        r = pr_ref[rows, 0:c]
        k = pr_ref[rows, c:2 * c]
        v = pr_ref[rows, 2 * c:3 * c]
        t1 = pr_ref[rows, RW_W:RW_W + LANES]
        t2 = pr_ref[rows, RW_W + LANES:RW_W + 2 * LANES]
        t3 = pr_ref[rows, RW_W + 2 * LANES:RW_W + 3 * LANES]

        lw = _bdot(jnp.tanh(t1), wup_ref[...])
        la = _bdot(t1, aup_ref[...])
        g_sc[rows, :] = (_bdot(jax.nn.sigmoid(t2), gup1_ref[...])
                         + _bdot(jax.nn.sigmoid(t3), gup2_ref[...]))
        yield

        z = -(w0_ref[...] + lw)
        softplus = jnp.maximum(z, 0.0) + jnp.log(1.0 + jnp.exp(-jnp.abs(z)))
        logw = -jnp.exp(-softplus - 0.5)
        yield
        a = jax.nn.sigmoid(a0_ref[...] + la)
        if first:
            vout_ref[rows, :] = v
        else:
            v = v + (vf_ref[rows, :] - v) * jax.nn.sigmoid(v0_ref[...] + _bdot(t3, vup_ref[...]))
        yield

        kk = k * kk_ref[...]
        kk = kk * jnp.minimum(lax.rsqrt(head_sum(kk * kk)), 1e12)
        yield

        hi = logw.astype(BF16)
        rem = logw - hi.astype(F32)
        mid = rem.astype(BF16)
        lo = (rem - mid.astype(F32)).astype(BF16)
        ci_sc[rows, :] = (jnp.dot(tri, hi, preferred_element_type=F32)
                          + jnp.dot(tri, mid, preferred_element_type=F32)
                          + jnp.dot(tri, lo, preferred_element_type=F32))
        yield
        r_sc[rows, :] = r
        k_sc[rows, :] = k * (1.0 + (a - 1.0) * ka_ref[...])
        v_sc[rows, :] = v
        as_sc[rows, :] = -kk
        bs_sc[rows, :] = kk * a
        lw_sc[rows, :] = logw
        yield

    def post(r0):
        rows = slice(r0, r0 + MXU_W)
        y = y_sc[rows, :]
        mean = head_sum(y) * (1.0 / RWKV_HEAD)
        yield
        yc = y - mean
        var = head_sum(yc * yc) * (1.0 / RWKV_HEAD)
        yield
        yn = yc * lax.rsqrt(var + GN_EPS) * gng_ref[...] + gnb_ref[...]
        bonus = head_sum(r_sc[rows, :] * k_sc[rows, :] * rk_ref[...]) * v_sc[rows, :]
        yield
        y_ref[rows, :] = ((yn + bonus) * g_sc[rows, :]).astype(y_ref.dtype)
        yield

    i2 = lax.broadcasted_iota(jnp.int32, (4 * L, 4 * L), 0)
    j2 = lax.broadcasted_iota(jnp.int32, (4 * L, 4 * L), 1)
    incl = jnp.where(i2 >= 2 * L, 1, 0)
    amask = ((((i2 >> ls_) & 1) == ((j2 >> ls_) & 1))
             & ((j2 & (L - 1)) < (i2 & (L - 1)) + incl))
    e_i = lax.broadcasted_iota(jnp.int32, (2 * L, 2 * L), 0)
    e_j = lax.broadcasted_iota(jnp.int32, (2 * L, 2 * L), 1)
    eye = e_i == e_j
    eye_f = jnp.where(eye, 1.0, 0.0)

    pairs = range(c // LANES)
    lsl = [slice(p * LANES, (p + 1) * LANES) for p in pairs]
    n_sq = int(math.log2(L)) - 2

    def group_setup(grp, out):
        cins = []
        for j in range(RW_UNROLL):
            ch = grp * RW_UNROLL + j
            rows = slice(ch * L, (ch + 1) * L)
            ci_c = ci_sc[rows, :]
            last = ci_sc[ch * L + L - 1:ch * L + L, :]
            e_neg = jnp.exp(-ci_c)
            e_end = jnp.exp(last - ci_c)
            bs_c = bs_sc[rows, :]
            k_c = k_sc[rows, :]
            cins.append(dict(
                rows=rows, g_end=jnp.exp(last),
                at=as_sc[rows, :] * jnp.exp(ci_c - lw_sc[rows, :]),
                bt=bs_c * e_neg, kt=k_c * e_neg, rt=r_sc[rows, :] * jnp.exp(ci_c),
                bh=bs_c * e_end, kh=k_c * e_end, v=v_sc[rows, :]))
            yield
        combos = [(j, p) for j in range(RW_UNROLL) for p in pairs]
        at_s = [_stack_heads(cins[j]["at"][:, lsl[p]]) for j, p in combos]
        rt_s = [_stack_heads(cins[j]["rt"][:, lsl[p]]) for j, p in combos]
        v_s = [_stack_heads(cins[j]["v"][:, lsl[p]]) for j, p in combos]
        a_all = []
        for i, (j, p) in enumerate(combos):
            lhs = jnp.concatenate([at_s[i], rt_s[i]], axis=0).astype(BF16)
            rhs = jnp.concatenate([_stack_heads(cins[j]["bt"][:, lsl[p]]),
                                   _stack_heads(cins[j]["kt"][:, lsl[p]])], axis=0).astype(BF16)
            a = lax.dot_general(lhs, rhs, (((1,), (1,)), ((), ())), preferred_element_type=F32)
            a_all.append(jnp.where(amask, a, 0.0))
        yield
        ids = range(len(combos))
        a_ab = [a[:2 * L, :2 * L] for a in a_all]
        a_r = [a[2 * L:, :] for a in a_all]
        pw = [_bdot(a, a) for a in a_ab]
        tinv = [eye_f + a for a in a_ab]
        akv = [_bdot(a_all[i][:2 * L, 2 * L:], v_s[i]) for i in ids]
        yield
        for _ in range(n_sq):
            for i in ids:
                pw_b = pw[i].astype(BF16)
                both = jnp.dot(jnp.concatenate([pw_b, tinv[i].astype(BF16)], axis=0), pw_b,
                               preferred_element_type=F32)
                pw[i] = both[:2 * L]
                tinv[i] = tinv[i] + both[2 * L:]
            yield
        tinv = [tinv[i] + _bdot(tinv[i], pw[i]) for i in ids]
        yield
        wu = [_bdot(tinv[i], jnp.concatenate([at_s[i], akv[i]], axis=1)) for i in ids]
        yield
        bk_t = [jnp.concatenate([_stack_heads(cins[j]["bh"][:, lsl[p]]),
                                 _stack_heads(cins[j]["kh"][:, lsl[p]])], axis=0).T.astype(BF16)
                for j, p in combos]
        out.update(cins=cins, wu=wu, rt_s=rt_s, v_s=v_s, a_r=a_r, bk_t=bk_t)
        yield

    def group_state(setup):
        cins, wu, rt_s, v_s, a_r, bk_t = (setup[k] for k in
                                          ("cins", "wu", "rt_s", "v_s", "a_r", "bk_t"))
        h = [h_sc[p] for p in pairs]
        for j in range(RW_UNROLL):
            sel = [j * len(pairs) + p for p in pairs]
            wrg = [_bdot(jnp.concatenate(
                [wu[i][:, :2 * L], rt_s[i], jnp.where(eye, cins[j]["g_end"][:, lsl[p]], 0.0)],
                axis=0), h[p]) for p, i in zip(pairs, sel)]
            yield
            uv = [jnp.concatenate([wrg[p][:2 * L] + wu[i][:, 2 * L:], v_s[i]],
                                  axis=0).astype(BF16) for p, i in zip(pairs, sel)]
            for p, i in zip(pairs, sel):
                y_st = wrg[p][2 * L:4 * L] + jnp.dot(a_r[i].astype(BF16), uv[p],
                                                     preferred_element_type=F32)
                y_sc[cins[j]["rows"], lsl[p]] = y_st[:L] + y_st[L:]
            h = [wrg[p][4 * L:] + jnp.dot(bk_t[i], uv[p], preferred_element_type=F32)
                 for p, i in zip(pairs, sel)]
            yield
        for p in pairs:
            h_sc[p] = h[p]
        yield

    def interleave(*gens):
        gens = list(gens)
        while gens:
            for gen in list(gens):
                try:
                    next(gen)
                except StopIteration:
                    gens.remove(gen)

    n_grp = tb // grp_rows
    setups = [dict() for _ in range(n_grp)]
    interleave(prep(0))
    interleave(group_setup(0, setups[0]), *([prep(grp_rows)] if n_grp > 1 else []))
    for grp in range(n_grp):
        side = []
        if grp + 1 < n_grp:
            side.append(group_setup(grp + 1, setups[grp + 1]))
        if grp + 2 < n_grp:
            side.append(prep((grp + 2) * grp_rows))
        if grp > 0:
            side.append(post((grp - 1) * grp_rows))
        interleave(group_state(setups[grp]), *side)
    interleave(post((n_grp - 1) * grp_rows))


def _rwkv_mix(pr, v_first, prm, batch, seq):
    tb = RW_TB
    nt = seq // tb
    first = v_first is None
    c = D_RWKV
    row_spec = lambda w: pl.BlockSpec((tb, w), lambda b, t: (b * nt + t, 0))
    vec_spec = lambda w: pl.BlockSpec((1, w), lambda b, t: (0, 0))
    mat_spec = lambda: pl.BlockSpec((LANES, c), lambda b, t: (0, 0))
    vec = lambda x: x.reshape(1, -1)
    if first:
        args = [pr, vec(prm["w0"]), vec(prm["a0"]), vec(prm["k_k"]), vec(prm["k_a"]),
                vec(prm["r_k"]), vec(prm["gn_g"]), vec(prm["gn_b"]),
                prm["w_up"], prm["a_up"], prm["g_up1"], prm["g_up2"]]
        in_specs = [row_spec(RW_PAD)] + [vec_spec(c)] * 7 + [mat_spec()] * 4
        out_shape = [jax.ShapeDtypeStruct((batch * seq, c), BF16),
                     jax.ShapeDtypeStruct((batch * seq, c), F32)]
        out_specs = [row_spec(c), row_spec(c)]
    else:
        args = [pr, v_first, vec(prm["w0"]), vec(prm["a0"]), vec(prm["v0"]),
                vec(prm["k_k"]), vec(prm["k_a"]), vec(prm["r_k"]), vec(prm["gn_g"]), vec(prm["gn_b"]),
                prm["w_up"], prm["a_up"], prm["g_up1"], prm["g_up2"], prm["v_up"]]
        in_specs = [row_spec(RW_PAD), row_spec(c)] + [vec_spec(c)] * 8 + [mat_spec()] * 5
        out_shape = jax.ShapeDtypeStruct((batch * seq, c), BF16)
        out_specs = row_spec(c)
    scratch = [pltpu.VMEM((c // LANES, LANES, LANES), F32)]
    scratch += [pltpu.VMEM((tb, c), F32)] * 9
    return pl.pallas_call(
        functools.partial(_rwkv_body, tb=tb, first=first),
        grid=(batch, nt),
        in_specs=in_specs,
        out_specs=out_specs,
        out_shape=out_shape,
        scratch_shapes=scratch,
        compiler_params=_cparams(("parallel", "arbitrary")),
        name="rwkv7_mix_first" if first else "rwkv7_mix_rest",
    )(*args)


def _tail_body(yd_ref, yr_ref, x_ref, wo_ref, gm_ref, bm_ref, wu_ref, wd_ref, gf_ref, bf_ref,
               o_ref, z_sc, *, tf):
    s = pl.program_id(0)

    @pl.when(s == 0)
    def _():
        z_sc[...] = jnp.zeros(z_sc.shape, F32)

    slot = s % 2
    n_f = wu_ref.shape[1] // tf
    qr = x_ref.shape[0] // n_f

    tm = x_ref.shape[0]
    halves = (slice(0, tm // 2), slice(tm // 2, tm))
    mixh = [jnp.dot(yd_ref[r, :], wo_ref[:D_DIFF, :], preferred_element_type=F32)
            + jnp.dot(yr_ref[r, :], wo_ref[D_DIFF:, :], preferred_element_type=F32)
            for r in halves]
    x1h = [_layer_norm(ALPHA * x_ref[r, :] + m, gm_ref[...], bm_ref[...])
           for r, m in zip(halves, mixh)]
    h0 = [jnp.dot(x.astype(BF16), wu_ref[:, :tf], preferred_element_type=F32) for x in x1h]
    x1 = jnp.concatenate(x1h, axis=0)
    xb = x1.astype(BF16)
    acc = ALPHA * x1
    pending = []
    for i in range(n_f):
        f = i * tf
        floor = 0.0
        if pending:
            tie = sum(jnp.sum(p, axis=0, keepdims=True) for p in pending)
            floor = jnp.where(tie != tie, 1.0, 0.0)
            pending = []
        up = (jnp.concatenate(h0, axis=0) if i == 0
              else jnp.dot(xb, wu_ref[:, f:f + tf], preferred_element_type=F32))
        h = jnp.maximum(up, floor)
        acc = acc + jnp.dot((h * h).astype(BF16), wd_ref[f:f + tf, :], preferred_element_type=F32)
        for p in range(n_f):
            if min(p, n_f - 2) == i:
                rows = slice(p * qr, (p + 1) * qr)
                prev = _layer_norm(z_sc[1 - slot, rows, :], gf_ref[...], bf_ref[...])
                o_ref[rows, :] = prev
                pending.append(prev)
    z_sc[slot] = acc


def _layer_tail(yd, yr, x, wo, gm, bm, wu, wd, gf, bf, tm=512, tf=1024):
    t, d = x.shape
    ff = wu.shape[1]
    n = t // tm
    row_in = lambda wd_: pl.BlockSpec((tm, wd_), lambda s: (jnp.minimum(s, n - 1), 0))
    const = lambda r, c: pl.BlockSpec((r, c), lambda s: (0, 0), pipeline_mode=pl.Buffered(1))
    vec = lambda v: v.reshape(1, d)
    return pl.pallas_call(
        functools.partial(_tail_body, tf=tf),
        grid=(n + 1,),
        in_specs=[row_in(D_DIFF), row_in(D_RWKV), row_in(d), const(D_DIFF + D_RWKV, d),
                  const(1, d), const(1, d), const(d, ff), const(ff, d), const(1, d), const(1, d)],
        out_specs=pl.BlockSpec((tm, d), lambda s: (jnp.maximum(s - 1, 0), 0)),
        out_shape=jax.ShapeDtypeStruct((t, d), F32),
        scratch_shapes=[pltpu.VMEM((2, tm, d), F32)],
        compiler_params=_cparams(("arbitrary",)),
        name="out_proj_mlp_norm",
    )(yd, yr, x, wo, vec(gm), vec(bm), wu, wd, vec(gf), vec(bf))


def _pad_rows(w, before, total):
    return jnp.pad(w, ((before, total - before - w.shape[0]), (0, 0)))


def _rwkv_params(l, rw_w0, rw_w_up, rw_a0, rw_a_up, rw_g_up, rw_v0, rw_v_up,
                 rw_k_k, rw_k_a, rw_r_k, rw_gn_g, rw_gn_b):
    prm = {
        "w0": rw_w0[l], "a0": rw_a0[l], "k_k": rw_k_k[l], "k_a": rw_k_a[l],
        "r_k": rw_r_k[l].reshape(-1), "gn_g": rw_gn_g[l], "gn_b": rw_gn_b[l],
        "w_up": _pad_rows(rw_w_up[l], 0, LANES).astype(BF16),
        "a_up": _pad_rows(rw_a_up[l], LORA_W, LANES).astype(BF16),
        "g_up1": rw_g_up[l][:LANES].astype(BF16),
        "g_up2": _pad_rows(rw_g_up[l][LANES:], 0, LANES).astype(BF16),
    }
    if l > 0:
        prm["v0"] = rw_v0[l - 1]
        prm["v_up"] = _pad_rows(rw_v_up[l - 1], LORA_G - LANES, LANES).astype(BF16)
    return prm


def _forward(x, ln_in_g, ln_in_b, w_in_first, w_in_rest, mu_first, mu_rest, rel_bias,
             lambda_q1, lambda_k1, lambda_q2, lambda_k2, subln_g,
             rw_w0, rw_w_up, rw_a0, rw_a_up, rw_g_up, rw_v0, rw_v_up,
             rw_k_k, rw_k_a, rw_r_k, rw_gn_g, rw_gn_b,
             w_out, ln_mix_g, ln_mix_b, w_up, w_down, ln_ffn_g, ln_ffn_b):
    batch, seq, d = x.shape
    xs = x.reshape(batch * seq, d)
    bias_tiles = _bias_tiles(rel_bias, ATT_T)
    v_first = None
    for l in range(DEPTH):
        w_in = w_in_first if l == 0 else w_in_rest[l - 1]
        mu = mu_first if l == 0 else mu_rest[l - 1]
        w_pad = jnp.pad(w_in, ((0, 0), (0, N_DIFF + RW_PAD - w_in.shape[1]))).astype(BF16)
        mu_pad = jnp.pad(mu, (0, RW_PAD - mu.shape[0]))
        if l == 0:
            xs, qkv, pr = _in_proj(xs, w_pad, mu_pad, (ln_in_g, ln_in_b), seq)
        else:
            qkv, pr = _in_proj(xs, w_pad, mu_pad, None, seq)

        lam_init = 0.8 - 0.6 * math.exp(-0.3 * l)
        lam_params = jnp.stack([lambda_q1[l], lambda_k1[l], lambda_q2[l], lambda_k2[l]])
        y_diff = _diff_attention(qkv, bias_tiles, rel_bias, lam_params, subln_g[l], lam_init, batch, seq)

        prm = _rwkv_params(l, rw_w0, rw_w_up, rw_a0, rw_a_up, rw_g_up, rw_v0, rw_v_up,
                           rw_k_k, rw_k_a, rw_r_k, rw_gn_g, rw_gn_b)
        if l == 0:
            y_rw, v_first = _rwkv_mix(pr, None, prm, batch, seq)
        else:
            y_rw = _rwkv_mix(pr, v_first, prm, batch, seq)

        xs = _layer_tail(y_diff, y_rw, xs, w_out[l].astype(BF16), ln_mix_g[l], ln_mix_b[l],
                         w_up[l].astype(BF16), w_down[l].astype(BF16), ln_ffn_g[l], ln_ffn_b[l])
    return xs.reshape(batch, seq, d)


def kernel(x, ln_in_g, ln_in_b, w_in_first, w_in_rest, mu_first, mu_rest, rel_bias, lambda_q1, lambda_k1, lambda_q2, lambda_k2, subln_g, rw_w0, rw_w_up, rw_a0, rw_a_up, rw_g_up, rw_v0, rw_v_up, rw_k_k, rw_k_a, rw_r_k, rw_gn_g, rw_gn_b, w_out, ln_mix_g, ln_mix_b, w_up, w_down, ln_ffn_g, ln_ffn_b):
    return _forward(x, ln_in_g, ln_in_b, w_in_first, w_in_rest, mu_first, mu_rest, rel_bias,
                    lambda_q1, lambda_k1, lambda_q2, lambda_k2, subln_g,
                    rw_w0, rw_w_up, rw_a0, rw_a_up, rw_g_up, rw_v0, rw_v_up,
                    rw_k_k, rw_k_a, rw_r_k, rw_gn_g, rw_gn_b,
                    w_out, ln_mix_g, ln_mix_b, w_up, w_down, ln_ffn_g, ln_ffn_b)
```

```python
import functools
import math

import jax
import jax.numpy as jnp
from jax import lax
from jax.experimental import pallas as pl
from jax.experimental.pallas import tpu as pltpu

F32 = jnp.float32
BF16 = jnp.bfloat16

D_MODEL = 1024
DEPTH = 2
D_DIFF = 512
D_RWKV = 512
DIFF_HEADS = 4
DIFF_HEAD_DIM = 64
DIFF_V_DIM = 128
RWKV_HEAD = 64
LORA_W = 64
LORA_A = 64
LORA_V = 32
LORA_G = 160
D_FF = 4 * D_MODEL
N_BUCKETS = 32
MAX_DISTANCE = 128
LN_EPS = 1e-5
SUBLN_EPS = 1e-5
GN_EPS = 64e-5
ALPHA = (2 * DEPTH) ** 0.25
N_DIFF = 3 * D_DIFF
RW_W = 3 * D_RWKV

LANES = 128
MXU_W = 256
VMEM_LIMIT = 56 * 1024 * 1024

RW_PAD = 2048
ATT_T = 512
RW_TB = 512
RW_L = 64
RW_UNROLL = 4
NEG = -1e30


def _cparams(sem):
    return pltpu.CompilerParams(dimension_semantics=sem, vmem_limit_bytes=VMEM_LIMIT)


def _layer_norm(z, g, b):
    mu = jnp.mean(z, -1, keepdims=True)
    zc = z - mu
    var = jnp.mean(zc * zc, -1, keepdims=True)
    return zc * lax.rsqrt(var + LN_EPS) * g + b


def _proj_body(*refs, with_ln, tn, tiles_per_seq):
    if with_ln:
        x_ref, g_ref, b_ref, w_ref, mu_ref, xn_ref, qkv_ref, pr_ref, carry_sc = refs
        xn = _layer_norm(x_ref[...], g_ref[...], b_ref[...])
        xn_ref[...] = xn
    else:
        x_ref, w_ref, mu_ref, qkv_ref, pr_ref, carry_sc = refs
        xn = x_ref[...]
    tm = x_ref.shape[0]
    xb = xn.astype(BF16)
    seq_start = (pl.program_id(0) % tiles_per_seq) == 0
    row8 = lax.broadcasted_iota(jnp.int32, (8, 1), 0)
    for j in range(w_ref.shape[1] // tn):
        o = jnp.dot(xb, w_ref[:, j * tn:(j + 1) * tn], preferred_element_type=F32)
        if (j + 1) * tn <= N_DIFF:
            qkv_ref[:, j * tn:(j + 1) * tn] = o.astype(BF16)
        else:
            cols = slice(j * tn - N_DIFF, (j + 1) * tn - N_DIFF)
            carry = jnp.where(seq_start, 0.0, carry_sc[:, cols])
            rolled = pltpu.roll(o, 1, 0)
            prev = jnp.concatenate([jnp.where(row8 == 0, carry, rolled[:8]), rolled[8:]], axis=0)
            carry_sc[:, cols] = o[tm - 1:tm, :]
            pr_ref[:, cols] = o + (prev - o) * mu_ref[:, cols]


def _in_proj(x, w, mu, ln, seq, tm=512, tn=512):
    t, d = x.shape
    n = w.shape[1]
    with_ln = ln is not None
    row = lambda wd: pl.BlockSpec((tm, wd), lambda i: (i, 0))
    const = lambda r, wd: pl.BlockSpec((r, wd), lambda i: (0, 0))
    in_specs = ([row(d)] + ([const(1, d), const(1, d)] if with_ln else [])
                + [const(d, n), const(1, n - N_DIFF)])
    args = ([x] + ([ln[0].reshape(1, d), ln[1].reshape(1, d)] if with_ln else [])
            + [w, mu.reshape(1, n - N_DIFF)])
    out_shape = [jax.ShapeDtypeStruct((t, N_DIFF), BF16), jax.ShapeDtypeStruct((t, n - N_DIFF), F32)]
    out_specs = [row(N_DIFF), row(n - N_DIFF)]
    if with_ln:
        out_shape = [jax.ShapeDtypeStruct((t, d), F32)] + out_shape
        out_specs = [row(d)] + out_specs
    return pl.pallas_call(
        functools.partial(_proj_body, with_ln=with_ln, tn=tn, tiles_per_seq=seq // tm),
        grid=(t // tm,),
        in_specs=in_specs,
        out_specs=out_specs,
        out_shape=out_shape,
        scratch_shapes=[pltpu.VMEM((1, n - N_DIFF), F32)],
        compiler_params=_cparams(("arbitrary",)),
        name="ln_in_proj" if with_ln else "in_proj",
    )(*args)


def _bias_body(rb_ref, o_ref, *, t):
    which = pl.program_id(0)
    i = lax.broadcasted_iota(jnp.int32, (t, t), 0)
    j = lax.broadcasted_iota(jnp.int32, (t, t), 1)
    n = jnp.maximum(which * t + i - j, 0)
    max_exact = N_BUCKETS // 2
    n_log = N_BUCKETS - max_exact
    steps = [math.ceil(max_exact * (MAX_DISTANCE / max_exact) ** (k / n_log))
             for k in range(1, n_log)]
    bucket = jnp.minimum(n, max_exact)
    for thr in steps:
        bucket = bucket + jnp.where(n >= thr, 1, 0)
    for h in range(DIFF_HEADS):
        acc = jnp.zeros((t, t), F32)
        for b in range(N_BUCKETS):
            acc = jnp.where(bucket == b, rb_ref[b, h], acc)
        o_ref[h, 0] = acc


def _bias_tiles(rel_bias, t):
    return pl.pallas_call(
        functools.partial(_bias_body, t=t),
        grid=(2,),
        in_specs=[pl.BlockSpec(memory_space=pltpu.SMEM)],
        out_specs=pl.BlockSpec((DIFF_HEADS, 1, t, t), lambda w: (0, w, 0, 0)),
        out_shape=jax.ShapeDtypeStruct((DIFF_HEADS, 2, t, t), F32),
        compiler_params=_cparams(("parallel",)),
        name="t5_bias_tiles",
    )(rel_bias)


def _attn_body(rb_ref, lamp_ref, q_ref, k_ref, v_ref, bias_ref, g_ref, o_ref,
               m_sc, acc_sc, *, t, nq, lam_init):
    h = pl.program_id(1)
    lane = lax.broadcasted_iota(jnp.int32, (t, LANES), 1)
    c_far = rb_ref[N_BUCKETS - 1, h]
    lp = lamp_ref[...]
    lam = (jnp.exp(jnp.sum(lp[0:1] * lp[1:2], -1, keepdims=True))
           - jnp.exp(jnp.sum(lp[2:3] * lp[3:4], -1, keepdims=True)) + lam_init)

    def q_tile(iq):
        qrows = slice(iq * t, (iq + 1) * t)
        q = q_ref[qrows, :].astype(F32) * (DIFF_HEAD_DIM ** -0.5)
        qm = (jnp.where(lane < DIFF_HEAD_DIM, q, 0.0).astype(BF16),
              jnp.where(lane < DIFF_HEAD_DIM, 0.0, q).astype(BF16))
        m_sc[...] = jnp.full(m_sc.shape, NEG, F32)
        acc_sc[...] = jnp.zeros(acc_sc.shape, F32)

        def tile(row0, width, bias, diag_col0):
            rows = slice(row0, row0 + width)
            kt = k_ref[rows, :]
            vt = jnp.concatenate([v_ref[rows, :], jnp.ones((width, LANES), BF16)], axis=1)
            maps = range(2)
            s = [lax.dot_general(qm[m], kt, (((1,), (1,)), ((), ())), preferred_element_type=F32)
                 for m in maps]
            if bias is None:
                m_cur = [jnp.max(s[m], -1, keepdims=True) + c_far for m in maps]
            else:
                s = [s[m] + bias for m in maps]
                if diag_col0 is not None:
                    ri = lax.broadcasted_iota(jnp.int32, (t, width), 0)
                    ci = lax.broadcasted_iota(jnp.int32, (t, width), 1)
                    s = [jnp.where(ri >= ci - diag_col0, s[m], NEG) for m in maps]
                m_cur = [jnp.max(s[m], -1, keepdims=True) for m in maps]
            m_old = [m_sc[m] for m in maps]
            m_new = [jnp.maximum(m_old[m], m_cur[m]) for m in maps]
            a = [jnp.exp(m_old[m] - m_new[m]) for m in maps]
            sub = [m_new[m] - c_far if bias is None else m_new[m] for m in maps]
            p = [jnp.exp(s[m] - jnp.concatenate([sub[m]] * (width // LANES), axis=1)).astype(BF16)
                 for m in maps]
            pv = [jnp.dot(p[m], vt, preferred_element_type=F32) for m in maps]
            for m in maps:
                acc_sc[m] = jnp.concatenate([a[m], a[m]], axis=1) * acc_sc[m] + pv[m]
                m_sc[m] = m_new[m]

        n_far = max(iq - 1, 0)
        for j in range(n_far // 2):
            tile(j * 2 * t, 2 * t, None, None)
        if n_far % 2 == 1:
            tile((iq - 2) * t, 3 * t,
                 jnp.concatenate([jnp.full((t, t), c_far, F32), bias_ref[0, 1], bias_ref[0, 0]],
                                 axis=1), 2 * t)
        elif iq > 0:
            tile((iq - 1) * t, 2 * t,
                 jnp.concatenate([bias_ref[0, 1], bias_ref[0, 0]], axis=1), t)
        else:
            tile(0, t, bias_ref[0, 0], 0)

        a0 = acc_sc[0]
        a1 = acc_sc[1]
        out = a0[:, :LANES] / a0[:, LANES:] - lam * (a1[:, :LANES] / a1[:, LANES:])
        out = out * lax.rsqrt(jnp.mean(out * out, -1, keepdims=True) + SUBLN_EPS) * g_ref[...]
        o_ref[qrows, :] = (out * (1.0 - lam_init)).astype(o_ref.dtype)

    for iq in range(nq):
        pl.when(h >= 0)(functools.partial(q_tile, iq))


def _diff_attention(qkv, bias_tiles, rel_bias, lam_params, subln_g, lam_init, batch, seq):
    t = ATT_T
    hb = D_DIFF // LANES
    return pl.pallas_call(
        functools.partial(_attn_body, t=t, nq=seq // t, lam_init=lam_init),
        grid=(batch, DIFF_HEADS),
        in_specs=[pl.BlockSpec(memory_space=pltpu.SMEM),
                  pl.BlockSpec((4, DIFF_HEAD_DIM), lambda b, h: (0, 0)),
                  pl.BlockSpec((seq, LANES), lambda b, h: (b, h)),
                  pl.BlockSpec((seq, LANES), lambda b, h: (b, hb + h)),
                  pl.BlockSpec((seq, LANES), lambda b, h: (b, 2 * hb + h)),
                  pl.BlockSpec((1, 2, t, t), lambda b, h: (h, 0, 0, 0)),
                  pl.BlockSpec((1, DIFF_V_DIM), lambda b, h: (0, 0))],
        out_specs=pl.BlockSpec((seq, LANES), lambda b, h: (b, h)),
        out_shape=jax.ShapeDtypeStruct((batch * seq, D_DIFF), BF16),
        scratch_shapes=[pltpu.VMEM((2, t, LANES), F32),
                        pltpu.VMEM((2, t, DIFF_V_DIM + LANES), F32)],
        compiler_params=_cparams(("parallel", "parallel")),
        name="diff_attention",
    )(rel_bias, lam_params, qkv, qkv, qkv, bias_tiles, subln_g.reshape(1, DIFF_V_DIM))


def _split_dot(x, m):
    hi = x.astype(BF16)
    lo = (x - hi.astype(F32)).astype(BF16)
    return (jnp.dot(hi, m, preferred_element_type=F32)
            + jnp.dot(lo, m, preferred_element_type=F32))


def _stack_heads(x):
    lane = lax.broadcasted_iota(jnp.int32, x.shape, 1)
    return jnp.concatenate([jnp.where(lane < RWKV_HEAD, x, 0.0),
                            jnp.where(lane < RWKV_HEAD, 0.0, x)], axis=0)


def _bdot(a, b):
    return jnp.dot(a.astype(BF16), b.astype(BF16), preferred_element_type=F32)


def _rwkv_body(*refs, tb, first):
    if first:
        (pr_ref, w0_ref, a0_ref, kk_ref, ka_ref, rk_ref, gng_ref, gnb_ref,
         wup_ref, aup_ref, gup1_ref, gup2_ref,
         y_ref, vout_ref,
         h_sc, r_sc, k_sc, v_sc, as_sc, bs_sc, ci_sc, lw_sc, g_sc, y_sc) = refs
    else:
        (pr_ref, vf_ref, w0_ref, a0_ref, v0_ref, kk_ref, ka_ref, rk_ref, gng_ref, gnb_ref,
         wup_ref, aup_ref, gup1_ref, gup2_ref, vup_ref,
         y_ref,
         h_sc, r_sc, k_sc, v_sc, as_sc, bs_sc, ci_sc, lw_sc, g_sc, y_sc) = refs
    L = RW_L
    c = D_RWKV
    grp_rows = L * RW_UNROLL
    assert grp_rows == MXU_W and tb % grp_rows == 0

    @pl.when(pl.program_id(1) == 0)
    def _():
        h_sc[...] = jnp.zeros(h_sc.shape, F32)

    bi = lax.broadcasted_iota(jnp.int32, (MXU_W, MXU_W), 0)
    bj = lax.broadcasted_iota(jnp.int32, (MXU_W, MXU_W), 1)
    hs = RWKV_HEAD.bit_length() - 1
    head_ones = jnp.where((bi >> hs) == (bj >> hs), 1.0, 0.0).astype(BF16)
    ls_ = L.bit_length() - 1
    tri = jnp.where(((bi >> ls_) == (bj >> ls_)) & (bj <= bi), 1.0, 0.0).astype(BF16)

    def head_sum(x, split=True):
        part = _split_dot if split else _bdot
        return jnp.concatenate([part(x[:, i:i + MXU_W], head_ones)
                                for i in range(0, c, MXU_W)], axis=1)

    def prep(r0):
        rows = slice(r0, r0 + MXU_W)
        r = pr_ref[rows, 0:c]
        k = pr_ref[rows, c:2 * c]
        v = pr_ref[rows, 2 * c:3 * c]
        t1 = pr_ref[rows, RW_W:RW_W + LANES]
        t2 = pr_ref[rows, RW_W + LANES:RW_W + 2 * LANES]
        t3 = pr_ref[rows, RW_W + 2 * LANES:RW_W + 3 * LANES]

        lw = _bdot(jnp.tanh(t1), wup_ref[...])
        la = _bdot(t1, aup_ref[...])
        g_sc[rows, :] = (_bdot(jax.nn.sigmoid(t2), gup1_ref[...])
                         + _bdot(jax.nn.sigmoid(t3), gup2_ref[...]))
        yield

        z = -(w0_ref[...] + lw)
        softplus = jnp.maximum(z, 0.0) + jnp.log(1.0 + jnp.exp(-jnp.abs(z)))
        logw = -jnp.exp(-softplus - 0.5)
        yield
        a = jax.nn.sigmoid(a0_ref[...] + la)
        if first:
            vout_ref[rows, :] = v
        else:
            v = v + (vf_ref[rows, :] - v) * jax.nn.sigmoid(v0_ref[...] + _bdot(t3, vup_ref[...]))
        yield

        kk = k * kk_ref[...]
        kk = kk * jnp.minimum(lax.rsqrt(head_sum(kk * kk)), 1e12)
        yield

        hi = logw.astype(BF16)
        lo = (logw - hi.astype(F32)).astype(BF16)
        ci_sc[rows, :] = (jnp.dot(tri, hi, preferred_element_type=F32)
                          + jnp.dot(tri, lo, preferred_element_type=F32))
        yield
        r_sc[rows, :] = r
        k_sc[rows, :] = k * (1.0 + (a - 1.0) * ka_ref[...])
        v_sc[rows, :] = v
        as_sc[rows, :] = -kk
        bs_sc[rows, :] = kk * a
        lw_sc[rows, :] = logw
        yield

    def post(r0):
        rows = slice(r0, r0 + MXU_W)
        y = y_sc[rows, :]
        mean = head_sum(y) * (1.0 / RWKV_HEAD)
        yield
        yc = y - mean
        var = head_sum(yc * yc, split=False) * (1.0 / RWKV_HEAD)
        yield
        yn = yc * lax.rsqrt(var + GN_EPS) * gng_ref[...] + gnb_ref[...]
        bonus = head_sum(r_sc[rows, :] * k_sc[rows, :] * rk_ref[...], split=False) * v_sc[rows, :]
        yield
        y_ref[rows, :] = ((yn + bonus) * g_sc[rows, :]).astype(y_ref.dtype)
        yield

    i2 = lax.broadcasted_iota(jnp.int32, (4 * L, 4 * L), 0)
    j2 = lax.broadcasted_iota(jnp.int32, (4 * L, 4 * L), 1)
    incl = jnp.where(i2 >= 2 * L, 1, 0)
    amask = ((((i2 >> ls_) & 1) == ((j2 >> ls_) & 1))
             & ((j2 & (L - 1)) < (i2 & (L - 1)) + incl))
    e_i = lax.broadcasted_iota(jnp.int32, (2 * L, 2 * L), 0)
    e_j = lax.broadcasted_iota(jnp.int32, (2 * L, 2 * L), 1)
    eye = e_i == e_j
    eye_f = jnp.where(eye, 1.0, 0.0)

    pairs = range(c // LANES)
    lsl = [slice(p * LANES, (p + 1) * LANES) for p in pairs]
    n_sq = int(math.log2(L)) - 2

    def group_setup(grp, out):
        cins = []
        for j in range(RW_UNROLL):
            ch = grp * RW_UNROLL + j
            rows = slice(ch * L, (ch + 1) * L)
            ci_c = ci_sc[rows, :]
            last = ci_sc[ch * L + L - 1:ch * L + L, :]
            e_neg = jnp.exp(-ci_c)
            e_end = jnp.exp(last - ci_c)
            bs_c = bs_sc[rows, :]
            k_c = k_sc[rows, :]
            cins.append(dict(
                rows=rows, g_end=jnp.exp(last),
                at=as_sc[rows, :] * jnp.exp(ci_c - lw_sc[rows, :]),
                bt=bs_c * e_neg, kt=k_c * e_neg, rt=r_sc[rows, :] * jnp.exp(ci_c),
                bh=bs_c * e_end, kh=k_c * e_end, v=v_sc[rows, :]))
            yield
        combos = [(j, p) for j in range(RW_UNROLL) for p in pairs]
        at_s = [_stack_heads(cins[j]["at"][:, lsl[p]]) for j, p in combos]
        rt_s = [_stack_heads(cins[j]["rt"][:, lsl[p]]) for j, p in combos]
        v_s = [_stack_heads(cins[j]["v"][:, lsl[p]]) for j, p in combos]
        a_all = []
        for i, (j, p) in enumerate(combos):
            lhs = jnp.concatenate([at_s[i], rt_s[i]], axis=0).astype(BF16)
            rhs = jnp.concatenate([_stack_heads(cins[j]["bt"][:, lsl[p]]),
                                   _stack_heads(cins[j]["kt"][:, lsl[p]])], axis=0).astype(BF16)
            a = lax.dot_general(lhs, rhs, (((1,), (1,)), ((), ())), preferred_element_type=F32)
            a_all.append(jnp.where(amask, a, 0.0))
        yield
        ids = range(len(combos))
        a_ab = [a[:2 * L, :2 * L] for a in a_all]
        a_r = [a[2 * L:, :] for a in a_all]
        pw = [_bdot(a, a) for a in a_ab]
        tinv = [eye_f + a for a in a_ab]
        akv = [_bdot(a_all[i][:2 * L, 2 * L:], v_s[i]) for i in ids]
        yield
        for _ in range(n_sq):
            for i in ids:
                pw_b = pw[i].astype(BF16)
                both = jnp.dot(jnp.concatenate([pw_b, tinv[i].astype(BF16)], axis=0), pw_b,
                               preferred_element_type=F32)
                pw[i] = both[:2 * L]
                tinv[i] = tinv[i] + both[2 * L:]
            yield
        tinv = [tinv[i] + _bdot(tinv[i], pw[i]) for i in ids]
        yield
        wu = [_bdot(tinv[i], jnp.concatenate([at_s[i], akv[i]], axis=1)) for i in ids]
        yield
        bk_t = [jnp.concatenate([_stack_heads(cins[j]["bh"][:, lsl[p]]),
                                 _stack_heads(cins[j]["kh"][:, lsl[p]])], axis=0).T.astype(BF16)
                for j, p in combos]
        out.update(cins=cins, wu=wu, rt_s=rt_s, v_s=v_s, a_r=a_r, bk_t=bk_t)
        yield

    def group_state(setup):
        cins, wu, rt_s, v_s, a_r, bk_t = (setup[k] for k in
                                          ("cins", "wu", "rt_s", "v_s", "a_r", "bk_t"))
        h = [h_sc[p] for p in pairs]
        for j in range(RW_UNROLL):
            sel = [j * len(pairs) + p for p in pairs]
            wrg = [_bdot(jnp.concatenate(
                [wu[i][:, :2 * L], rt_s[i], jnp.where(eye, cins[j]["g_end"][:, lsl[p]], 0.0)],
                axis=0), h[p]) for p, i in zip(pairs, sel)]
            yield
            uv = [jnp.concatenate([wrg[p][:2 * L] + wu[i][:, 2 * L:], v_s[i]],
                                  axis=0).astype(BF16) for p, i in zip(pairs, sel)]
            for p, i in zip(pairs, sel):
                y_st = wrg[p][2 * L:4 * L] + jnp.dot(a_r[i].astype(BF16), uv[p],
                                                     preferred_element_type=F32)
                y_sc[cins[j]["rows"], lsl[p]] = y_st[:L] + y_st[L:]
            h = [wrg[p][4 * L:] + jnp.dot(bk_t[i], uv[p], preferred_element_type=F32)
                 for p, i in zip(pairs, sel)]
            yield
        for p in pairs:
            h_sc[p] = h[p]
        yield

    def interleave(*gens):
        gens = list(gens)
        while gens:
            for gen in list(gens):
                try:
                    next(gen)
                except StopIteration:
                    gens.remove(gen)

    n_grp = tb // grp_rows
    setups = [dict() for _ in range(n_grp)]
    interleave(prep(0))
    interleave(group_setup(0, setups[0]), *([prep(grp_rows)] if n_grp > 1 else []))
    for grp in range(n_grp):
        side = []
        if grp + 1 < n_grp:
            side.append(group_setup(grp + 1, setups[grp + 1]))
        if grp + 2 < n_grp:
            side.append(prep((grp + 2) * grp_rows))
        if grp > 0:
            side.append(post((grp - 1) * grp_rows))
        interleave(group_state(setups[grp]), *side)
    interleave(post((n_grp - 1) * grp_rows))


def _rwkv_mix(pr, v_first, prm, batch, seq):
    tb = RW_TB
    nt = seq // tb
    first = v_first is None
    c = D_RWKV
    row_spec = lambda w: pl.BlockSpec((tb, w), lambda b, t: (b * nt + t, 0))
    vec_spec = lambda w: pl.BlockSpec((1, w), lambda b, t: (0, 0))
    mat_spec = lambda: pl.BlockSpec((LANES, c), lambda b, t: (0, 0))
    vec = lambda x: x.reshape(1, -1)
    if first:
        args = [pr, vec(prm["w0"]), vec(prm["a0"]), vec(prm["k_k"]), vec(prm["k_a"]),
                vec(prm["r_k"]), vec(prm["gn_g"]), vec(prm["gn_b"]),
                prm["w_up"], prm["a_up"], prm["g_up1"], prm["g_up2"]]
        in_specs = [row_spec(RW_PAD)] + [vec_spec(c)] * 7 + [mat_spec()] * 4
        out_shape = [jax.ShapeDtypeStruct((batch * seq, c), BF16),
                     jax.ShapeDtypeStruct((batch * seq, c), F32)]
        out_specs = [row_spec(c), row_spec(c)]
    else:
        args = [pr, v_first, vec(prm["w0"]), vec(prm["a0"]), vec(prm["v0"]),
                vec(prm["k_k"]), vec(prm["k_a"]), vec(prm["r_k"]), vec(prm["gn_g"]), vec(prm["gn_b"]),
                prm["w_up"], prm["a_up"], prm["g_up1"], prm["g_up2"], prm["v_up"]]
        in_specs = [row_spec(RW_PAD), row_spec(c)] + [vec_spec(c)] * 8 + [mat_spec()] * 5
        out_shape = jax.ShapeDtypeStruct((batch * seq, c), BF16)
        out_specs = row_spec(c)
    scratch = [pltpu.VMEM((c // LANES, LANES, LANES), F32)]
    scratch += [pltpu.VMEM((tb, c), F32)] * 9
    return pl.pallas_call(
        functools.partial(_rwkv_body, tb=tb, first=first),
        grid=(batch, nt),
        in_specs=in_specs,
        out_specs=out_specs,
        out_shape=out_shape,
        scratch_shapes=scratch,
        compiler_params=_cparams(("parallel", "arbitrary")),
        name="rwkv7_mix_first" if first else "rwkv7_mix_rest",
    )(*args)


def _tail_body(yd_ref, yr_ref, x_ref, wo_ref, gm_ref, bm_ref, wu_ref, wd_ref, gf_ref, bf_ref,
               o_ref, z_sc, *, tf):
    s = pl.program_id(0)

    @pl.when(s == 0)
    def _():
        z_sc[...] = jnp.zeros(z_sc.shape, F32)

    slot = s % 2
    n_f = wu_ref.shape[1] // tf
    qr = x_ref.shape[0] // n_f

    tm = x_ref.shape[0]
    halves = (slice(0, tm // 2), slice(tm // 2, tm))
    mixh = [jnp.dot(yd_ref[r, :], wo_ref[:D_DIFF, :], preferred_element_type=F32)
            + jnp.dot(yr_ref[r, :], wo_ref[D_DIFF:, :], preferred_element_type=F32)
            for r in halves]
    x1h = [_layer_norm(ALPHA * x_ref[r, :] + m, gm_ref[...], bm_ref[...])
           for r, m in zip(halves, mixh)]
    h0 = [jnp.dot(x.astype(BF16), wu_ref[:, :tf], preferred_element_type=F32) for x in x1h]
    x1 = jnp.concatenate(x1h, axis=0)
    xb = x1.astype(BF16)
    acc = ALPHA * x1
    pending = []
    for i in range(n_f):
        f = i * tf
        floor = 0.0
        if pending:
            tie = sum(jnp.sum(p, axis=0, keepdims=True) for p in pending)
            floor = jnp.where(tie != tie, 1.0, 0.0)
            pending = []
        up = (jnp.concatenate(h0, axis=0) if i == 0
              else jnp.dot(xb, wu_ref[:, f:f + tf], preferred_element_type=F32))
        h = jnp.maximum(up, floor)
        acc = acc + jnp.dot((h * h).astype(BF16), wd_ref[f:f + tf, :], preferred_element_type=F32)
        for p in range(n_f):
            if min(p, n_f - 2) == i:
                rows = slice(p * qr, (p + 1) * qr)
                prev = _layer_norm(z_sc[1 - slot, rows, :], gf_ref[...], bf_ref[...])
                o_ref[rows, :] = prev
                pending.append(prev)
    z_sc[slot] = acc


def _layer_tail(yd, yr, x, wo, gm, bm, wu, wd, gf, bf, tm=512, tf=1024):
    t, d = x.shape
    ff = wu.shape[1]
    n = t // tm
    row_in = lambda wd_: pl.BlockSpec((tm, wd_), lambda s: (jnp.minimum(s, n - 1), 0))
    const = lambda r, c: pl.BlockSpec((r, c), lambda s: (0, 0), pipeline_mode=pl.Buffered(1))
    vec = lambda v: v.reshape(1, d)
    return pl.pallas_call(
        functools.partial(_tail_body, tf=tf),
        grid=(n + 1,),
        in_specs=[row_in(D_DIFF), row_in(D_RWKV), row_in(d), const(D_DIFF + D_RWKV, d),
                  const(1, d), const(1, d), const(d, ff), const(ff, d), const(1, d), const(1, d)],
        out_specs=pl.BlockSpec((tm, d), lambda s: (jnp.maximum(s - 1, 0), 0)),
        out_shape=jax.ShapeDtypeStruct((t, d), F32),
        scratch_shapes=[pltpu.VMEM((2, tm, d), F32)],
        compiler_params=_cparams(("arbitrary",)),
        name="out_proj_mlp_norm",
    )(yd, yr, x, wo, vec(gm), vec(bm), wu, wd, vec(gf), vec(bf))


def _pad_rows(w, before, total):
    return jnp.pad(w, ((before, total - before - w.shape[0]), (0, 0)))


def _rwkv_params(l, rw_w0, rw_w_up, rw_a0, rw_a_up, rw_g_up, rw_v0, rw_v_up,
                 rw_k_k, rw_k_a, rw_r_k, rw_gn_g, rw_gn_b):
    prm = {
        "w0": rw_w0[l], "a0": rw_a0[l], "k_k": rw_k_k[l], "k_a": rw_k_a[l],
        "r_k": rw_r_k[l].reshape(-1), "gn_g": rw_gn_g[l], "gn_b": rw_gn_b[l],
        "w_up": _pad_rows(rw_w_up[l], 0, LANES).astype(BF16),
        "a_up": _pad_rows(rw_a_up[l], LORA_W, LANES).astype(BF16),
        "g_up1": rw_g_up[l][:LANES].astype(BF16),
        "g_up2": _pad_rows(rw_g_up[l][LANES:], 0, LANES).astype(BF16),
    }
    if l > 0:
        prm["v0"] = rw_v0[l - 1]
        prm["v_up"] = _pad_rows(rw_v_up[l - 1], LORA_G - LANES, LANES).astype(BF16)
    return prm


def _forward(x, ln_in_g, ln_in_b, w_in_first, w_in_rest, mu_first, mu_rest, rel_bias,
             lambda_q1, lambda_k1, lambda_q2, lambda_k2, subln_g,
             rw_w0, rw_w_up, rw_a0, rw_a_up, rw_g_up, rw_v0, rw_v_up,
             rw_k_k, rw_k_a, rw_r_k, rw_gn_g, rw_gn_b,
             w_out, ln_mix_g, ln_mix_b, w_up, w_down, ln_ffn_g, ln_ffn_b):
    batch, seq, d = x.shape
    xs = x.reshape(batch * seq, d)
    bias_tiles = _bias_tiles(rel_bias, ATT_T)
    v_first = None
    for l in range(DEPTH):
        w_in = w_in_first if l == 0 else w_in_rest[l - 1]
        mu = mu_first if l == 0 else mu_rest[l - 1]
        w_pad = jnp.pad(w_in, ((0, 0), (0, N_DIFF + RW_PAD - w_in.shape[1]))).astype(BF16)
        mu_pad = jnp.pad(mu, (0, RW_PAD - mu.shape[0]))
        if l == 0:
            xs, qkv, pr = _in_proj(xs, w_pad, mu_pad, (ln_in_g, ln_in_b), seq)
        else:
            qkv, pr = _in_proj(xs, w_pad, mu_pad, None, seq)

        lam_init = 0.8 - 0.6 * math.exp(-0.3 * l)
        lam_params = jnp.stack([lambda_q1[l], lambda_k1[l], lambda_q2[l], lambda_k2[l]])
        y_diff = _diff_attention(qkv, bias_tiles, rel_bias, lam_params, subln_g[l], lam_init, batch, seq)

        prm = _rwkv_params(l, rw_w0, rw_w_up, rw_a0, rw_a_up, rw_g_up, rw_v0, rw_v_up,
                           rw_k_k, rw_k_a, rw_r_k, rw_gn_g, rw_gn_b)
        if l == 0:
            y_rw, v_first = _rwkv_mix(pr, None, prm, batch, seq)
        else:
            y_rw = _rwkv_mix(pr, v_first, prm, batch, seq)

        xs = _layer_tail(y_diff, y_rw, xs, w_out[l].astype(BF16), ln_mix_g[l], ln_mix_b[l],
                         w_up[l].astype(BF16), w_down[l].astype(BF16), ln_ffn_g[l], ln_ffn_b[l])
    return xs.reshape(batch, seq, d)


def kernel(x, ln_in_g, ln_in_b, w_in_first, w_in_rest, mu_first, mu_rest, rel_bias, lambda_q1, lambda_k1, lambda_q2, lambda_k2, subln_g, rw_w0, rw_w_up, rw_a0, rw_a_up, rw_g_up, rw_v0, rw_v_up, rw_k_k, rw_k_a, rw_r_k, rw_gn_g, rw_gn_b, w_out, ln_mix_g, ln_mix_b, w_up, w_down, ln_ffn_g, ln_ffn_b):
    return _forward(x, ln_in_g, ln_in_b, w_in_first, w_in_rest, mu_first, mu_rest, rel_bias,
                    lambda_q1, lambda_k1, lambda_q2, lambda_k2, subln_g,
                    rw_w0, rw_w_up, rw_a0, rw_a_up, rw_g_up, rw_v0, rw_v_up,
                    rw_k_k, rw_k_a, rw_r_k, rw_gn_g, rw_gn_b,
                    w_out, ln_mix_g, ln_mix_b, w_up, w_down, ln_ffn_g, ln_ffn_b)
```

```python
import functools
import math

import jax
import jax.numpy as jnp
from jax import lax
from jax.experimental import pallas as pl
from jax.experimental.pallas import tpu as pltpu

F32 = jnp.float32
BF16 = jnp.bfloat16

D_MODEL = 1024
DEPTH = 2
D_DIFF = 512
D_RWKV = 512
DIFF_HEADS = 4
DIFF_HEAD_DIM = 64
DIFF_V_DIM = 128
RWKV_HEAD = 64
LORA_W = 64
LORA_A = 64
LORA_V = 32
LORA_G = 160
D_FF = 4 * D_MODEL
N_BUCKETS = 32
MAX_DISTANCE = 128
LN_EPS = 1e-5
SUBLN_EPS = 1e-5
GN_EPS = 64e-5
ALPHA = (2 * DEPTH) ** 0.25
N_DIFF = 3 * D_DIFF
RW_W = 3 * D_RWKV

LANES = 128
MXU_W = 256
VMEM_LIMIT = 56 * 1024 * 1024

RW_PAD = 2048
ATT_T = 512
RW_TB = 512
RW_L = 64
RW_UNROLL = 4
NEG = -1e30


def _cparams(sem):
    return pltpu.CompilerParams(dimension_semantics=sem, vmem_limit_bytes=VMEM_LIMIT)


def _layer_norm(z, g, b):
    mu = jnp.mean(z, -1, keepdims=True)
    zc = z - mu
    var = jnp.mean(zc * zc, -1, keepdims=True)
    return zc * lax.rsqrt(var + LN_EPS) * g + b


def _proj_body(*refs, with_ln, tn, tiles_per_seq):
    if with_ln:
        x_ref, g_ref, b_ref, w_ref, mu_ref, xn_ref, qkv_ref, pr_ref, carry_sc = refs
        xn = _layer_norm(x_ref[...], g_ref[...], b_ref[...])
        xn_ref[...] = xn
    else:
        x_ref, w_ref, mu_ref, qkv_ref, pr_ref, carry_sc = refs
        xn = x_ref[...]
    tm = x_ref.shape[0]
    xb = xn.astype(BF16)
    seq_start = (pl.program_id(0) % tiles_per_seq) == 0
    row8 = lax.broadcasted_iota(jnp.int32, (8, 1), 0)
    for j in range(w_ref.shape[1] // tn):
        o = jnp.dot(xb, w_ref[:, j * tn:(j + 1) * tn], preferred_element_type=F32)
        if (j + 1) * tn <= N_DIFF:
            qkv_ref[:, j * tn:(j + 1) * tn] = o.astype(BF16)
        else:
            cols = slice(j * tn - N_DIFF, (j + 1) * tn - N_DIFF)
            carry = jnp.where(seq_start, 0.0, carry_sc[:, cols])
            rolled = pltpu.roll(o, 1, 0)
            prev = jnp.concatenate([jnp.where(row8 == 0, carry, rolled[:8]), rolled[8:]], axis=0)
            carry_sc[:, cols] = o[tm - 1:tm, :]
            pr_ref[:, cols] = o + (prev - o) * mu_ref[:, cols]


def _in_proj(x, w, mu, ln, seq, tm=512, tn=512):
    t, d = x.shape
    n = w.shape[1]
    with_ln = ln is not None
    row = lambda wd: pl.BlockSpec((tm, wd), lambda i: (i, 0))
    const = lambda r, wd: pl.BlockSpec((r, wd), lambda i: (0, 0))
    in_specs = ([row(d)] + ([const(1, d), const(1, d)] if with_ln else [])
                + [const(d, n), const(1, n - N_DIFF)])
    args = ([x] + ([ln[0].reshape(1, d), ln[1].reshape(1, d)] if with_ln else [])
            + [w, mu.reshape(1, n - N_DIFF)])
    out_shape = [jax.ShapeDtypeStruct((t, N_DIFF), BF16), jax.ShapeDtypeStruct((t, n - N_DIFF), F32)]
    out_specs = [row(N_DIFF), row(n - N_DIFF)]
    if with_ln:
        out_shape = [jax.ShapeDtypeStruct((t, d), F32)] + out_shape
        out_specs = [row(d)] + out_specs
    return pl.pallas_call(
        functools.partial(_proj_body, with_ln=with_ln, tn=tn, tiles_per_seq=seq // tm),
        grid=(t // tm,),
        in_specs=in_specs,
        out_specs=out_specs,
        out_shape=out_shape,
        scratch_shapes=[pltpu.VMEM((1, n - N_DIFF), F32)],
        compiler_params=_cparams(("arbitrary",)),
        name="ln_in_proj" if with_ln else "in_proj",
    )(*args)


def _bias_body(rb_ref, o_ref, *, t):
    which = pl.program_id(0)
    i = lax.broadcasted_iota(jnp.int32, (t, t), 0)
    j = lax.broadcasted_iota(jnp.int32, (t, t), 1)
    n = jnp.maximum(which * t + i - j, 0)
    max_exact = N_BUCKETS // 2
    n_log = N_BUCKETS - max_exact
    steps = [math.ceil(max_exact * (MAX_DISTANCE / max_exact) ** (k / n_log))
             for k in range(1, n_log)]
    bucket = jnp.minimum(n, max_exact)
    for thr in steps:
        bucket = bucket + jnp.where(n >= thr, 1, 0)
    for h in range(DIFF_HEADS):
        acc = jnp.zeros((t, t), F32)
        for b in range(N_BUCKETS):
            acc = jnp.where(bucket == b, rb_ref[b, h], acc)
        o_ref[h, 0] = acc


def _bias_tiles(rel_bias, t):
    return pl.pallas_call(
        functools.partial(_bias_body, t=t),
        grid=(2,),
        in_specs=[pl.BlockSpec(memory_space=pltpu.SMEM)],
        out_specs=pl.BlockSpec((DIFF_HEADS, 1, t, t), lambda w: (0, w, 0, 0)),
        out_shape=jax.ShapeDtypeStruct((DIFF_HEADS, 2, t, t), F32),
        compiler_params=_cparams(("parallel",)),
        name="t5_bias_tiles",
    )(rel_bias)


def _attn_body(rb_ref, lamp_ref, q_ref, k_ref, v_ref, bias_ref, g_ref, o_ref,
               m_sc, acc_sc, *, t, nq, lam_init):
    h = pl.program_id(1)
    lane = lax.broadcasted_iota(jnp.int32, (t, LANES), 1)
    c_far = rb_ref[N_BUCKETS - 1, h]
    lp = lamp_ref[...]
    lam = (jnp.exp(jnp.sum(lp[0:1] * lp[1:2], -1, keepdims=True))
           - jnp.exp(jnp.sum(lp[2:3] * lp[3:4], -1, keepdims=True)) + lam_init)

    def q_tile(iq):
        qrows = slice(iq * t, (iq + 1) * t)
        q = q_ref[qrows, :].astype(F32) * (DIFF_HEAD_DIM ** -0.5)
        qm = (jnp.where(lane < DIFF_HEAD_DIM, q, 0.0).astype(BF16),
              jnp.where(lane < DIFF_HEAD_DIM, 0.0, q).astype(BF16))
        m_sc[...] = jnp.full(m_sc.shape, NEG, F32)
        acc_sc[...] = jnp.zeros(acc_sc.shape, F32)

        def tile(row0, width, bias, diag_col0):
            rows = slice(row0, row0 + width)
            kt = k_ref[rows, :]
            vt = jnp.concatenate([v_ref[rows, :], jnp.ones((width, LANES), BF16)], axis=1)
            maps = range(2)
            s = [lax.dot_general(qm[m], kt, (((1,), (1,)), ((), ())), preferred_element_type=F32)
                 for m in maps]
            if bias is None:
                m_cur = [jnp.max(s[m], -1, keepdims=True) + c_far for m in maps]
            else:
                s = [s[m] + bias for m in maps]
                if diag_col0 is not None:
                    ri = lax.broadcasted_iota(jnp.int32, (t, width), 0)
                    ci = lax.broadcasted_iota(jnp.int32, (t, width), 1)
                    s = [jnp.where(ri >= ci - diag_col0, s[m], NEG) for m in maps]
                m_cur = [jnp.max(s[m], -1, keepdims=True) for m in maps]
            m_old = [m_sc[m] for m in maps]
            m_new = [jnp.maximum(m_old[m], m_cur[m]) for m in maps]
            a = [jnp.exp(m_old[m] - m_new[m]) for m in maps]
            sub = [m_new[m] - c_far if bias is None else m_new[m] for m in maps]
            p = [jnp.exp(s[m] - jnp.concatenate([sub[m]] * (width // LANES), axis=1)).astype(BF16)
                 for m in maps]
            pv = [jnp.dot(p[m], vt, preferred_element_type=F32) for m in maps]
            for m in maps:
                acc_sc[m] = jnp.concatenate([a[m], a[m]], axis=1) * acc_sc[m] + pv[m]
                m_sc[m] = m_new[m]

        n_far = max(iq - 1, 0)
        row0 = 0
        for width in [4 * t] * (n_far // 4) + [2 * t] * (n_far % 4 // 2):
            tile(row0, width, None, None)
            row0 += width
        if n_far % 2 == 1:
            tile((iq - 2) * t, 3 * t,
                 jnp.concatenate([jnp.full((t, t), c_far, F32), bias_ref[0, 1], bias_ref[0, 0]],
                                 axis=1), 2 * t)
        elif iq > 0:
            tile((iq - 1) * t, 2 * t,
                 jnp.concatenate([bias_ref[0, 1], bias_ref[0, 0]], axis=1), t)
        else:
            tile(0, t, bias_ref[0, 0], 0)

        a0 = acc_sc[0]
        a1 = acc_sc[1]
        out = a0[:, :LANES] / a0[:, LANES:] - lam * (a1[:, :LANES] / a1[:, LANES:])
        out = out * lax.rsqrt(jnp.mean(out * out, -1, keepdims=True) + SUBLN_EPS) * g_ref[...]
        o_ref[qrows, :] = (out * (1.0 - lam_init)).astype(o_ref.dtype)

    for iq in range(nq):
        pl.when(h >= 0)(functools.partial(q_tile, iq))


def _diff_attention(qkv, bias_tiles, rel_bias, lam_params, subln_g, lam_init, batch, seq):
    t = ATT_T
    hb = D_DIFF // LANES
    return pl.pallas_call(
        functools.partial(_attn_body, t=t, nq=seq // t, lam_init=lam_init),
        grid=(batch, DIFF_HEADS),
        in_specs=[pl.BlockSpec(memory_space=pltpu.SMEM),
                  pl.BlockSpec((4, DIFF_HEAD_DIM), lambda b, h: (0, 0)),
                  pl.BlockSpec((seq, LANES), lambda b, h: (b, h)),
                  pl.BlockSpec((seq, LANES), lambda b, h: (b, hb + h)),
                  pl.BlockSpec((seq, LANES), lambda b, h: (b, 2 * hb + h)),
                  pl.BlockSpec((1, 2, t, t), lambda b, h: (h, 0, 0, 0)),
                  pl.BlockSpec((1, DIFF_V_DIM), lambda b, h: (0, 0))],
        out_specs=pl.BlockSpec((seq, LANES), lambda b, h: (b, h)),
        out_shape=jax.ShapeDtypeStruct((batch * seq, D_DIFF), BF16),
        scratch_shapes=[pltpu.VMEM((2, t, LANES), F32),
                        pltpu.VMEM((2, t, DIFF_V_DIM + LANES), F32)],
        compiler_params=_cparams(("parallel", "parallel")),
        name="diff_attention",
    )(rel_bias, lam_params, qkv, qkv, qkv, bias_tiles, subln_g.reshape(1, DIFF_V_DIM))


def _split_dot(x, m):
    hi = x.astype(BF16)
    lo = (x - hi.astype(F32)).astype(BF16)
    return (jnp.dot(hi, m, preferred_element_type=F32)
            + jnp.dot(lo, m, preferred_element_type=F32))


def _stack_heads(x):
    lane = lax.broadcasted_iota(jnp.int32, x.shape, 1)
    return jnp.concatenate([jnp.where(lane < RWKV_HEAD, x, 0.0),
                            jnp.where(lane < RWKV_HEAD, 0.0, x)], axis=0)


def _bdot(a, b):
    return jnp.dot(a.astype(BF16), b.astype(BF16), preferred_element_type=F32)


def _rwkv_body(*refs, tb, first):
    if first:
        (pr_ref, w0_ref, a0_ref, kk_ref, ka_ref, rk_ref, gng_ref, gnb_ref,
         wup_ref, aup_ref, gup1_ref, gup2_ref,
         y_ref, vout_ref,
         h_sc, r_sc, k_sc, v_sc, as_sc, bs_sc, ci_sc, lw_sc, g_sc, y_sc) = refs
    else:
        (pr_ref, vf_ref, w0_ref, a0_ref, v0_ref, kk_ref, ka_ref, rk_ref, gng_ref, gnb_ref,
         wup_ref, aup_ref, gup1_ref, gup2_ref, vup_ref,
         y_ref,
         h_sc, r_sc, k_sc, v_sc, as_sc, bs_sc, ci_sc, lw_sc, g_sc, y_sc) = refs
    L = RW_L
    c = D_RWKV
    grp_rows = L * RW_UNROLL
    assert grp_rows == MXU_W and tb % grp_rows == 0

    @pl.when(pl.program_id(1) == 0)
    def _():
        h_sc[...] = jnp.zeros(h_sc.shape, F32)

    bi = lax.broadcasted_iota(jnp.int32, (MXU_W, MXU_W), 0)
    bj = lax.broadcasted_iota(jnp.int32, (MXU_W, MXU_W), 1)
    hs = RWKV_HEAD.bit_length() - 1
    head_ones = jnp.where((bi >> hs) == (bj >> hs), 1.0, 0.0).astype(BF16)
    ls_ = L.bit_length() - 1
    tri = jnp.where(((bi >> ls_) == (bj >> ls_)) & (bj <= bi), 1.0, 0.0).astype(BF16)

    def head_sum(x, split=True):
        part = _split_dot if split else _bdot
        return jnp.concatenate([part(x[:, i:i + MXU_W], head_ones)
                                for i in range(0, c, MXU_W)], axis=1)

    def prep(r0):
        rows = slice(r0, r0 + MXU_W)
        r = pr_ref[rows, 0:c]
        k = pr_ref[rows, c:2 * c]
        v = pr_ref[rows, 2 * c:3 * c]
        t1 = pr_ref[rows, RW_W:RW_W + LANES]
        t2 = pr_ref[rows, RW_W + LANES:RW_W + 2 * LANES]
        t3 = pr_ref[rows, RW_W + 2 * LANES:RW_W + 3 * LANES]

        lw = _bdot(jnp.tanh(t1), wup_ref[...])
        la = _bdot(t1, aup_ref[...])
        g_sc[rows, :] = (_bdot(jax.nn.sigmoid(t2), gup1_ref[...])
                         + _bdot(jax.nn.sigmoid(t3), gup2_ref[...]))
        yield

        z = -(w0_ref[...] + lw)
        softplus = jnp.maximum(z, 0.0) + jnp.log(1.0 + jnp.exp(-jnp.abs(z)))
        logw = -jnp.exp(-softplus - 0.5)
        yield
        a = jax.nn.sigmoid(a0_ref[...] + la)
        if first:
            vout_ref[rows, :] = v
        else:
            v = v + (vf_ref[rows, :] - v) * jax.nn.sigmoid(v0_ref[...] + _bdot(t3, vup_ref[...]))
        yield

        kk = k * kk_ref[...]
        kk = kk * jnp.minimum(lax.rsqrt(head_sum(kk * kk)), 1e12)
        yield

        hi = logw.astype(BF16)
        lo = (logw - hi.astype(F32)).astype(BF16)
        ci_sc[rows, :] = (jnp.dot(tri, hi, preferred_element_type=F32)
                          + jnp.dot(tri, lo, preferred_element_type=F32))
        yield
        r_sc[rows, :] = r
        k_sc[rows, :] = k * (1.0 + (a - 1.0) * ka_ref[...])
        v_sc[rows, :] = v
        as_sc[rows, :] = -kk
        bs_sc[rows, :] = kk * a
        lw_sc[rows, :] = logw
        yield

    def post(r0):
        rows = slice(r0, r0 + MXU_W)
        y = y_sc[rows, :]
        mean = head_sum(y) * (1.0 / RWKV_HEAD)
        yield
        yc = y - mean
        var = head_sum(yc * yc, split=False) * (1.0 / RWKV_HEAD)
        yield
        yn = yc * lax.rsqrt(var + GN_EPS) * gng_ref[...] + gnb_ref[...]
        bonus = head_sum(r_sc[rows, :] * k_sc[rows, :] * rk_ref[...], split=False) * v_sc[rows, :]
        yield
        y_ref[rows, :] = ((yn + bonus) * g_sc[rows, :]).astype(y_ref.dtype)
        yield

    i2 = lax.broadcasted_iota(jnp.int32, (4 * L, 4 * L), 0)
    j2 = lax.broadcasted_iota(jnp.int32, (4 * L, 4 * L), 1)
    incl = jnp.where(i2 >= 2 * L, 1, 0)
    amask = ((((i2 >> ls_) & 1) == ((j2 >> ls_) & 1))
             & ((j2 & (L - 1)) < (i2 & (L - 1)) + incl))
    e_i = lax.broadcasted_iota(jnp.int32, (2 * L, 2 * L), 0)
    e_j = lax.broadcasted_iota(jnp.int32, (2 * L, 2 * L), 1)
    eye = e_i == e_j
    eye_f = jnp.where(eye, 1.0, 0.0)

    pairs = range(c // LANES)
    lsl = [slice(p * LANES, (p + 1) * LANES) for p in pairs]
    n_sq = int(math.log2(L)) - 2

    def group_setup(grp, out):
        cins = []
        for j in range(RW_UNROLL):
            ch = grp * RW_UNROLL + j
            rows = slice(ch * L, (ch + 1) * L)
            ci_c = ci_sc[rows, :]
            last = ci_sc[ch * L + L - 1:ch * L + L, :]
            e_neg = jnp.exp(-ci_c)
            e_end = jnp.exp(last - ci_c)
            bs_c = bs_sc[rows, :]
            k_c = k_sc[rows, :]
            cins.append(dict(
                rows=rows, g_end=jnp.exp(last),
                at=as_sc[rows, :] * jnp.exp(ci_c - lw_sc[rows, :]),
                bt=bs_c * e_neg, kt=k_c * e_neg, rt=r_sc[rows, :] * jnp.exp(ci_c),
                bh=bs_c * e_end, kh=k_c * e_end, v=v_sc[rows, :]))
            yield
        combos = [(j, p) for j in range(RW_UNROLL) for p in pairs]
        at_s = [_stack_heads(cins[j]["at"][:, lsl[p]]) for j, p in combos]
        rt_s = [_stack_heads(cins[j]["rt"][:, lsl[p]]) for j, p in combos]
        v_s = [_stack_heads(cins[j]["v"][:, lsl[p]]) for j, p in combos]
        a_all = []
        for i, (j, p) in enumerate(combos):
            lhs = jnp.concatenate([at_s[i], rt_s[i]], axis=0).astype(BF16)
            rhs = jnp.concatenate([_stack_heads(cins[j]["bt"][:, lsl[p]]),
                                   _stack_heads(cins[j]["kt"][:, lsl[p]])], axis=0).astype(BF16)
            a = lax.dot_general(lhs, rhs, (((1,), (1,)), ((), ())), preferred_element_type=F32)
            a_all.append(jnp.where(amask, a, 0.0))
        yield
        ids = range(len(combos))
        a_ab = [a[:2 * L, :2 * L] for a in a_all]
        a_r = [a[2 * L:, :] for a in a_all]
        pw = [_bdot(a, a) for a in a_ab]
        tinv = [eye_f + a for a in a_ab]
        akv = [_bdot(a_all[i][:2 * L, 2 * L:], v_s[i]) for i in ids]
        yield
        for _ in range(n_sq):
            for i in ids:
                pw_b = pw[i].astype(BF16)
                both = jnp.dot(jnp.concatenate([pw_b, tinv[i].astype(BF16)], axis=0), pw_b,
                               preferred_element_type=F32)
                pw[i] = both[:2 * L]
                tinv[i] = tinv[i] + both[2 * L:]
            yield
        tinv = [tinv[i] + _bdot(tinv[i], pw[i]) for i in ids]
        yield
        wu = [_bdot(tinv[i], jnp.concatenate([at_s[i], akv[i]], axis=1)) for i in ids]
        yield
        bk_t = [jnp.concatenate([_stack_heads(cins[j]["bh"][:, lsl[p]]),
                                 _stack_heads(cins[j]["kh"][:, lsl[p]])], axis=0).T.astype(BF16)
                for j, p in combos]
        out.update(cins=cins, wu=wu, rt_s=rt_s, v_s=v_s, a_r=a_r, bk_t=bk_t)
        yield

    def group_state(setup):
        cins, wu, rt_s, v_s, a_r, bk_t = (setup[k] for k in
                                          ("cins", "wu", "rt_s", "v_s", "a_r", "bk_t"))
        h = [h_sc[p] for p in pairs]
        for j in range(RW_UNROLL):
            sel = [j * len(pairs) + p for p in pairs]
            wrg = [_bdot(jnp.concatenate(
                [wu[i][:, :2 * L], rt_s[i], jnp.where(eye, cins[j]["g_end"][:, lsl[p]], 0.0)],
                axis=0), h[p]) for p, i in zip(pairs, sel)]
            yield
            uv = [jnp.concatenate([wrg[p][:2 * L] + wu[i][:, 2 * L:], v_s[i]],
                                  axis=0).astype(BF16) for p, i in zip(pairs, sel)]
            for p, i in zip(pairs, sel):
                y_st = wrg[p][2 * L:4 * L] + jnp.dot(a_r[i].astype(BF16), uv[p],
                                                     preferred_element_type=F32)
                y_sc[cins[j]["rows"], lsl[p]] = y_st[:L] + y_st[L:]
            h = [wrg[p][4 * L:] + jnp.dot(bk_t[i], uv[p], preferred_element_type=F32)
                 for p, i in zip(pairs, sel)]
            yield
        for p in pairs:
            h_sc[p] = h[p]
        yield

    def interleave(*gens):
        gens = list(gens)
        while gens:
            for gen in list(gens):
                try:
                    next(gen)
                except StopIteration:
                    gens.remove(gen)

    n_grp = tb // grp_rows
    setups = [dict() for _ in range(n_grp)]
    interleave(prep(0))
    interleave(group_setup(0, setups[0]), *([prep(grp_rows)] if n_grp > 1 else []))
    for grp in range(n_grp):
        side = []
        if grp + 1 < n_grp:
            side.append(group_setup(grp + 1, setups[grp + 1]))
        if grp + 2 < n_grp:
            side.append(prep((grp + 2) * grp_rows))
        if grp > 0:
            side.append(post((grp - 1) * grp_rows))
        interleave(group_state(setups[grp]), *side)
    interleave(post((n_grp - 1) * grp_rows))


def _rwkv_mix(pr, v_first, prm, batch, seq):
    tb = RW_TB
    nt = seq // tb
    first = v_first is None
    c = D_RWKV
    row_spec = lambda w: pl.BlockSpec((tb, w), lambda b, t: (b * nt + t, 0))
    vec_spec = lambda w: pl.BlockSpec((1, w), lambda b, t: (0, 0))
    mat_spec = lambda: pl.BlockSpec((LANES, c), lambda b, t: (0, 0))
    vec = lambda x: x.reshape(1, -1)
    if first:
        args = [pr, vec(prm["w0"]), vec(prm["a0"]), vec(prm["k_k"]), vec(prm["k_a"]),
                vec(prm["r_k"]), vec(prm["gn_g"]), vec(prm["gn_b"]),
                prm["w_up"], prm["a_up"], prm["g_up1"], prm["g_up2"]]
        in_specs = [row_spec(RW_PAD)] + [vec_spec(c)] * 7 + [mat_spec()] * 4
        out_shape = [jax.ShapeDtypeStruct((batch * seq, c), BF16),
                     jax.ShapeDtypeStruct((batch * seq, c), F32)]
        out_specs = [row_spec(c), row_spec(c)]
    else:
        args = [pr, v_first, vec(prm["w0"]), vec(prm["a0"]), vec(prm["v0"]),
                vec(prm["k_k"]), vec(prm["k_a"]), vec(prm["r_k"]), vec(prm["gn_g"]), vec(prm["gn_b"]),
                prm["w_up"], prm["a_up"], prm["g_up1"], prm["g_up2"], prm["v_up"]]
        in_specs = [row_spec(RW_PAD), row_spec(c)] + [vec_spec(c)] * 8 + [mat_spec()] * 5
        out_shape = jax.ShapeDtypeStruct((batch * seq, c), BF16)
        out_specs = row_spec(c)
    scratch = [pltpu.VMEM((c // LANES, LANES, LANES), F32)]
    scratch += [pltpu.VMEM((tb, c), F32)] * 9
    return pl.pallas_call(
        functools.partial(_rwkv_body, tb=tb, first=first),
        grid=(batch, nt),
        in_specs=in_specs,
        out_specs=out_specs,
        out_shape=out_shape,
        scratch_shapes=scratch,
        compiler_params=_cparams(("parallel", "arbitrary")),
        name="rwkv7_mix_first" if first else "rwkv7_mix_rest",
    )(*args)


def _tail_body(yd_ref, yr_ref, x_ref, wo_ref, gm_ref, bm_ref, wu_ref, wd_ref, gf_ref, bf_ref,
               o_ref, z_sc, *, tf):
    s = pl.program_id(0)

    @pl.when(s == 0)
    def _():
        z_sc[...] = jnp.zeros(z_sc.shape, F32)

    slot = s % 2
    n_f = wu_ref.shape[1] // tf
    qr = x_ref.shape[0] // n_f

    tm = x_ref.shape[0]
    halves = (slice(0, tm // 2), slice(tm // 2, tm))
    mixh = [jnp.dot(yd_ref[r, :], wo_ref[:D_DIFF, :], preferred_element_type=F32)
            + jnp.dot(yr_ref[r, :], wo_ref[D_DIFF:, :], preferred_element_type=F32)
            for r in halves]
    x1h = [_layer_norm(ALPHA * x_ref[r, :] + m, gm_ref[...], bm_ref[...])
           for r, m in zip(halves, mixh)]
    h0 = [jnp.dot(x.astype(BF16), wu_ref[:, :tf], preferred_element_type=F32) for x in x1h]
    x1 = jnp.concatenate(x1h, axis=0)
    xb = x1.astype(BF16)
    acc = ALPHA * x1
    pending = []
    for i in range(n_f):
        f = i * tf
        floor = 0.0
        if pending:
            tie = sum(jnp.sum(p, axis=0, keepdims=True) for p in pending)
            floor = jnp.where(tie != tie, 1.0, 0.0)
            pending = []
        up = (jnp.concatenate(h0, axis=0) if i == 0
              else jnp.dot(xb, wu_ref[:, f:f + tf], preferred_element_type=F32))
        h = jnp.maximum(up, floor)
        acc = acc + jnp.dot((h * h).astype(BF16), wd_ref[f:f + tf, :], preferred_element_type=F32)
        for p in range(n_f):
            if min(p, n_f - 2) == i:
                rows = slice(p * qr, (p + 1) * qr)
                prev = _layer_norm(z_sc[1 - slot, rows, :], gf_ref[...], bf_ref[...])
                o_ref[rows, :] = prev
                pending.append(prev)
    z_sc[slot] = acc


def _layer_tail(yd, yr, x, wo, gm, bm, wu, wd, gf, bf, tm=512, tf=1024):
    t, d = x.shape
    ff = wu.shape[1]
    n = t // tm
    row_in = lambda wd_: pl.BlockSpec((tm, wd_), lambda s: (jnp.minimum(s, n - 1), 0))
    const = lambda r, c: pl.BlockSpec((r, c), lambda s: (0, 0), pipeline_mode=pl.Buffered(1))
    vec = lambda v: v.reshape(1, d)
    return pl.pallas_call(
        functools.partial(_tail_body, tf=tf),
        grid=(n + 1,),
        in_specs=[row_in(D_DIFF), row_in(D_RWKV), row_in(d), const(D_DIFF + D_RWKV, d),
                  const(1, d), const(1, d), const(d, ff), const(ff, d), const(1, d), const(1, d)],
        out_specs=pl.BlockSpec((tm, d), lambda s: (jnp.maximum(s - 1, 0), 0)),
        out_shape=jax.ShapeDtypeStruct((t, d), F32),
        scratch_shapes=[pltpu.VMEM((2, tm, d), F32)],
        compiler_params=_cparams(("arbitrary",)),
        name="out_proj_mlp_norm",
    )(yd, yr, x, wo, vec(gm), vec(bm), wu, wd, vec(gf), vec(bf))


def _pad_rows(w, before, total):
    return jnp.pad(w, ((before, total - before - w.shape[0]), (0, 0)))


def _rwkv_params(l, rw_w0, rw_w_up, rw_a0, rw_a_up, rw_g_up, rw_v0, rw_v_up,
                 rw_k_k, rw_k_a, rw_r_k, rw_gn_g, rw_gn_b):
    prm = {
        "w0": rw_w0[l], "a0": rw_a0[l], "k_k": rw_k_k[l], "k_a": rw_k_a[l],
        "r_k": rw_r_k[l].reshape(-1), "gn_g": rw_gn_g[l], "gn_b": rw_gn_b[l],
        "w_up": _pad_rows(rw_w_up[l], 0, LANES).astype(BF16),
        "a_up": _pad_rows(rw_a_up[l], LORA_W, LANES).astype(BF16),
        "g_up1": rw_g_up[l][:LANES].astype(BF16),
        "g_up2": _pad_rows(rw_g_up[l][LANES:], 0, LANES).astype(BF16),
    }
    if l > 0:
        prm["v0"] = rw_v0[l - 1]
        prm["v_up"] = _pad_rows(rw_v_up[l - 1], LORA_G - LANES, LANES).astype(BF16)
    return prm


def _forward(x, ln_in_g, ln_in_b, w_in_first, w_in_rest, mu_first, mu_rest, rel_bias,
             lambda_q1, lambda_k1, lambda_q2, lambda_k2, subln_g,
             rw_w0, rw_w_up, rw_a0, rw_a_up, rw_g_up, rw_v0, rw_v_up,
             rw_k_k, rw_k_a, rw_r_k, rw_gn_g, rw_gn_b,
             w_out, ln_mix_g, ln_mix_b, w_up, w_down, ln_ffn_g, ln_ffn_b):
    batch, seq, d = x.shape
    xs = x.reshape(batch * seq, d)
    bias_tiles = _bias_tiles(rel_bias, ATT_T)
    v_first = None
    for l in range(DEPTH):
        w_in = w_in_first if l == 0 else w_in_rest[l - 1]
        mu = mu_first if l == 0 else mu_rest[l - 1]
        w_pad = jnp.pad(w_in, ((0, 0), (0, N_DIFF + RW_PAD - w_in.shape[1]))).astype(BF16)
        mu_pad = jnp.pad(mu, (0, RW_PAD - mu.shape[0]))
        if l == 0:
            xs, qkv, pr = _in_proj(xs, w_pad, mu_pad, (ln_in_g, ln_in_b), seq)
        else:
            qkv, pr = _in_proj(xs, w_pad, mu_pad, None, seq)

        lam_init = 0.8 - 0.6 * math.exp(-0.3 * l)
        lam_params = jnp.stack([lambda_q1[l], lambda_k1[l], lambda_q2[l], lambda_k2[l]])
        y_diff = _diff_attention(qkv, bias_tiles, rel_bias, lam_params, subln_g[l], lam_init, batch, seq)

        prm = _rwkv_params(l, rw_w0, rw_w_up, rw_a0, rw_a_up, rw_g_up, rw_v0, rw_v_up,
                           rw_k_k, rw_k_a, rw_r_k, rw_gn_g, rw_gn_b)
        if l == 0:
            y_rw, v_first = _rwkv_mix(pr, None, prm, batch, seq)
        else:
            y_rw = _rwkv_mix(pr, v_first, prm, batch, seq)

        xs = _layer_tail(y_diff, y_rw, xs, w_out[l].astype(BF16), ln_mix_g[l], ln_mix_b[l],
                         w_up[l].astype(BF16), w_down[l].astype(BF16), ln_ffn_g[l], ln_ffn_b[l])
    return xs.reshape(batch, seq, d)


def kernel(x, ln_in_g, ln_in_b, w_in_first, w_in_rest, mu_first, mu_rest, rel_bias, lambda_q1, lambda_k1, lambda_q2, lambda_k2, subln_g, rw_w0, rw_w_up, rw_a0, rw_a_up, rw_g_up, rw_v0, rw_v_up, rw_k_k, rw_k_a, rw_r_k, rw_gn_g, rw_gn_b, w_out, ln_mix_g, ln_mix_b, w_up, w_down, ln_ffn_g, ln_ffn_b):
    return _forward(x, ln_in_g, ln_in_b, w_in_first, w_in_rest, mu_first, mu_rest, rel_bias,
                    lambda_q1, lambda_k1, lambda_q2, lambda_k2, subln_g,
                    rw_w0, rw_w_up, rw_a0, rw_a_up, rw_g_up, rw_v0, rw_v_up,
                    rw_k_k, rw_k_a, rw_r_k, rw_gn_g, rw_gn_b,
                    w_out, ln_mix_g, ln_mix_b, w_up, w_down, ln_ffn_g, ln_ffn_b)
```

```python
import functools
import math

import jax
import jax.numpy as jnp
from jax import lax
from jax.experimental import pallas as pl
from jax.experimental.pallas import tpu as pltpu

F32 = jnp.float32
BF16 = jnp.bfloat16

D_MODEL = 1024
DEPTH = 2
D_DIFF = 512
D_RWKV = 512
DIFF_HEADS = 4
DIFF_HEAD_DIM = 64
DIFF_V_DIM = 128
RWKV_HEAD = 64
LORA_W = 64
LORA_A = 64
LORA_V = 32
LORA_G = 160
D_FF = 4 * D_MODEL
N_BUCKETS = 32
MAX_DISTANCE = 128
LN_EPS = 1e-5
SUBLN_EPS = 1e-5
GN_EPS = 64e-5
ALPHA = (2 * DEPTH) ** 0.25
N_DIFF = 3 * D_DIFF
RW_W = 3 * D_RWKV

LANES = 128
MXU_W = 256
VMEM_LIMIT = 56 * 1024 * 1024

RW_PAD = 2048
ATT_T = 512
RW_TB = 512
RW_L = 64
RW_UNROLL = 4
NEG = -1e30


def _cparams(sem):
    return pltpu.CompilerParams(dimension_semantics=sem, vmem_limit_bytes=VMEM_LIMIT)


def _layer_norm(z, g, b):
    mu = jnp.mean(z, -1, keepdims=True)
    zc = z - mu
    var = jnp.mean(zc * zc, -1, keepdims=True)
    return zc * lax.rsqrt(var + LN_EPS) * g + b


def _proj_body(*refs, with_ln, tn, tiles_per_seq):
    if with_ln:
        x_ref, g_ref, b_ref, w_ref, mu_ref, xn_ref, qkv_ref, pr_ref, carry_sc = refs
        xn = _layer_norm(x_ref[...], g_ref[...], b_ref[...])
        xn_ref[...] = xn
    else:
        x_ref, w_ref, mu_ref, qkv_ref, pr_ref, carry_sc = refs
        xn = x_ref[...]
    tm = x_ref.shape[0]
    xb = xn.astype(BF16)
    seq_start = (pl.program_id(0) % tiles_per_seq) == 0
    row8 = lax.broadcasted_iota(jnp.int32, (8, 1), 0)
    for j in range(w_ref.shape[1] // tn):
        o = jnp.dot(xb, w_ref[:, j * tn:(j + 1) * tn], preferred_element_type=F32)
        if (j + 1) * tn <= N_DIFF:
            qkv_ref[:, j * tn:(j + 1) * tn] = o.astype(BF16)
        else:
            cols = slice(j * tn - N_DIFF, (j + 1) * tn - N_DIFF)
            carry = jnp.where(seq_start, 0.0, carry_sc[:, cols])
            rolled = pltpu.roll(o, 1, 0)
            prev = jnp.concatenate([jnp.where(row8 == 0, carry, rolled[:8]), rolled[8:]], axis=0)
            carry_sc[:, cols] = o[tm - 1:tm, :]
            pr_ref[:, cols] = o + (prev - o) * mu_ref[:, cols]


def _in_proj(x, w, mu, ln, seq, tm=512, tn=512):
    t, d = x.shape
    n = w.shape[1]
    with_ln = ln is not None
    row = lambda wd: pl.BlockSpec((tm, wd), lambda i: (i, 0))
    const = lambda r, wd: pl.BlockSpec((r, wd), lambda i: (0, 0))
    in_specs = ([row(d)] + ([const(1, d), const(1, d)] if with_ln else [])
                + [const(d, n), const(1, n - N_DIFF)])
    args = ([x] + ([ln[0].reshape(1, d), ln[1].reshape(1, d)] if with_ln else [])
            + [w, mu.reshape(1, n - N_DIFF)])
    out_shape = [jax.ShapeDtypeStruct((t, N_DIFF), BF16), jax.ShapeDtypeStruct((t, n - N_DIFF), F32)]
    out_specs = [row(N_DIFF), row(n - N_DIFF)]
    if with_ln:
        out_shape = [jax.ShapeDtypeStruct((t, d), F32)] + out_shape
        out_specs = [row(d)] + out_specs
    return pl.pallas_call(
        functools.partial(_proj_body, with_ln=with_ln, tn=tn, tiles_per_seq=seq // tm),
        grid=(t // tm,),
        in_specs=in_specs,
        out_specs=out_specs,
        out_shape=out_shape,
        scratch_shapes=[pltpu.VMEM((1, n - N_DIFF), F32)],
        compiler_params=_cparams(("arbitrary",)),
        name="ln_in_proj" if with_ln else "in_proj",
    )(*args)


def _bias_body(rb_ref, o_ref, *, t):
    which = pl.program_id(0)
    i = lax.broadcasted_iota(jnp.int32, (t, t), 0)
    j = lax.broadcasted_iota(jnp.int32, (t, t), 1)
    n = jnp.maximum(which * t + i - j, 0)
    max_exact = N_BUCKETS // 2
    n_log = N_BUCKETS - max_exact
    steps = [math.ceil(max_exact * (MAX_DISTANCE / max_exact) ** (k / n_log))
             for k in range(1, n_log)]
    bucket = jnp.minimum(n, max_exact)
    for thr in steps:
        bucket = bucket + jnp.where(n >= thr, 1, 0)
    for h in range(DIFF_HEADS):
        acc = jnp.zeros((t, t), F32)
        for b in range(N_BUCKETS):
            acc = jnp.where(bucket == b, rb_ref[b, h], acc)
        o_ref[h, 0] = acc


def _bias_tiles(rel_bias, t):
    return pl.pallas_call(
        functools.partial(_bias_body, t=t),
        grid=(2,),
        in_specs=[pl.BlockSpec(memory_space=pltpu.SMEM)],
        out_specs=pl.BlockSpec((DIFF_HEADS, 1, t, t), lambda w: (0, w, 0, 0)),
        out_shape=jax.ShapeDtypeStruct((DIFF_HEADS, 2, t, t), F32),
        compiler_params=_cparams(("parallel",)),
        name="t5_bias_tiles",
    )(rel_bias)


def _attn_body(rb_ref, lamp_ref, q_ref, k_ref, v_ref, bias_ref, g_ref, o_ref,
               m_sc, acc_sc, *, t, nq, lam_init):
    h = pl.program_id(1)
    lane = lax.broadcasted_iota(jnp.int32, (t, LANES), 1)
    c_far = rb_ref[N_BUCKETS - 1, h]
    lp = lamp_ref[...]
    lam = (jnp.exp(jnp.sum(lp[0:1] * lp[1:2], -1, keepdims=True))
           - jnp.exp(jnp.sum(lp[2:3] * lp[3:4], -1, keepdims=True)) + lam_init)

    def q_tile(iq):
        qrows = slice(iq * t, (iq + 1) * t)
        q = q_ref[qrows, :].astype(F32) * (DIFF_HEAD_DIM ** -0.5)
        qm = (jnp.where(lane < DIFF_HEAD_DIM, q, 0.0).astype(BF16),
              jnp.where(lane < DIFF_HEAD_DIM, 0.0, q).astype(BF16))
        m_sc[...] = jnp.full(m_sc.shape, NEG, F32)
        acc_sc[...] = jnp.zeros(acc_sc.shape, F32)

        def tile(row0, width, bias, diag_col0, qs=slice(0, t)):
            rows = slice(row0, row0 + width)
            kt = k_ref[rows, :]
            vt = jnp.concatenate([v_ref[rows, :], jnp.ones((width, LANES), BF16)], axis=1)
            maps = range(2)
            s = [lax.dot_general(qm[m][qs], kt, (((1,), (1,)), ((), ())), preferred_element_type=F32)
                 for m in maps]
            if bias is None:
                m_cur = [jnp.max(s[m], -1, keepdims=True) + c_far for m in maps]
            else:
                s = [s[m] + bias for m in maps]
                if diag_col0 is not None:
                    nr = qs.stop - qs.start
                    ri = lax.broadcasted_iota(jnp.int32, (nr, width), 0) + qs.start
                    ci = lax.broadcasted_iota(jnp.int32, (nr, width), 1)
                    s = [jnp.where(ri >= ci - diag_col0, s[m], NEG) for m in maps]
                m_cur = [jnp.max(s[m], -1, keepdims=True) for m in maps]
            m_old = [m_sc[m, qs] for m in maps]
            m_new = [jnp.maximum(m_old[m], m_cur[m]) for m in maps]
            a = [jnp.exp(m_old[m] - m_new[m]) for m in maps]
            sub = [m_new[m] - c_far if bias is None else m_new[m] for m in maps]
            p = [jnp.exp(s[m] - jnp.concatenate([sub[m]] * (width // LANES), axis=1)).astype(BF16)
                 for m in maps]
            pv = [jnp.dot(p[m], vt, preferred_element_type=F32) for m in maps]
            for m in maps:
                acc_sc[m, qs] = jnp.concatenate([a[m], a[m]], axis=1) * acc_sc[m, qs] + pv[m]
                m_sc[m, qs] = m_new[m]

        n_far = max(iq - 1, 0)
        row0 = 0
        for width in [4 * t] * (n_far // 4) + [2 * t] * (n_far % 4 // 2):
            tile(row0, width, None, None)
            row0 += width
        def near(row0, width, bias, diag_col0):
            half = t // 2
            tile(row0, width - half, bias[:half, :width - half], diag_col0, slice(0, half))
            tile(row0, width, bias[half:, :], diag_col0, slice(half, t))

        if n_far % 2 == 1:
            near((iq - 2) * t, 3 * t,
                 jnp.concatenate([jnp.full((t, t), c_far, F32), bias_ref[0, 1], bias_ref[0, 0]],
                                 axis=1), 2 * t)
        elif iq > 0:
            near((iq - 1) * t, 2 * t,
                 jnp.concatenate([bias_ref[0, 1], bias_ref[0, 0]], axis=1), t)
        else:
            near(0, t, bias_ref[0, 0], 0)

        a0 = acc_sc[0]
        a1 = acc_sc[1]
        out = a0[:, :LANES] / a0[:, LANES:] - lam * (a1[:, :LANES] / a1[:, LANES:])
        out = out * lax.rsqrt(jnp.mean(out * out, -1, keepdims=True) + SUBLN_EPS) * g_ref[...]
        o_ref[qrows, :] = (out * (1.0 - lam_init)).astype(o_ref.dtype)

    for iq in range(nq):
        pl.when(h >= 0)(functools.partial(q_tile, iq))


def _diff_attention(qkv, bias_tiles, rel_bias, lam_params, subln_g, lam_init, batch, seq):
    t = ATT_T
    hb = D_DIFF // LANES
    return pl.pallas_call(
        functools.partial(_attn_body, t=t, nq=seq // t, lam_init=lam_init),
        grid=(batch, DIFF_HEADS),
        in_specs=[pl.BlockSpec(memory_space=pltpu.SMEM),
                  pl.BlockSpec((4, DIFF_HEAD_DIM), lambda b, h: (0, 0)),
                  pl.BlockSpec((seq, LANES), lambda b, h: (b, h)),
                  pl.BlockSpec((seq, LANES), lambda b, h: (b, hb + h)),
                  pl.BlockSpec((seq, LANES), lambda b, h: (b, 2 * hb + h)),
                  pl.BlockSpec((1, 2, t, t), lambda b, h: (h, 0, 0, 0)),
                  pl.BlockSpec((1, DIFF_V_DIM), lambda b, h: (0, 0))],
        out_specs=pl.BlockSpec((seq, LANES), lambda b, h: (b, h)),
        out_shape=jax.ShapeDtypeStruct((batch * seq, D_DIFF), BF16),
        scratch_shapes=[pltpu.VMEM((2, t, LANES), F32),
                        pltpu.VMEM((2, t, DIFF_V_DIM + LANES), F32)],
        compiler_params=_cparams(("parallel", "parallel")),
        name="diff_attention",
    )(rel_bias, lam_params, qkv, qkv, qkv, bias_tiles, subln_g.reshape(1, DIFF_V_DIM))


def _split_dot(x, m):
    hi = x.astype(BF16)
    lo = (x - hi.astype(F32)).astype(BF16)
    return (jnp.dot(hi, m, preferred_element_type=F32)
            + jnp.dot(lo, m, preferred_element_type=F32))


def _stack_heads(x):
    lane = lax.broadcasted_iota(jnp.int32, x.shape, 1)
    return jnp.concatenate([jnp.where(lane < RWKV_HEAD, x, 0.0),
                            jnp.where(lane < RWKV_HEAD, 0.0, x)], axis=0)


def _bdot(a, b):
    return jnp.dot(a.astype(BF16), b.astype(BF16), preferred_element_type=F32)


def _rwkv_body(*refs, tb, first):
    if first:
        (pr_ref, w0_ref, a0_ref, kk_ref, ka_ref, rk_ref, gng_ref, gnb_ref,
         wup_ref, aup_ref, gup1_ref, gup2_ref,
         y_ref, vout_ref,
         h_sc, r_sc, k_sc, v_sc, as_sc, bs_sc, ci_sc, lw_sc, g_sc, y_sc) = refs
    else:
        (pr_ref, vf_ref, w0_ref, a0_ref, v0_ref, kk_ref, ka_ref, rk_ref, gng_ref, gnb_ref,
         wup_ref, aup_ref, gup1_ref, gup2_ref, vup_ref,
         y_ref,
         h_sc, r_sc, k_sc, v_sc, as_sc, bs_sc, ci_sc, lw_sc, g_sc, y_sc) = refs
    L = RW_L
    c = D_RWKV
    grp_rows = L * RW_UNROLL
    assert grp_rows == MXU_W and tb % grp_rows == 0

    @pl.when(pl.program_id(1) == 0)
    def _():
        h_sc[...] = jnp.zeros(h_sc.shape, F32)

    bi = lax.broadcasted_iota(jnp.int32, (MXU_W, MXU_W), 0)
    bj = lax.broadcasted_iota(jnp.int32, (MXU_W, MXU_W), 1)
    hs = RWKV_HEAD.bit_length() - 1
    head_ones = jnp.where((bi >> hs) == (bj >> hs), 1.0, 0.0).astype(BF16)
    ls_ = L.bit_length() - 1
    tri = jnp.where(((bi >> ls_) == (bj >> ls_)) & (bj <= bi), 1.0, 0.0).astype(BF16)

    def head_sum(x, split=True):
        part = _split_dot if split else _bdot
        return jnp.concatenate([part(x[:, i:i + MXU_W], head_ones)
                                for i in range(0, c, MXU_W)], axis=1)

    def prep(r0):
        rows = slice(r0, r0 + MXU_W)
        r = pr_ref[rows, 0:c]
        k = pr_ref[rows, c:2 * c]
        v = pr_ref[rows, 2 * c:3 * c]
        t1 = pr_ref[rows, RW_W:RW_W + LANES]
        t2 = pr_ref[rows, RW_W + LANES:RW_W + 2 * LANES]
        t3 = pr_ref[rows, RW_W + 2 * LANES:RW_W + 3 * LANES]

        lw = _bdot(jnp.tanh(t1), wup_ref[...])
        la = _bdot(t1, aup_ref[...])
        g_sc[rows, :] = (_bdot(jax.nn.sigmoid(t2), gup1_ref[...])
                         + _bdot(jax.nn.sigmoid(t3), gup2_ref[...]))
        yield

        z = -(w0_ref[...] + lw)
        softplus = jnp.maximum(z, 0.0) + jnp.log(1.0 + jnp.exp(-jnp.abs(z)))
        logw = -jnp.exp(-softplus - 0.5)
        yield
        a = jax.nn.sigmoid(a0_ref[...] + la)
        if first:
            vout_ref[rows, :] = v
        else:
            v = v + (vf_ref[rows, :] - v) * jax.nn.sigmoid(v0_ref[...] + _bdot(t3, vup_ref[...]))
        yield

        kk = k * kk_ref[...]
        kk = kk * jnp.minimum(lax.rsqrt(head_sum(kk * kk)), 1e12)
        yield

        hi = logw.astype(BF16)
        lo = (logw - hi.astype(F32)).astype(BF16)
        ci_sc[rows, :] = (jnp.dot(tri, hi, preferred_element_type=F32)
                          + jnp.dot(tri, lo, preferred_element_type=F32))
        yield
        r_sc[rows, :] = r
        k_sc[rows, :] = k * (1.0 + (a - 1.0) * ka_ref[...])
        v_sc[rows, :] = v
        as_sc[rows, :] = -kk
        bs_sc[rows, :] = kk * a
        lw_sc[rows, :] = logw
        yield

    def post(r0):
        rows = slice(r0, r0 + MXU_W)
        y = y_sc[rows, :]
        mean = head_sum(y) * (1.0 / RWKV_HEAD)
        yield
        yc = y - mean
        var = head_sum(yc * yc, split=False) * (1.0 / RWKV_HEAD)
        yield
        yn = yc * lax.rsqrt(var + GN_EPS) * gng_ref[...] + gnb_ref[...]
        bonus = head_sum(r_sc[rows, :] * k_sc[rows, :] * rk_ref[...], split=False) * v_sc[rows, :]
        yield
        y_ref[rows, :] = ((yn + bonus) * g_sc[rows, :]).astype(y_ref.dtype)
        yield

    i2 = lax.broadcasted_iota(jnp.int32, (4 * L, 4 * L), 0)
    j2 = lax.broadcasted_iota(jnp.int32, (4 * L, 4 * L), 1)
    incl = jnp.where(i2 >= 2 * L, 1, 0)
    amask = ((((i2 >> ls_) & 1) == ((j2 >> ls_) & 1))
             & ((j2 & (L - 1)) < (i2 & (L - 1)) + incl))
    e_i = lax.broadcasted_iota(jnp.int32, (2 * L, 2 * L), 0)
    e_j = lax.broadcasted_iota(jnp.int32, (2 * L, 2 * L), 1)
    eye = e_i == e_j
    eye_f = jnp.where(eye, 1.0, 0.0)

    pairs = range(c // LANES)
    lsl = [slice(p * LANES, (p + 1) * LANES) for p in pairs]
    n_sq = int(math.log2(L)) - 2

    def group_setup(grp, out):
        cins = []
        for j in range(RW_UNROLL):
            ch = grp * RW_UNROLL + j
            rows = slice(ch * L, (ch + 1) * L)
            ci_c = ci_sc[rows, :]
            last = ci_sc[ch * L + L - 1:ch * L + L, :]
            e_neg = jnp.exp(-ci_c)
            e_end = jnp.exp(last - ci_c)
            bs_c = bs_sc[rows, :]
            k_c = k_sc[rows, :]
            cins.append(dict(
                rows=rows, g_end=jnp.exp(last),
                at=as_sc[rows, :] * jnp.exp(ci_c - lw_sc[rows, :]),
                bt=bs_c * e_neg, kt=k_c * e_neg, rt=r_sc[rows, :] * jnp.exp(ci_c),
                bh=bs_c * e_end, kh=k_c * e_end, v=v_sc[rows, :]))
            yield
        combos = [(j, p) for j in range(RW_UNROLL) for p in pairs]
        at_s = [_stack_heads(cins[j]["at"][:, lsl[p]]) for j, p in combos]
        rt_s = [_stack_heads(cins[j]["rt"][:, lsl[p]]) for j, p in combos]
        v_s = [_stack_heads(cins[j]["v"][:, lsl[p]]) for j, p in combos]
        a_all = []
        for i, (j, p) in enumerate(combos):
            lhs = jnp.concatenate([at_s[i], rt_s[i]], axis=0).astype(BF16)
            rhs = jnp.concatenate([_stack_heads(cins[j]["bt"][:, lsl[p]]),
                                   _stack_heads(cins[j]["kt"][:, lsl[p]])], axis=0).astype(BF16)
            a = lax.dot_general(lhs, rhs, (((1,), (1,)), ((), ())), preferred_element_type=F32)
            a_all.append(jnp.where(amask, a, 0.0))
        yield
        ids = range(len(combos))
        a_ab = [a[:2 * L, :2 * L] for a in a_all]
        a_r = [a[2 * L:, :] for a in a_all]
        pw = [_bdot(a, a) for a in a_ab]
        tinv = [eye_f + a for a in a_ab]
        akv = [_bdot(a_all[i][:2 * L, 2 * L:], v_s[i]) for i in ids]
        yield
        for _ in range(n_sq):
            for i in ids:
                pw_b = pw[i].astype(BF16)
                both = jnp.dot(jnp.concatenate([pw_b, tinv[i].astype(BF16)], axis=0), pw_b,
                               preferred_element_type=F32)
                pw[i] = both[:2 * L]
                tinv[i] = tinv[i] + both[2 * L:]
            yield
        tinv = [tinv[i] + _bdot(tinv[i], pw[i]) for i in ids]
        yield
        wu = [_bdot(tinv[i], jnp.concatenate([at_s[i], akv[i]], axis=1)) for i in ids]
        yield
        bk_t = [jnp.concatenate([_stack_heads(cins[j]["bh"][:, lsl[p]]),
                                 _stack_heads(cins[j]["kh"][:, lsl[p]])], axis=0).T.astype(BF16)
                for j, p in combos]
        out.update(cins=cins, wu=wu, rt_s=rt_s, v_s=v_s, a_r=a_r, bk_t=bk_t)
        yield

    def group_state(setup):
        cins, wu, rt_s, v_s, a_r, bk_t = (setup[k] for k in
                                          ("cins", "wu", "rt_s", "v_s", "a_r", "bk_t"))
        h = [h_sc[p] for p in pairs]
        for j in range(RW_UNROLL):
            sel = [j * len(pairs) + p for p in pairs]
            wrg = [_bdot(jnp.concatenate(
                [wu[i][:, :2 * L], rt_s[i], jnp.where(eye, cins[j]["g_end"][:, lsl[p]], 0.0)],
                axis=0), h[p]) for p, i in zip(pairs, sel)]
            yield
            uv = [jnp.concatenate([wrg[p][:2 * L] + wu[i][:, 2 * L:], v_s[i]],
                                  axis=0).astype(BF16) for p, i in zip(pairs, sel)]
            for p, i in zip(pairs, sel):
                y_st = wrg[p][2 * L:4 * L] + jnp.dot(a_r[i].astype(BF16), uv[p],
                                                     preferred_element_type=F32)
                y_sc[cins[j]["rows"], lsl[p]] = y_st[:L] + y_st[L:]
            h = [wrg[p][4 * L:] + jnp.dot(bk_t[i], uv[p], preferred_element_type=F32)
                 for p, i in zip(pairs, sel)]
            yield
        for p in pairs:
            h_sc[p] = h[p]
        yield

    def interleave(*gens):
        gens = list(gens)
        while gens:
            for gen in list(gens):
                try:
                    next(gen)
                except StopIteration:
                    gens.remove(gen)

    n_grp = tb // grp_rows
    setups = [dict() for _ in range(n_grp)]
    interleave(prep(0))
    interleave(group_setup(0, setups[0]), *([prep(grp_rows)] if n_grp > 1 else []))
    for grp in range(n_grp):
        side = []
        if grp + 1 < n_grp:
            side.append(group_setup(grp + 1, setups[grp + 1]))
        if grp + 2 < n_grp:
            side.append(prep((grp + 2) * grp_rows))
        if grp > 0:
            side.append(post((grp - 1) * grp_rows))
        interleave(group_state(setups[grp]), *side)
    interleave(post((n_grp - 1) * grp_rows))


def _rwkv_mix(pr, v_first, prm, batch, seq):
    tb = RW_TB
    nt = seq // tb
    first = v_first is None
    c = D_RWKV
    row_spec = lambda w: pl.BlockSpec((tb, w), lambda b, t: (b * nt + t, 0))
    vec_spec = lambda w: pl.BlockSpec((1, w), lambda b, t: (0, 0))
    mat_spec = lambda: pl.BlockSpec((LANES, c), lambda b, t: (0, 0))
    vec = lambda x: x.reshape(1, -1)
    if first:
        args = [pr, vec(prm["w0"]), vec(prm["a0"]), vec(prm["k_k"]), vec(prm["k_a"]),
                vec(prm["r_k"]), vec(prm["gn_g"]), vec(prm["gn_b"]),
                prm["w_up"], prm["a_up"], prm["g_up1"], prm["g_up2"]]
        in_specs = [row_spec(RW_PAD)] + [vec_spec(c)] * 7 + [mat_spec()] * 4
        out_shape = [jax.ShapeDtypeStruct((batch * seq, c), BF16),
                     jax.ShapeDtypeStruct((batch * seq, c), F32)]
        out_specs = [row_spec(c), row_spec(c)]
    else:
        args = [pr, v_first, vec(prm["w0"]), vec(prm["a0"]), vec(prm["v0"]),
                vec(prm["k_k"]), vec(prm["k_a"]), vec(prm["r_k"]), vec(prm["gn_g"]), vec(prm["gn_b"]),
                prm["w_up"], prm["a_up"], prm["g_up1"], prm["g_up2"], prm["v_up"]]
        in_specs = [row_spec(RW_PAD), row_spec(c)] + [vec_spec(c)] * 8 + [mat_spec()] * 5
        out_shape = jax.ShapeDtypeStruct((batch * seq, c), BF16)
        out_specs = row_spec(c)
    scratch = [pltpu.VMEM((c // LANES, LANES, LANES), F32)]
    scratch += [pltpu.VMEM((tb, c), F32)] * 9
    return pl.pallas_call(
        functools.partial(_rwkv_body, tb=tb, first=first),
        grid=(batch, nt),
        in_specs=in_specs,
        out_specs=out_specs,
        out_shape=out_shape,
        scratch_shapes=scratch,
        compiler_params=_cparams(("parallel", "arbitrary")),
        name="rwkv7_mix_first" if first else "rwkv7_mix_rest",
    )(*args)


def _tail_body(yd_ref, yr_ref, x_ref, wo_ref, gm_ref, bm_ref, wu_ref, wd_ref, gf_ref, bf_ref,
               o_ref, z_sc, *, tf):
    s = pl.program_id(0)

    @pl.when(s == 0)
    def _():
        z_sc[...] = jnp.zeros(z_sc.shape, F32)

    slot = s % 2
    n_f = wu_ref.shape[1] // tf
    qr = x_ref.shape[0] // n_f

    tm = x_ref.shape[0]
    halves = (slice(0, tm // 2), slice(tm // 2, tm))
    mixh = [jnp.dot(yd_ref[r, :], wo_ref[:D_DIFF, :], preferred_element_type=F32)
            + jnp.dot(yr_ref[r, :], wo_ref[D_DIFF:, :], preferred_element_type=F32)
            for r in halves]
    x1h = [_layer_norm(ALPHA * x_ref[r, :] + m, gm_ref[...], bm_ref[...])
           for r, m in zip(halves, mixh)]
    h0 = [jnp.dot(x.astype(BF16), wu_ref[:, :tf], preferred_element_type=F32) for x in x1h]
    x1 = jnp.concatenate(x1h, axis=0)
    xb = x1.astype(BF16)
    acc = ALPHA * x1
    pending = []
    for i in range(n_f):
        f = i * tf
        floor = 0.0
        if pending:
            tie = sum(jnp.sum(p, axis=0, keepdims=True) for p in pending)
            floor = jnp.where(tie != tie, 1.0, 0.0)
            pending = []
        up = (jnp.concatenate(h0, axis=0) if i == 0
              else jnp.dot(xb, wu_ref[:, f:f + tf], preferred_element_type=F32))
        h = jnp.maximum(up, floor)
        acc = acc + jnp.dot((h * h).astype(BF16), wd_ref[f:f + tf, :], preferred_element_type=F32)
        for p in range(n_f):
            if min(p, n_f - 2) == i:
                rows = slice(p * qr, (p + 1) * qr)
                prev = _layer_norm(z_sc[1 - slot, rows, :], gf_ref[...], bf_ref[...])
                o_ref[rows, :] = prev
                pending.append(prev)
    z_sc[slot] = acc


def _layer_tail(yd, yr, x, wo, gm, bm, wu, wd, gf, bf, tm=512, tf=1024):
    t, d = x.shape
    ff = wu.shape[1]
    n = t // tm
    row_in = lambda wd_: pl.BlockSpec((tm, wd_), lambda s: (jnp.minimum(s, n - 1), 0))
    const = lambda r, c: pl.BlockSpec((r, c), lambda s: (0, 0), pipeline_mode=pl.Buffered(1))
    vec = lambda v: v.reshape(1, d)
    return pl.pallas_call(
        functools.partial(_tail_body, tf=tf),
        grid=(n + 1,),
        in_specs=[row_in(D_DIFF), row_in(D_RWKV), row_in(d), const(D_DIFF + D_RWKV, d),
                  const(1, d), const(1, d), const(d, ff), const(ff, d), const(1, d), const(1, d)],
        out_specs=pl.BlockSpec((tm, d), lambda s: (jnp.maximum(s - 1, 0), 0)),
        out_shape=jax.ShapeDtypeStruct((t, d), F32),
        scratch_shapes=[pltpu.VMEM((2, tm, d), F32)],
        compiler_params=_cparams(("arbitrary",)),
        name="out_proj_mlp_norm",
    )(yd, yr, x, wo, vec(gm), vec(bm), wu, wd, vec(gf), vec(bf))


def _pad_rows(w, before, total):
    return jnp.pad(w, ((before, total - before - w.shape[0]), (0, 0)))


def _rwkv_params(l, rw_w0, rw_w_up, rw_a0, rw_a_up, rw_g_up, rw_v0, rw_v_up,
                 rw_k_k, rw_k_a, rw_r_k, rw_gn_g, rw_gn_b):
    prm = {
        "w0": rw_w0[l], "a0": rw_a0[l], "k_k": rw_k_k[l], "k_a": rw_k_a[l],
        "r_k": rw_r_k[l].reshape(-1), "gn_g": rw_gn_g[l], "gn_b": rw_gn_b[l],
        "w_up": _pad_rows(rw_w_up[l], 0, LANES).astype(BF16),
        "a_up": _pad_rows(rw_a_up[l], LORA_W, LANES).astype(BF16),
        "g_up1": rw_g_up[l][:LANES].astype(BF16),
        "g_up2": _pad_rows(rw_g_up[l][LANES:], 0, LANES).astype(BF16),
    }
    if l > 0:
        prm["v0"] = rw_v0[l - 1]
        prm["v_up"] = _pad_rows(rw_v_up[l - 1], LORA_G - LANES, LANES).astype(BF16)
    return prm


def _forward(x, ln_in_g, ln_in_b, w_in_first, w_in_rest, mu_first, mu_rest, rel_bias,
             lambda_q1, lambda_k1, lambda_q2, lambda_k2, subln_g,
             rw_w0, rw_w_up, rw_a0, rw_a_up, rw_g_up, rw_v0, rw_v_up,
             rw_k_k, rw_k_a, rw_r_k, rw_gn_g, rw_gn_b,
             w_out, ln_mix_g, ln_mix_b, w_up, w_down, ln_ffn_g, ln_ffn_b):
    batch, seq, d = x.shape
    xs = x.reshape(batch * seq, d)
    bias_tiles = _bias_tiles(rel_bias, ATT_T)
    v_first = None
    for l in range(DEPTH):
        w_in = w_in_first if l == 0 else w_in_rest[l - 1]
        mu = mu_first if l == 0 else mu_rest[l - 1]
        w_pad = jnp.pad(w_in, ((0, 0), (0, N_DIFF + RW_PAD - w_in.shape[1]))).astype(BF16)
        mu_pad = jnp.pad(mu, (0, RW_PAD - mu.shape[0]))
        if l == 0:
            xs, qkv, pr = _in_proj(xs, w_pad, mu_pad, (ln_in_g, ln_in_b), seq)
        else:
            qkv, pr = _in_proj(xs, w_pad, mu_pad, None, seq)

        lam_init = 0.8 - 0.6 * math.exp(-0.3 * l)
        lam_params = jnp.stack([lambda_q1[l], lambda_k1[l], lambda_q2[l], lambda_k2[l]])
        y_diff = _diff_attention(qkv, bias_tiles, rel_bias, lam_params, subln_g[l], lam_init, batch, seq)

        prm = _rwkv_params(l, rw_w0, rw_w_up, rw_a0, rw_a_up, rw_g_up, rw_v0, rw_v_up,
                           rw_k_k, rw_k_a, rw_r_k, rw_gn_g, rw_gn_b)
        if l == 0:
            y_rw, v_first = _rwkv_mix(pr, None, prm, batch, seq)
        else:
            y_rw = _rwkv_mix(pr, v_first, prm, batch, seq)

        xs = _layer_tail(y_diff, y_rw, xs, w_out[l].astype(BF16), ln_mix_g[l], ln_mix_b[l],
                         w_up[l].astype(BF16), w_down[l].astype(BF16), ln_ffn_g[l], ln_ffn_b[l])
    return xs.reshape(batch, seq, d)


def kernel(x, ln_in_g, ln_in_b, w_in_first, w_in_rest, mu_first, mu_rest, rel_bias, lambda_q1, lambda_k1, lambda_q2, lambda_k2, subln_g, rw_w0, rw_w_up, rw_a0, rw_a_up, rw_g_up, rw_v0, rw_v_up, rw_k_k, rw_k_a, rw_r_k, rw_gn_g, rw_gn_b, w_out, ln_mix_g, ln_mix_b, w_up, w_down, ln_ffn_g, ln_ffn_b):
    return _forward(x, ln_in_g, ln_in_b, w_in_first, w_in_rest, mu_first, mu_rest, rel_bias,
                    lambda_q1, lambda_k1, lambda_q2, lambda_k2, subln_g,
                    rw_w0, rw_w_up, rw_a0, rw_a_up, rw_g_up, rw_v0, rw_v_up,
                    rw_k_k, rw_k_a, rw_r_k, rw_gn_g, rw_gn_b,
                    w_out, ln_mix_g, ln_mix_b, w_up, w_down, ln_ffn_g, ln_ffn_b)
```
